```python
import math
import jax, jax.numpy as jnp
from jax import lax
import numpy as np

D_MODEL = 4096
BATCH = 4
SEQ = 4096
DEPTH = 4

N_META = 16
D_MIX = D_MODEL
MLA_HEADS = D_MODEL // 256
MLA_NOPE = 128
MLA_ROPE = 64
MLA_V = 128
Q_LORA = D_MODEL // 4
KV_LORA = D_MODEL // 8
SB_HEADS = D_MODEL // 256
SB_HEAD_DIM = 128
W_MLA = MLA_HEADS * MLA_V
W_SB = SB_HEADS * SB_HEAD_DIM
BLOCK_Q = 128
ROPE_THETA = 10000.0
EPS = 1e-6
MLA_SCALE = 1.0 / math.sqrt(MLA_NOPE + MLA_ROPE)
SB_SCALE = 1.0 / math.sqrt(SB_HEAD_DIM)
IN_SIZES = (Q_LORA, KV_LORA, MLA_ROPE, W_MLA, W_SB, W_SB, W_SB, W_SB)
D_IN = Q_LORA + KV_LORA + MLA_ROPE + W_MLA + 4 * W_SB

kernel_name = "hymba_mla_stickbreaking_hybrid"


def rms_norm(x, g):
    xf = x.astype(jnp.float32)
    y = xf * lax.rsqrt(jnp.mean(xf * xf, axis=-1, keepdims=True) + EPS)
    return (y * g.astype(jnp.float32)).astype(x.dtype)


def rope_tables(pos, dtype):
    inv_freq = ROPE_THETA ** (-jnp.arange(0, MLA_ROPE, 2, dtype=jnp.float32) / MLA_ROPE)
    ang = pos.astype(jnp.float32)[:, None] * inv_freq[None, :]
    return jnp.cos(ang).astype(dtype), jnp.sin(ang).astype(dtype)


def apply_rope(x, cos, sin):
    half = x.shape[-1] // 2
    x1, x2 = x[..., :half], x[..., half:]
    return jnp.concatenate([x1 * cos - x2 * sin, x2 * cos + x1 * sin], axis=-1)


def mla_block(q_nope, q_rope, k_nope, k_rope, v, q_pos, k_pos):
    s = (jnp.einsum('bqhd,bkhd->bhqk', q_nope, k_nope, preferred_element_type=jnp.float32)
         + jnp.einsum('bqhr,bkr->bhqk', q_rope, k_rope, preferred_element_type=jnp.float32))
    s = s * MLA_SCALE
    causal = k_pos[None, :] <= q_pos[:, None]
    s = jnp.where(causal, s, -jnp.inf)
    p = jax.nn.softmax(s, axis=-1)
    return jnp.einsum('bhqk,bkhd->bqhd', p.astype(v.dtype), v)


def sb_block(q, k, v, q_pos, k_pos):
    z = jnp.einsum('bqhd,bkhd->bhqk', q, k, preferred_element_type=jnp.float32) * SB_SCALE
    strict = k_pos[None, :] < q_pos[:, None]
    log_beta = jax.nn.log_sigmoid(z)
    log_one_minus = jnp.where(strict, jax.nn.log_sigmoid(-z), 0.0)
    log_remaining = lax.cumsum(log_one_minus, axis=3, reverse=True) - log_one_minus
    a = jnp.where(strict, jnp.exp(log_beta + log_remaining), 0.0)
    return jnp.einsum('bhqk,bkhd->bqhd', a.astype(v.dtype), v)


def blocked_attention(block_fn, qs, kvs, pos):
    B, L = qs[0].shape[0], qs[0].shape[1]
    meta_out = block_fn(*[a[:, :N_META] for a in qs], *[a[:, :N_META] for a in kvs],
                        pos[:N_META], pos[:N_META])
    n_blk = (L - N_META) // BLOCK_Q

    def to_blocks(a):
        r = a[:, N_META:].reshape((B, n_blk, BLOCK_Q) + a.shape[2:])
        return jnp.moveaxis(r, 1, 0)

    q_blocks = tuple(to_blocks(a) for a in qs)
    pos_blocks = pos[N_META:].reshape(n_blk, BLOCK_Q)

    def body(args):
        *qb, pb = args
        return block_fn(*qb, *kvs, pb, pos)

    out = lax.map(body, (*q_blocks, pos_blocks))
    out = jnp.moveaxis(out, 0, 1).reshape((B, L - N_META) + out.shape[3:])
    return jnp.concatenate([meta_out, out], axis=1)


def hybrid_layer(h, g_norm, w_in, g_q, g_kv, w_uq, w_ukv, g_out_mla, g_out_sb, w_o,
                 cos, sin, pos):
    B, L, _ = h.shape
    u = rms_norm(h, g_norm)
    proj = u @ w_in
    split_points = tuple(int(v) for v in np.cumsum(IN_SIZES)[:-1])
    c_q, c_kv, k_r, z_mla, q_sb, k_sb, v_sb, z_sb = jnp.split(proj, split_points, axis=-1)

    q = (rms_norm(c_q, g_q) @ w_uq).reshape(B, L, MLA_HEADS, MLA_NOPE + MLA_ROPE)
    q_nope = q[..., :MLA_NOPE]
    q_rope = apply_rope(q[..., MLA_NOPE:], cos[:, None, :], sin[:, None, :])
    kv = (rms_norm(c_kv, g_kv) @ w_ukv).reshape(B, L, MLA_HEADS, MLA_NOPE + MLA_V)
    k_nope, v_mla = kv[..., :MLA_NOPE], kv[..., MLA_NOPE:]
    k_rope = apply_rope(k_r, cos, sin)
    y_mla = blocked_attention(mla_block, (q_nope, q_rope), (k_nope, k_rope, v_mla), pos)
    y_mla = y_mla.reshape(B, L, W_MLA)

    q_s = q_sb.reshape(B, L, SB_HEADS, SB_HEAD_DIM)
    k_s = k_sb.reshape(B, L, SB_HEADS, SB_HEAD_DIM)
    v_s = v_sb.reshape(B, L, SB_HEADS, SB_HEAD_DIM)
    y_sb = blocked_attention(sb_block, (q_s,), (k_s, v_s), pos).reshape(B, L, W_SB)

    y = jnp.concatenate([rms_norm(y_mla, g_out_mla) * jax.nn.silu(z_mla),
                         rms_norm(y_sb, g_out_sb) * jax.nn.silu(z_sb)], axis=-1)
    return h + y @ w_o


def setup_inputs(seed: int = 0) -> dict:
    key = jax.random.key(seed)
    ks = jax.random.split(key, 13)
    f32 = jnp.float32
    x = jax.random.normal(ks[0], (BATCH, SEQ, D_MODEL), f32)
    meta_tokens = jax.random.normal(ks[1], (N_META, D_MODEL), f32)
    g_norm = 1.0 + 0.02 * jax.random.normal(ks[2], (DEPTH, D_MODEL), f32)
    w_in = jax.random.normal(ks[3], (DEPTH, D_MODEL, D_IN), f32) * D_MODEL ** -0.5
    g_q = 1.0 + 0.02 * jax.random.normal(ks[4], (DEPTH, Q_LORA), f32)
    g_kv = 1.0 + 0.02 * jax.random.normal(ks[5], (DEPTH, KV_LORA), f32)
    w_uq = jax.random.normal(ks[6], (DEPTH, Q_LORA, MLA_HEADS * (MLA_NOPE + MLA_ROPE)), f32) * Q_LORA ** -0.5
    w_ukv = jax.random.normal(ks[7], (DEPTH, KV_LORA, MLA_HEADS * (MLA_NOPE + MLA_V)), f32) * KV_LORA ** -0.5
    g_out_mla = 1.0 + 0.02 * jax.random.normal(ks[8], (DEPTH, W_MLA), f32)
    g_out_sb = 1.0 + 0.02 * jax.random.normal(ks[9], (DEPTH, W_SB), f32)
    w_o = jax.random.normal(ks[10], (DEPTH, D_MIX, D_MODEL), f32) * D_MIX ** -0.5
    g_final = 1.0 + 0.02 * jax.random.normal(ks[11], (D_MODEL,), f32)
    return {"x": x, "meta_tokens": meta_tokens, "g_norm": g_norm, "w_in": w_in,
            "g_q": g_q, "g_kv": g_kv, "w_uq": w_uq, "w_ukv": w_ukv,
            "g_out_mla": g_out_mla, "g_out_sb": g_out_sb, "w_o": w_o,
            "g_final": g_final}


def reference(x, meta_tokens, g_norm, w_in, g_q, g_kv, w_uq, w_ukv, g_out_mla, g_out_sb,
              w_o, g_final):
    B = x.shape[0]
    meta = jnp.broadcast_to(meta_tokens[None].astype(x.dtype), (B, N_META, D_MODEL))
    h = jnp.concatenate([meta, x], axis=1)
    L = h.shape[1]
    pos = jnp.arange(L, dtype=jnp.int32)
    cos, sin = rope_tables(pos, x.dtype)
    for i in range(DEPTH):
        h = hybrid_layer(h, g_norm[i], w_in[i], g_q[i], g_kv[i], w_uq[i], w_ukv[i],
                         g_out_mla[i], g_out_sb[i], w_o[i], cos, sin, pos)
    return rms_norm(h[:, N_META:], g_final)
```

```python
import functools
import math

import jax
import jax.numpy as jnp
from jax import lax
from jax.experimental import pallas as pl
from jax.experimental.pallas import tpu as pltpu

N_META = 16
NOPE = 128
ROPE = 64
VDIM = 128
QK_CAT = 256
ROPE_THETA = 10000.0
EPS = 1e-6
MLA_SCALE = 1.0 / math.sqrt(NOPE + ROPE)
SB_SCALE = 1.0 / math.sqrt(VDIM)
LANE = 128
BF16_SUBLANE = 16
ATT_BLK = 256
VMEM_LIMIT = 56 * 1024 * 1024
NEG_BIG = -1e30

_f32 = jnp.float32
_bf16 = jnp.bfloat16


def _pick_tile(n, target):
    best = None
    for t in range(BF16_SUBLANE, min(n, target) + 1, BF16_SUBLANE):
        if n % t == 0:
            best = t
    assert best is not None, (n, target)
    return best


def _params(*sem):
    return pltpu.CompilerParams(dimension_semantics=sem, vmem_limit_bytes=VMEM_LIMIT)


def _rms(xf, g):
    return xf * lax.rsqrt(jnp.mean(xf * xf, axis=-1, keepdims=True) + EPS) * g


def _rmsnorm_kernel(x_ref, g_ref, o_ref):
    o_ref[...] = _rms(x_ref[...].astype(_f32), g_ref[...]).astype(o_ref.dtype)


def _rmsnorm(x, g, out_dtype, tm):
    m, d = x.shape
    return pl.pallas_call(
        _rmsnorm_kernel,
        grid=(m // tm,),
        in_specs=[pl.BlockSpec((tm, d), lambda i: (i, 0)),
                  pl.BlockSpec((1, d), lambda i: (0, 0))],
        out_specs=pl.BlockSpec((tm, d), lambda i: (i, 0)),
        out_shape=jax.ShapeDtypeStruct((m, d), out_dtype),
        compiler_params=_params("parallel"),
        name="rmsnorm",
    )(x, g.reshape(1, d))


def _mm_scale_kernel(x_ref, w_ref, s_ref, o_ref):
    acc = jnp.dot(x_ref[...], w_ref[...], preferred_element_type=_f32)
    o_ref[...] = (acc * s_ref[...]).astype(o_ref.dtype)


def _matmul_colscale(x, w, colscale, out_dtype, tm, tn):
    m, k = x.shape
    n = w.shape[1]
    return pl.pallas_call(
        _mm_scale_kernel,
        grid=(n // tn, m // tm),
        in_specs=[pl.BlockSpec((tm, k), lambda j, i: (i, 0)),
                  pl.BlockSpec((k, tn), lambda j, i: (0, j)),
                  pl.BlockSpec((1, tn), lambda j, i: (0, j))],
        out_specs=pl.BlockSpec((tm, tn), lambda j, i: (i, j)),
        out_shape=jax.ShapeDtypeStruct((m, n), out_dtype),
        compiler_params=_params("parallel", "parallel"),
        name="matmul_colscale",
    )(x, w, colscale)


def _mm_kernel(x_ref, w_ref, o_ref):
    o_ref[...] = jnp.dot(x_ref[...], w_ref[...], preferred_element_type=_f32).astype(o_ref.dtype)


def _matmul_small(x, w, tm):
    m, k = x.shape
    n = w.shape[1]
    return pl.pallas_call(
        _mm_kernel,
        grid=(m // tm,),
        in_specs=[pl.BlockSpec((tm, k), lambda i: (i, 0)),
                  pl.BlockSpec((k, n), lambda i: (0, 0))],
        out_specs=pl.BlockSpec((tm, n), lambda i: (i, 0)),
        out_shape=jax.ShapeDtypeStruct((m, n), _f32),
        compiler_params=_params("parallel"),
        name="matmul_small",
    )(x, w)


def _mm_res_kernel(x_ref, w_ref, r_ref, o_ref):
    o_ref[...] = r_ref[...] + jnp.dot(x_ref[...], w_ref[...], preferred_element_type=_f32)


def _matmul_residual(x, w, res, tm, tn):
    m, k = x.shape
    n = w.shape[1]
    return pl.pallas_call(
        _mm_res_kernel,
        grid=(n // tn, m // tm),
        in_specs=[pl.BlockSpec((tm, k), lambda j, i: (i, 0)),
                  pl.BlockSpec((k, tn), lambda j, i: (0, j)),
                  pl.BlockSpec((tm, tn), lambda j, i: (i, j))],
        out_specs=pl.BlockSpec((tm, tn), lambda j, i: (i, j)),
        out_shape=jax.ShapeDtypeStruct((m, n), _f32),
        compiler_params=_params("parallel", "parallel"),
        name="matmul_residual",
    )(x, w, res)


def _rope128(x, cos, sin_signed):
    lane = lax.broadcasted_iota(jnp.int32, x.shape, 1)
    partner = jnp.where(lane < ROPE // 2,
                        pltpu.roll(x, LANE - ROPE // 2, 1),
                        pltpu.roll(x, ROPE // 2, 1))
    return x * cos + partner * sin_signed


def _qproj_kernel(x_ref, g_ref, w_ref, cos_ref, sin_ref, o_ref, xn_ref, *, heads_per_step):
    @pl.when(pl.program_id(1) == 0)
    def _():
        xn_ref[...] = _rms(x_ref[...], g_ref[...]).astype(_bf16)

    acc = jnp.dot(xn_ref[...], w_ref[...], preferred_element_type=_f32)
    cos = cos_ref[...]
    sin = sin_ref[...]
    for hh in range(heads_per_step):
        c0 = hh * QK_CAT
        o_ref[:, c0:c0 + NOPE] = (acc[:, c0:c0 + NOPE] * MLA_SCALE).astype(o_ref.dtype)
        roped = _rope128(acc[:, c0 + NOPE:c0 + QK_CAT], cos, sin)
        o_ref[:, c0 + NOPE:c0 + QK_CAT] = (roped * MLA_SCALE).astype(o_ref.dtype)


def _q_proj(small, g_q, w_q, cos, sin, lp, tm, ql):
    m = small.shape[0]
    n = w_q.shape[1]
    hps = 2 if (n // QK_CAT) % 2 == 0 else 1
    tn = hps * QK_CAT
    nrow = lp // tm
    return pl.pallas_call(
        functools.partial(_qproj_kernel, heads_per_step=hps),
        grid=(m // tm, n // tn),
        in_specs=[pl.BlockSpec((tm, ql), lambda i, j: (i, 0)),
                  pl.BlockSpec((1, ql), lambda i, j: (0, 0)),
                  pl.BlockSpec((ql, tn), lambda i, j: (0, j)),
                  pl.BlockSpec((tm, LANE), lambda i, j: (i % nrow, 0)),
                  pl.BlockSpec((tm, LANE), lambda i, j: (i % nrow, 0))],
        out_specs=pl.BlockSpec((tm, tn), lambda i, j: (i, j)),
        out_shape=jax.ShapeDtypeStruct((m, n), _bf16),
        scratch_shapes=[pltpu.VMEM((tm, ql), _bf16)],
        compiler_params=_params("parallel", "arbitrary"),
        name="q_proj",
    )(small, g_q.reshape(1, ql), w_q, cos, sin)


def _kvproj_kernel(x_ref, kr_ref, g_ref, wk_ref, wv_ref, cos_ref, sin_ref,
                   k_ref, v_ref, xn_ref, krr_ref, *, heads_per_step):
    @pl.when(pl.program_id(1) == 0)
    def _():
        xn_ref[...] = _rms(x_ref[...], g_ref[...]).astype(_bf16)
        krr_ref[...] = _rope128(kr_ref[...], cos_ref[...], sin_ref[...]).astype(_bf16)

    xn = xn_ref[...]
    kn = jnp.dot(xn, wk_ref[...], preferred_element_type=_f32)
    v_ref[...] = jnp.dot(xn, wv_ref[...], preferred_element_type=_f32).astype(v_ref.dtype)
    krr = krr_ref[...]
    for hh in range(heads_per_step):
        k_ref[:, hh * QK_CAT:hh * QK_CAT + NOPE] = kn[:, hh * NOPE:(hh + 1) * NOPE].astype(k_ref.dtype)
        k_ref[:, hh * QK_CAT + NOPE:(hh + 1) * QK_CAT] = krr


def _kv_proj(small, g_kv, w_k, w_v, cos, sin, lp, tm, ql, kvl):
    m = small.shape[0]
    heads = w_k.shape[1] // NOPE
    hps = 2 if heads % 2 == 0 else 1
    nrow = lp // tm
    return pl.pallas_call(
        functools.partial(_kvproj_kernel, heads_per_step=hps),
        grid=(m // tm, heads // hps),
        in_specs=[pl.BlockSpec((tm, kvl), lambda i, j: (i, ql // kvl)),
                  pl.BlockSpec((tm, LANE), lambda i, j: (i, (ql + kvl) // LANE)),
                  pl.BlockSpec((1, kvl), lambda i, j: (0, 0)),
                  pl.BlockSpec((kvl, hps * NOPE), lambda i, j: (0, j)),
                  pl.BlockSpec((kvl, hps * VDIM), lambda i, j: (0, j)),
                  pl.BlockSpec((tm, LANE), lambda i, j: (i % nrow, 0)),
                  pl.BlockSpec((tm, LANE), lambda i, j: (i % nrow, 0))],
        out_specs=[pl.BlockSpec((tm, hps * QK_CAT), lambda i, j: (i, j)),
                   pl.BlockSpec((tm, hps * VDIM), lambda i, j: (i, j))],
        out_shape=[jax.ShapeDtypeStruct((m, heads * QK_CAT), _bf16),
                   jax.ShapeDtypeStruct((m, heads * VDIM), _bf16)],
        scratch_shapes=[pltpu.VMEM((tm, kvl), _bf16), pltpu.VMEM((tm, LANE), _bf16)],
        compiler_params=_params("parallel", "arbitrary"),
        name="kv_proj",
    )(small, small, g_kv.reshape(1, kvl), w_k, w_v, cos, sin)


def _dot_nt(a, b):
    return lax.dot_general(a, b, (((1,), (1,)), ((), ())), preferred_element_type=_f32)


def _row_blocks(lp):
    n_main = lp // ATT_BLK
    tail = lp - n_main * ATT_BLK
    return n_main, tail


def _mla_kernel(q_ref, k_ref, v_ref, o_ref, m_ref, l_ref, acc_ref, *, lp):
    n_main, tail = _row_blocks(lp)

    def q_block(q0, bq, n_full):
        q = q_ref[pl.ds(q0, bq), :]
        m_ref[0:bq, :] = jnp.full((bq, 1), NEG_BIG, _f32)
        l_ref[0:bq, :] = jnp.zeros((bq, 1), _f32)
        acc_ref[0:bq, :] = jnp.zeros((bq, VDIM), _f32)

        def step(k0, bk, masked):
            s = _dot_nt(q, k_ref[pl.ds(k0, bk), :])
            if masked:
                row = lax.broadcasted_iota(jnp.int32, (bq, bk), 0)
                col = lax.broadcasted_iota(jnp.int32, (bq, bk), 1)
                s = jnp.where(col <= row, s, NEG_BIG)
            m_prev = m_ref[0:bq, :]
            m_new = jnp.maximum(m_prev, jnp.max(s, axis=-1, keepdims=True))
            alpha = jnp.exp(m_prev - m_new)
            p = jnp.exp(s - m_new)
            l_ref[0:bq, :] = alpha * l_ref[0:bq, :] + jnp.sum(p, axis=-1, keepdims=True)
            acc_ref[0:bq, :] = alpha * acc_ref[0:bq, :] + jnp.dot(
                p.astype(_bf16), v_ref[pl.ds(k0, bk), :], preferred_element_type=_f32)
            m_ref[0:bq, :] = m_new

        def full_step(j, carry):
            step(pl.multiple_of(j * ATT_BLK, ATT_BLK), ATT_BLK, False)
            return carry

        lax.fori_loop(0, n_full, full_step, 0)
        step(q0, bq, True)
        o_ref[pl.ds(q0, bq), :] = (acc_ref[0:bq, :] / l_ref[0:bq, :]).astype(o_ref.dtype)

    def main_block(i, carry):
        q_block(pl.multiple_of(i * ATT_BLK, ATT_BLK), ATT_BLK, i)
        return carry

    lax.fori_loop(0, n_main, main_block, 0)
    if tail:
        q_block(n_main * ATT_BLK, tail, n_main)


def _sb_kernel(q_ref, k_ref, v_ref, o_ref, r_ref, acc_ref, *, lp):
    n_main, tail = _row_blocks(lp)
    tri_r = lax.broadcasted_iota(jnp.int32, (ATT_BLK, ATT_BLK), 0)
    tri_c = lax.broadcasted_iota(jnp.int32, (ATT_BLK, ATT_BLK), 1)
    later = (tri_r > tri_c).astype(_bf16)

    def q_block(q0, bq, n_full):
        q = q_ref[pl.ds(q0, bq), :]
        r_ref[0:bq, :] = jnp.zeros((bq, 1), _f32)
        acc_ref[0:bq, :] = jnp.zeros((bq, VDIM), _f32)

        def step(k0, bk, masked):
            z = _dot_nt(q, k_ref[pl.ds(k0, bk), :])
            tail_sp = jnp.log(1.0 + jnp.exp(-jnp.abs(z)))
            log_beta = jnp.minimum(z, 0.0) - tail_sp
            log_rest = -jnp.maximum(z, 0.0) - tail_sp
            if masked:
                row = lax.broadcasted_iota(jnp.int32, (bq, bk), 0)
                col = lax.broadcasted_iota(jnp.int32, (bq, bk), 1)
                strict = col < row
                log_rest = jnp.where(strict, log_rest, 0.0)
            within = jnp.dot(log_rest.astype(_bf16), later[0:bk, 0:bk], preferred_element_type=_f32)
            a = jnp.exp(log_beta + within + r_ref[0:bq, :])
            if masked:
                a = jnp.where(strict, a, 0.0)
            acc_ref[0:bq, :] = acc_ref[0:bq, :] + jnp.dot(
                a.astype(_bf16), v_ref[pl.ds(k0, bk), :], preferred_element_type=_f32)
            r_ref[0:bq, :] = r_ref[0:bq, :] + jnp.sum(log_rest, axis=-1, keepdims=True)

        step(q0, bq, True)

        def full_step(jj, carry):
            j = n_full - 1 - jj
            step(pl.multiple_of(j * ATT_BLK, ATT_BLK), ATT_BLK, False)
            return carry

        lax.fori_loop(0, n_full, full_step, 0)
        o_ref[pl.ds(q0, bq), :] = acc_ref[0:bq, :].astype(o_ref.dtype)

    def main_block(i, carry):
        q_block(pl.multiple_of(i * ATT_BLK, ATT_BLK), ATT_BLK, i)
        return carry

    lax.fori_loop(0, n_main, main_block, 0)
    if tail:
        q_block(n_main * ATT_BLK, tail, n_main)


def _mla_attention(q_cat, k_cat, v, heads):
    b, lp, _ = q_cat.shape
    qk_spec = pl.BlockSpec((None, lp, QK_CAT), lambda bi, h: (bi, 0, h))
    v_spec = pl.BlockSpec((None, lp, VDIM), lambda bi, h: (bi, 0, h))
    return pl.pallas_call(
        functools.partial(_mla_kernel, lp=lp),
        grid=(b, heads),
        in_specs=[qk_spec, qk_spec, v_spec],
        out_specs=v_spec,
        out_shape=jax.ShapeDtypeStruct((b, lp, heads * VDIM), _f32),
        scratch_shapes=[pltpu.VMEM((ATT_BLK, 1), _f32), pltpu.VMEM((ATT_BLK, 1), _f32),
                        pltpu.VMEM((ATT_BLK, VDIM), _f32)],
        compiler_params=_params("parallel", "parallel"),
        name="mla_attention",
    )(q_cat, k_cat, v)


def _sb_attention(big, heads, q_col, k_col, v_col):
    b, lp, _ = big.shape

    def spec(col0):
        return pl.BlockSpec((None, lp, VDIM), lambda bi, h: (bi, 0, col0 // VDIM + h))

    return pl.pallas_call(
        functools.partial(_sb_kernel, lp=lp),
        grid=(b, heads),
        in_specs=[spec(q_col), spec(k_col), spec(v_col)],
        out_specs=pl.BlockSpec((None, lp, VDIM), lambda bi, h: (bi, 0, h)),
        out_shape=jax.ShapeDtypeStruct((b, lp, heads * VDIM), _f32),
        scratch_shapes=[pltpu.VMEM((ATT_BLK, 1), _f32), pltpu.VMEM((ATT_BLK, VDIM), _f32)],
        compiler_params=_params("parallel", "parallel"),
        name="sb_attention",
    )(big, big, big)


def _gate_kernel(ym_ref, ys_ref, zm_ref, zs_ref, gm_ref, gs_ref, o_ref, *, w):
    def one(y_ref, z_ref, g_ref):
        z = z_ref[...].astype(_f32)
        silu = z / (1.0 + jnp.exp(-z))
        return (_rms(y_ref[...], g_ref[...]) * silu).astype(o_ref.dtype)

    o_ref[:, 0:w] = one(ym_ref, zm_ref, gm_ref)
    o_ref[:, w:2 * w] = one(ys_ref, zs_ref, gs_ref)


def _gate(y_mla, y_sb, big, g_mla, g_sb, zm_col, zs_col, tm):
    m, w = y_mla.shape
    yspec = pl.BlockSpec((tm, w), lambda i: (i, 0))
    gspec = pl.BlockSpec((1, w), lambda i: (0, 0))
    return pl.pallas_call(
        functools.partial(_gate_kernel, w=w),
        grid=(m // tm,),
        in_specs=[yspec, yspec,
                  pl.BlockSpec((tm, w), lambda i: (i, zm_col // w)),
                  pl.BlockSpec((tm, w), lambda i: (i, zs_col // w)),
                  gspec, gspec],
        out_specs=pl.BlockSpec((tm, 2 * w), lambda i: (i, 0)),
        out_shape=jax.ShapeDtypeStruct((m, 2 * w), _bf16),
        compiler_params=_params("parallel"),
        name="gate",
    )(y_mla, y_sb, big, big, g_mla.reshape(1, w), g_sb.reshape(1, w))


def _rope_tables(lp):
    inv_freq = ROPE_THETA ** (-jnp.arange(0, ROPE, 2, dtype=_f32) / ROPE)
    ang = jnp.arange(lp, dtype=jnp.int32).astype(_f32)[:, None] * inv_freq[None, :]
    cos, sin = jnp.cos(ang), jnp.sin(ang)
    zeros = jnp.zeros((lp, LANE - ROPE), _f32)
    return (jnp.concatenate([cos, cos, zeros], axis=1),
            jnp.concatenate([-sin, sin, zeros], axis=1))


@jax.jit
def _forward(x, meta_tokens, g_norm, w_in, g_q, g_kv, w_uq, w_ukv, g_out_mla, g_out_sb,
             w_o, g_final):
    b, seq, d = x.shape
    depth = w_in.shape[0]
    ql, kvl = g_q.shape[1], g_kv.shape[1]
    heads = w_uq.shape[2] // (NOPE + ROPE)
    w_grp = heads * VDIM
    assert w_ukv.shape[2] == heads * (NOPE + VDIM)
    assert w_in.shape[2] == ql + kvl + ROPE + 5 * w_grp
    assert ql % kvl == 0 and (ql + kvl) % LANE == 0 and kvl % LANE == 0

    l_real = N_META + seq
    lp = -(-l_real // LANE) * LANE
    mp = b * lp

    n_small = ql + kvl + ROPE
    w_small = jnp.pad(w_in[:, :, :n_small], ((0, 0), (0, 0), (0, LANE - ROPE))).astype(_bf16)
    w_big = w_in[:, :, n_small:].astype(_bf16)
    zm_col, qs_col, ks_col, vs_col, zs_col = (i * w_grp for i in range(5))
    colscale = jnp.ones((1, 5 * w_grp), _f32).at[:, qs_col:qs_col + w_grp].set(SB_SCALE)
    wq = w_uq.reshape(depth, ql, heads, NOPE + ROPE)
    wq = jnp.pad(wq, ((0, 0), (0, 0), (0, 0), (0, QK_CAT - NOPE - ROPE)))
    wq = wq.reshape(depth, ql, heads * QK_CAT).astype(_bf16)
    wkv = w_ukv.reshape(depth, kvl, heads, NOPE + VDIM)
    wk = wkv[..., :NOPE].reshape(depth, kvl, heads * NOPE).astype(_bf16)
    wv = wkv[..., NOPE:].reshape(depth, kvl, heads * VDIM).astype(_bf16)
    wo = w_o.astype(_bf16)
    cos, sin = _rope_tables(lp)

    meta = jnp.broadcast_to(meta_tokens[None].astype(x.dtype), (b, N_META, d))
    h = jnp.concatenate([meta, x, jnp.zeros((b, lp - l_real, d), x.dtype)], axis=1)
    h = h.reshape(mp, d)

    tm_norm = _pick_tile(lp, 384)
    tm_mm = _pick_tile(lp, 528)
    tm_proj = _pick_tile(lp, 1056)
    tn_big = 1024 if (5 * w_grp) % 1024 == 0 else 512
    tn_out = 1024 if d % 1024 == 0 else 512

    for i in range(depth):
        u = _rmsnorm(h, g_norm[i], _bf16, tm_norm)
        small = _matmul_small(u, w_small[i], tm_mm)
        big = _matmul_colscale(u, w_big[i], colscale, _bf16, tm_mm, tn_big)
        q_cat = _q_proj(small, g_q[i], wq[i], cos, sin, lp, tm_proj, ql)
        k_cat, v_mla = _kv_proj(small, g_kv[i], wk[i], wv[i], cos, sin, lp, tm_proj, ql, kvl)
        y_mla = _mla_attention(q_cat.reshape(b, lp, -1), k_cat.reshape(b, lp, -1),
                               v_mla.reshape(b, lp, -1), heads)
        y_sb = _sb_attention(big.reshape(b, lp, -1), heads, qs_col, ks_col, vs_col)
        y = _gate(y_mla.reshape(mp, w_grp), y_sb.reshape(mp, w_grp), big,
                  g_out_mla[i], g_out_sb[i], zm_col, zs_col, tm_norm)
        h = _matmul_residual(y, wo[i], h, tm_mm, tn_out)

    h_real = h.reshape(b, lp, d)[:, N_META:l_real].reshape(b * seq, d)
    out = _rmsnorm(h_real, g_final, x.dtype, _pick_tile(seq, 512))
    return out.reshape(b, seq, d)


def kernel(x, meta_tokens, g_norm, w_in, g_q, g_kv, w_uq, w_ukv, g_out_mla, g_out_sb, w_o, g_final):
    return _forward(x, meta_tokens, g_norm, w_in, g_q, g_kv, w_uq, w_ukv, g_out_mla, g_out_sb,
                    w_o, g_final)
```

```python
import functools
import math

import jax
import jax.numpy as jnp
from jax import lax
from jax.experimental import pallas as pl
from jax.experimental.pallas import tpu as pltpu

N_META = 16
NOPE = 128
ROPE = 64
VDIM = 128
QK_CAT = 256
ROPE_THETA = 10000.0
EPS = 1e-6
MLA_SCALE = 1.0 / math.sqrt(NOPE + ROPE)
SB_SCALE = 1.0 / math.sqrt(VDIM)
LANE = 128
BF16_SUBLANE = 16
ATT_BLK = 256
ATT_HEADS = 2
VMEM_LIMIT = 56 * 1024 * 1024
NEG_BIG = -1e30

_f32 = jnp.float32
_bf16 = jnp.bfloat16


def _pick_tile(n, target):
    best = None
    for t in range(BF16_SUBLANE, min(n, target) + 1, BF16_SUBLANE):
        if n % t == 0:
            best = t
    assert best is not None, (n, target)
    return best


def _params(*sem):
    return pltpu.CompilerParams(dimension_semantics=sem, vmem_limit_bytes=VMEM_LIMIT)


def _rms(xf, g):
    return xf * lax.rsqrt(jnp.mean(xf * xf, axis=-1, keepdims=True) + EPS) * g


def _rmsnorm_kernel(x_ref, g_ref, o_ref):
    o_ref[...] = _rms(x_ref[...].astype(_f32), g_ref[...]).astype(o_ref.dtype)


def _rmsnorm(x, g, out_dtype, tm):
    m, d = x.shape
    return pl.pallas_call(
        _rmsnorm_kernel,
        grid=(m // tm,),
        in_specs=[pl.BlockSpec((tm, d), lambda i: (i, 0)),
                  pl.BlockSpec((1, d), lambda i: (0, 0))],
        out_specs=pl.BlockSpec((tm, d), lambda i: (i, 0)),
        out_shape=jax.ShapeDtypeStruct((m, d), out_dtype),
        compiler_params=_params("parallel"),
        name="rmsnorm",
    )(x, g.reshape(1, d))


def _mm_scale_kernel(x_ref, w_ref, s_ref, o_ref):
    acc = jnp.dot(x_ref[...], w_ref[...], preferred_element_type=_f32)
    o_ref[...] = (acc * s_ref[...]).astype(o_ref.dtype)


def _matmul_colscale(x, w, colscale, out_dtype, tm, tn):
    m, k = x.shape
    n = w.shape[1]
    return pl.pallas_call(
        _mm_scale_kernel,
        grid=(n // tn, m // tm),
        in_specs=[pl.BlockSpec((tm, k), lambda j, i: (i, 0)),
                  pl.BlockSpec((k, tn), lambda j, i: (0, j)),
                  pl.BlockSpec((1, tn), lambda j, i: (0, j))],
        out_specs=pl.BlockSpec((tm, tn), lambda j, i: (i, j)),
        out_shape=jax.ShapeDtypeStruct((m, n), out_dtype),
        compiler_params=_params("parallel", "parallel"),
        name="matmul_colscale",
    )(x, w, colscale)


def _mm_kernel(x_ref, w_ref, o_ref):
    o_ref[...] = jnp.dot(x_ref[...], w_ref[...], preferred_element_type=_f32).astype(o_ref.dtype)


def _matmul_small(x, w, tm):
    m, k = x.shape
    n = w.shape[1]
    return pl.pallas_call(
        _mm_kernel,
        grid=(m // tm,),
        in_specs=[pl.BlockSpec((tm, k), lambda i: (i, 0)),
                  pl.BlockSpec((k, n), lambda i: (0, 0))],
        out_specs=pl.BlockSpec((tm, n), lambda i: (i, 0)),
        out_shape=jax.ShapeDtypeStruct((m, n), _f32),
        compiler_params=_params("parallel"),
        name="matmul_small",
    )(x, w)


def _mm_res_kernel(x_ref, w_ref, r_ref, o_ref):
    o_ref[...] = r_ref[...] + jnp.dot(x_ref[...], w_ref[...], preferred_element_type=_f32)


def _matmul_residual(x, w, res, tm, tn):
    m, k = x.shape
    n = w.shape[1]
    return pl.pallas_call(
        _mm_res_kernel,
        grid=(n // tn, m // tm),
        in_specs=[pl.BlockSpec((tm, k), lambda j, i: (i, 0)),
                  pl.BlockSpec((k, tn), lambda j, i: (0, j)),
                  pl.BlockSpec((tm, tn), lambda j, i: (i, j))],
        out_specs=pl.BlockSpec((tm, tn), lambda j, i: (i, j)),
        out_shape=jax.ShapeDtypeStruct((m, n), _f32),
        compiler_params=_params("parallel", "parallel"),
        name="matmul_residual",
    )(x, w, res)


def _rope128(x, cos, sin_signed):
    lane = lax.broadcasted_iota(jnp.int32, x.shape, 1)
    partner = jnp.where(lane < ROPE // 2,
                        pltpu.roll(x, LANE - ROPE // 2, 1),
                        pltpu.roll(x, ROPE // 2, 1))
    return x * cos + partner * sin_signed


def _qproj_kernel(x_ref, g_ref, w_ref, cos_ref, sin_ref, o_ref, xn_ref, *, heads_per_step):
    @pl.when(pl.program_id(1) == 0)
    def _():
        xn_ref[...] = _rms(x_ref[...], g_ref[...]).astype(_bf16)

    acc = jnp.dot(xn_ref[...], w_ref[...], preferred_element_type=_f32)
    cos = cos_ref[...]
    sin = sin_ref[...]
    for hh in range(heads_per_step):
        c0 = hh * QK_CAT
        o_ref[:, c0:c0 + NOPE] = (acc[:, c0:c0 + NOPE] * MLA_SCALE).astype(o_ref.dtype)
        roped = _rope128(acc[:, c0 + NOPE:c0 + QK_CAT], cos, sin)
        o_ref[:, c0 + NOPE:c0 + QK_CAT] = (roped * MLA_SCALE).astype(o_ref.dtype)


def _q_proj(small, g_q, w_q, cos, sin, lp, tm, ql):
    m = small.shape[0]
    n = w_q.shape[1]
    hps = 2 if (n // QK_CAT) % 2 == 0 else 1
    tn = hps * QK_CAT
    nrow = lp // tm
    return pl.pallas_call(
        functools.partial(_qproj_kernel, heads_per_step=hps),
        grid=(m // tm, n // tn),
        in_specs=[pl.BlockSpec((tm, ql), lambda i, j: (i, 0)),
                  pl.BlockSpec((1, ql), lambda i, j: (0, 0)),
                  pl.BlockSpec((ql, tn), lambda i, j: (0, j)),
                  pl.BlockSpec((tm, LANE), lambda i, j: (i % nrow, 0)),
                  pl.BlockSpec((tm, LANE), lambda i, j: (i % nrow, 0))],
        out_specs=pl.BlockSpec((tm, tn), lambda i, j: (i, j)),
        out_shape=jax.ShapeDtypeStruct((m, n), _bf16),
        scratch_shapes=[pltpu.VMEM((tm, ql), _bf16)],
        compiler_params=_params("parallel", "arbitrary"),
        name="q_proj",
    )(small, g_q.reshape(1, ql), w_q, cos, sin)


def _kvproj_kernel(x_ref, kr_ref, g_ref, wk_ref, wv_ref, cos_ref, sin_ref,
                   k_ref, v_ref, xn_ref, krr_ref, *, heads_per_step):
    @pl.when(pl.program_id(1) == 0)
    def _():
        xn_ref[...] = _rms(x_ref[...], g_ref[...]).astype(_bf16)
        krr_ref[...] = _rope128(kr_ref[...], cos_ref[...], sin_ref[...]).astype(_bf16)

    xn = xn_ref[...]
    kn = jnp.dot(xn, wk_ref[...], preferred_element_type=_f32)
    v_ref[...] = jnp.dot(xn, wv_ref[...], preferred_element_type=_f32).astype(v_ref.dtype)
    krr = krr_ref[...]
    for hh in range(heads_per_step):
        k_ref[:, hh * QK_CAT:hh * QK_CAT + NOPE] = kn[:, hh * NOPE:(hh + 1) * NOPE].astype(k_ref.dtype)
        k_ref[:, hh * QK_CAT + NOPE:(hh + 1) * QK_CAT] = krr


def _kv_proj(small, g_kv, w_k, w_v, cos, sin, lp, tm, ql, kvl):
    m = small.shape[0]
    heads = w_k.shape[1] // NOPE
    hps = 2 if heads % 2 == 0 else 1
    nrow = lp // tm
    return pl.pallas_call(
        functools.partial(_kvproj_kernel, heads_per_step=hps),
        grid=(m // tm, heads // hps),
        in_specs=[pl.BlockSpec((tm, kvl), lambda i, j: (i, ql // kvl)),
                  pl.BlockSpec((tm, LANE), lambda i, j: (i, (ql + kvl) // LANE)),
                  pl.BlockSpec((1, kvl), lambda i, j: (0, 0)),
                  pl.BlockSpec((kvl, hps * NOPE), lambda i, j: (0, j)),
                  pl.BlockSpec((kvl, hps * VDIM), lambda i, j: (0, j)),
                  pl.BlockSpec((tm, LANE), lambda i, j: (i % nrow, 0)),
                  pl.BlockSpec((tm, LANE), lambda i, j: (i % nrow, 0))],
        out_specs=[pl.BlockSpec((tm, hps * QK_CAT), lambda i, j: (i, j)),
                   pl.BlockSpec((tm, hps * VDIM), lambda i, j: (i, j))],
        out_shape=[jax.ShapeDtypeStruct((m, heads * QK_CAT), _bf16),
                   jax.ShapeDtypeStruct((m, heads * VDIM), _bf16)],
        scratch_shapes=[pltpu.VMEM((tm, kvl), _bf16), pltpu.VMEM((tm, LANE), _bf16)],
        compiler_params=_params("parallel", "arbitrary"),
        name="kv_proj",
    )(small, small, g_kv.reshape(1, kvl), w_k, w_v, cos, sin)


def _dot_nt(a, b):
    return lax.dot_general(a, b, (((1,), (1,)), ((), ())), preferred_element_type=_f32)


def _dot_tn(a, b):
    return lax.dot_general(a, b, (((0,), (0,)), ((), ())), preferred_element_type=_f32)


def _row_blocks(lp):
    n_main = lp // ATT_BLK
    tail = lp - n_main * ATT_BLK
    return n_main, tail


def _key_query_iota(bk, bq):
    key = lax.broadcasted_iota(jnp.int32, (bk, bq), 0)
    qry = lax.broadcasted_iota(jnp.int32, (bk, bq), 1)
    return key, qry


def _mla_kernel(q_ref, k_ref, v_ref, o_ref, s_ref, m_ref, l_ref, acc_ref, *, lp, hps):
    n_main, tail = _row_blocks(lp)

    def q_block(q0, bq, n_full, is_tail):
        qs = [q_ref[pl.ds(q0, bq), a * QK_CAT:(a + 1) * QK_CAT] for a in range(hps)]

        def scores(a, k0, bk):
            return _dot_nt(k_ref[pl.ds(k0, bk), a * QK_CAT:(a + 1) * QK_CAT], qs[a])

        def update(a, s, k0, bk, masked):
            if masked:
                key, qry = _key_query_iota(bk, bq)
                s = jnp.where(key <= qry, s, NEG_BIG)
            m_prev = m_ref[a, :, 0:bq]
            m_new = jnp.maximum(m_prev, jnp.max(s, axis=0, keepdims=True))
            alpha = jnp.exp(m_prev - m_new)
            p = jnp.exp(s - m_new)
            l_ref[a, :, 0:bq] = alpha * l_ref[a, :, 0:bq] + jnp.sum(p, axis=0, keepdims=True)
            pv = _dot_tn(v_ref[pl.ds(k0, bk), a * VDIM:(a + 1) * VDIM], p.astype(_bf16))
            acc_ref[a, :, 0:bq] = alpha * acc_ref[a, :, 0:bq] + pv
            m_ref[a, :, 0:bq] = m_new

        for a in range(hps):
            m_ref[a, :, 0:bq] = jnp.full((1, bq), NEG_BIG, _f32)
            l_ref[a, :, 0:bq] = jnp.zeros((1, bq), _f32)
            acc_ref[a, :, 0:bq] = jnp.zeros((VDIM, bq), _f32)
            s_ref[0, a, :, 0:bq] = scores(a, 0, ATT_BLK)

        last = n_full - 1 if is_tail else n_full

        def full_step(j, carry):
            k0 = pl.multiple_of(j * ATT_BLK, ATT_BLK)
            nxt = pl.multiple_of(jnp.minimum(j + 1, last) * ATT_BLK, ATT_BLK)
            for a in range(hps):
                s = s_ref[j % 2, a, :, 0:bq]
                s_ref[(j + 1) % 2, a, :, 0:bq] = scores(a, nxt, ATT_BLK)
                update(a, s, k0, ATT_BLK, False)
            return carry

        lax.fori_loop(0, n_full, full_step, 0)
        for a in range(hps):
            if is_tail:
                s = scores(a, q0, bq)
            else:
                s = s_ref[n_full % 2, a, :, 0:bq]
            update(a, s, q0, bq, True)
            out_t = acc_ref[a, :, 0:bq] * (1.0 / l_ref[a, :, 0:bq])
            o_ref[pl.ds(q0, bq), a * VDIM:(a + 1) * VDIM] = out_t.T.astype(o_ref.dtype)

    def main_block(i, carry):
        q_block(pl.multiple_of(i * ATT_BLK, ATT_BLK), ATT_BLK, i, False)
        return carry

    lax.fori_loop(0, n_main, main_block, 0)
    if tail:
        q_block(n_main * ATT_BLK, tail, n_main, True)


def _sb_kernel(q_ref, k_ref, v_ref, o_ref, s_ref, r_ref, acc_ref, *, lp, hps):
    n_main, tail = _row_blocks(lp)
    key_i, key_j = _key_query_iota(ATT_BLK, ATT_BLK)
    later = (key_j > key_i).astype(_bf16)

    def q_block(q0, bq, n_full, is_tail):
        qs = [q_ref[pl.ds(q0, bq), a * VDIM:(a + 1) * VDIM] for a in range(hps)]

        def scores(a, k0, bk):
            return _dot_nt(k_ref[pl.ds(k0, bk), a * VDIM:(a + 1) * VDIM], qs[a])

        def update(a, z, k0, bk, masked):
            tail_sp = jnp.log(1.0 + jnp.exp(-jnp.abs(z)))
            log_beta = jnp.minimum(z, 0.0) - tail_sp
            log_rest = -jnp.maximum(z, 0.0) - tail_sp
            if masked:
                key, qry = _key_query_iota(bk, bq)
                strict = key < qry
                log_rest = jnp.where(strict, log_rest, 0.0)
            within = jnp.dot(later[0:bk, 0:bk], log_rest.astype(_bf16), preferred_element_type=_f32)
            r = r_ref[a, :, 0:bq]
            w = jnp.exp(log_beta + within + r)
            if masked:
                w = jnp.where(strict, w, 0.0)
            acc_ref[a, :, 0:bq] = acc_ref[a, :, 0:bq] + _dot_tn(
                v_ref[pl.ds(k0, bk), a * VDIM:(a + 1) * VDIM], w.astype(_bf16))
            r_ref[a, :, 0:bq] = r + jnp.sum(log_rest, axis=0, keepdims=True)

        def prefetch(a, c):
            k0 = pl.multiple_of(jnp.maximum(c, 0) * ATT_BLK, ATT_BLK)
            s_ref[(c + 2) % 2, a, :, 0:bq] = scores(a, k0, ATT_BLK)

        for a in range(hps):
            r_ref[a, :, 0:bq] = jnp.zeros((1, bq), _f32)
            acc_ref[a, :, 0:bq] = jnp.zeros((VDIM, bq), _f32)
            if is_tail:
                z = scores(a, q0, bq)
            else:
                z = scores(a, q0, ATT_BLK)
            prefetch(a, n_full - 1)
            update(a, z, q0, bq, True)

        def full_step(jj, carry):
            c = n_full - 1 - jj
            k0 = pl.multiple_of(c * ATT_BLK, ATT_BLK)
            for a in range(hps):
                z = s_ref[c % 2, a, :, 0:bq]
                prefetch(a, c - 1)
                update(a, z, k0, ATT_BLK, False)
            return carry

        lax.fori_loop(0, n_full, full_step, 0)
        for a in range(hps):
            o_ref[pl.ds(q0, bq), a * VDIM:(a + 1) * VDIM] = acc_ref[a, :, 0:bq].T.astype(o_ref.dtype)

    def main_block(i, carry):
        q_block(pl.multiple_of(i * ATT_BLK, ATT_BLK), ATT_BLK, i, False)
        return carry

    lax.fori_loop(0, n_main, main_block, 0)
    if tail:
        q_block(n_main * ATT_BLK, tail, n_main, True)


def _att_scratch(hps, n_stat):
    return ([pltpu.VMEM((2, hps, ATT_BLK, ATT_BLK), _f32)]
            + [pltpu.VMEM((hps, 1, ATT_BLK), _f32)] * n_stat
            + [pltpu.VMEM((hps, VDIM, ATT_BLK), _f32)])


def _mla_attention(q_cat, k_cat, v, heads):
    b, lp, _ = q_cat.shape
    hps = ATT_HEADS if heads % ATT_HEADS == 0 else 1
    qk_spec = pl.BlockSpec((None, lp, hps * QK_CAT), lambda bi, h: (bi, 0, h))
    v_spec = pl.BlockSpec((None, lp, hps * VDIM), lambda bi, h: (bi, 0, h))
    return pl.pallas_call(
        functools.partial(_mla_kernel, lp=lp, hps=hps),
        grid=(b, heads // hps),
        in_specs=[qk_spec, qk_spec, v_spec],
        out_specs=v_spec,
        out_shape=jax.ShapeDtypeStruct((b, lp, heads * VDIM), _f32),
        scratch_shapes=_att_scratch(hps, 2),
        compiler_params=_params("parallel", "parallel"),
        name="mla_attention",
    )(q_cat, k_cat, v)


def _sb_attention(big, heads, q_col, k_col, v_col):
    b, lp, _ = big.shape
    hps = ATT_HEADS if heads % ATT_HEADS == 0 else 1
    w = hps * VDIM

    def spec(col0):
        return pl.BlockSpec((None, lp, w), lambda bi, h: (bi, 0, col0 // w + h))

    return pl.pallas_call(
        functools.partial(_sb_kernel, lp=lp, hps=hps),
        grid=(b, heads // hps),
        in_specs=[spec(q_col), spec(k_col), spec(v_col)],
        out_specs=pl.BlockSpec((None, lp, w), lambda bi, h: (bi, 0, h)),
        out_shape=jax.ShapeDtypeStruct((b, lp, heads * VDIM), _f32),
        scratch_shapes=_att_scratch(hps, 1),
        compiler_params=_params("parallel", "parallel"),
        name="sb_attention",
    )(big, big, big)


def _gate_kernel(ym_ref, ys_ref, zm_ref, zs_ref, gm_ref, gs_ref, o_ref, *, w):
    def one(y_ref, z_ref, g_ref):
        z = z_ref[...].astype(_f32)
        silu = z / (1.0 + jnp.exp(-z))
        return (_rms(y_ref[...], g_ref[...]) * silu).astype(o_ref.dtype)

    o_ref[:, 0:w] = one(ym_ref, zm_ref, gm_ref)
    o_ref[:, w:2 * w] = one(ys_ref, zs_ref, gs_ref)


def _gate(y_mla, y_sb, big, g_mla, g_sb, zm_col, zs_col, tm):
    m, w = y_mla.shape
    yspec = pl.BlockSpec((tm, w), lambda i: (i, 0))
    gspec = pl.BlockSpec((1, w), lambda i: (0, 0))
    return pl.pallas_call(
        functools.partial(_gate_kernel, w=w),
        grid=(m // tm,),
        in_specs=[yspec, yspec,
                  pl.BlockSpec((tm, w), lambda i: (i, zm_col // w)),
                  pl.BlockSpec((tm, w), lambda i: (i, zs_col // w)),
                  gspec, gspec],
        out_specs=pl.BlockSpec((tm, 2 * w), lambda i: (i, 0)),
        out_shape=jax.ShapeDtypeStruct((m, 2 * w), _bf16),
        compiler_params=_params("parallel"),
        name="gate",
    )(y_mla, y_sb, big, big, g_mla.reshape(1, w), g_sb.reshape(1, w))


def _rope_tables(lp):
    inv_freq = ROPE_THETA ** (-jnp.arange(0, ROPE, 2, dtype=_f32) / ROPE)
    ang = jnp.arange(lp, dtype=jnp.int32).astype(_f32)[:, None] * inv_freq[None, :]
    cos, sin = jnp.cos(ang), jnp.sin(ang)
    zeros = jnp.zeros((lp, LANE - ROPE), _f32)
    return (jnp.concatenate([cos, cos, zeros], axis=1),
            jnp.concatenate([-sin, sin, zeros], axis=1))


@jax.jit
def _forward(x, meta_tokens, g_norm, w_in, g_q, g_kv, w_uq, w_ukv, g_out_mla, g_out_sb,
             w_o, g_final):
    b, seq, d = x.shape
    depth = w_in.shape[0]
    ql, kvl = g_q.shape[1], g_kv.shape[1]
    heads = w_uq.shape[2] // (NOPE + ROPE)
    w_grp = heads * VDIM
    assert w_ukv.shape[2] == heads * (NOPE + VDIM)
    assert w_in.shape[2] == ql + kvl + ROPE + 5 * w_grp
    assert ql % kvl == 0 and (ql + kvl) % LANE == 0 and kvl % LANE == 0

    l_real = N_META + seq
    lp = -(-l_real // LANE) * LANE
    mp = b * lp

    n_small = ql + kvl + ROPE
    w_small = jnp.pad(w_in[:, :, :n_small], ((0, 0), (0, 0), (0, LANE - ROPE))).astype(_bf16)
    w_big = w_in[:, :, n_small:].astype(_bf16)
    zm_col, qs_col, ks_col, vs_col, zs_col = (i * w_grp for i in range(5))
    colscale = jnp.ones((1, 5 * w_grp), _f32).at[:, qs_col:qs_col + w_grp].set(SB_SCALE)
    wq = w_uq.reshape(depth, ql, heads, NOPE + ROPE)
    wq = jnp.pad(wq, ((0, 0), (0, 0), (0, 0), (0, QK_CAT - NOPE - ROPE)))
    wq = wq.reshape(depth, ql, heads * QK_CAT).astype(_bf16)
    wkv = w_ukv.reshape(depth, kvl, heads, NOPE + VDIM)
    wk = wkv[..., :NOPE].reshape(depth, kvl, heads * NOPE).astype(_bf16)
    wv = wkv[..., NOPE:].reshape(depth, kvl, heads * VDIM).astype(_bf16)
    wo = w_o.astype(_bf16)
    cos, sin = _rope_tables(lp)

    meta = jnp.broadcast_to(meta_tokens[None].astype(x.dtype), (b, N_META, d))
    h = jnp.concatenate([meta, x, jnp.zeros((b, lp - l_real, d), x.dtype)], axis=1)
    h = h.reshape(mp, d)

    tm_norm = _pick_tile(lp, 384)
    tm_mm = _pick_tile(lp, 528)
    tm_proj = _pick_tile(lp, 1056)
    tn_big = 1024 if (5 * w_grp) % 1024 == 0 else 512
    tn_out = 1024 if d % 1024 == 0 else 512

    for i in range(depth):
        u = _rmsnorm(h, g_norm[i], _bf16, tm_norm)
        small = _matmul_small(u, w_small[i], tm_mm)
        big = _matmul_colscale(u, w_big[i], colscale, _bf16, tm_mm, tn_big)
        q_cat = _q_proj(small, g_q[i], wq[i], cos, sin, lp, tm_proj, ql)
        k_cat, v_mla = _kv_proj(small, g_kv[i], wk[i], wv[i], cos, sin, lp, tm_proj, ql, kvl)
        y_mla = _mla_attention(q_cat.reshape(b, lp, -1), k_cat.reshape(b, lp, -1),
                               v_mla.reshape(b, lp, -1), heads)
        y_sb = _sb_attention(big.reshape(b, lp, -1), heads, qs_col, ks_col, vs_col)
        y = _gate(y_mla.reshape(mp, w_grp), y_sb.reshape(mp, w_grp), big,
                  g_out_mla[i], g_out_sb[i], zm_col, zs_col, tm_norm)
        h = _matmul_residual(y, wo[i], h, tm_mm, tn_out)

    h_real = h.reshape(b, lp, d)[:, N_META:l_real].reshape(b * seq, d)
    out = _rmsnorm(h_real, g_final, x.dtype, _pick_tile(seq, 512))
    return out.reshape(b, seq, d)


def kernel(x, meta_tokens, g_norm, w_in, g_q, g_kv, w_uq, w_ukv, g_out_mla, g_out_sb, w_o, g_final):
    return _forward(x, meta_tokens, g_norm, w_in, g_q, g_kv, w_uq, w_ukv, g_out_mla, g_out_sb,
                    w_o, g_final)
```

```python
import functools
import math

import jax
import jax.numpy as jnp
from jax import lax
from jax.experimental import pallas as pl
from jax.experimental.pallas import tpu as pltpu

N_META = 16
NOPE = 128
ROPE = 64
VDIM = 128
QK_CAT = 256
ROPE_THETA = 10000.0
EPS = 1e-6
LOG2E = math.log2(math.e)
MLA_SCALE2 = LOG2E / math.sqrt(NOPE + ROPE)
SB_SCALE2 = LOG2E / math.sqrt(VDIM)
LANE = 128
BF16_SUBLANE = 16
ATT_BLK = 256
MLA_HEADS_PER_STEP = 2
SB_HEADS_PER_STEP = 4
VMEM_LIMIT = 56 * 1024 * 1024
NEG_BIG = -1e30

_f32 = jnp.float32
_bf16 = jnp.bfloat16


def _pick_tile(n, target):
    best = None
    for t in range(BF16_SUBLANE, min(n, target) + 1, BF16_SUBLANE):
        if n % t == 0:
            best = t
    assert best is not None, (n, target)
    return best


def _params(*sem):
    return pltpu.CompilerParams(dimension_semantics=sem, vmem_limit_bytes=VMEM_LIMIT)


def _rms(xf, g):
    return xf * lax.rsqrt(jnp.mean(xf * xf, axis=-1, keepdims=True) + EPS) * g


def _rmsnorm_kernel(x_ref, g_ref, o_ref):
    o_ref[...] = _rms(x_ref[...].astype(_f32), g_ref[...]).astype(o_ref.dtype)


def _rmsnorm(x, g, out_dtype, tm):
    m, d = x.shape
    return pl.pallas_call(
        _rmsnorm_kernel,
        grid=(m // tm,),
        in_specs=[pl.BlockSpec((tm, d), lambda i: (i, 0)),
                  pl.BlockSpec((1, d), lambda i: (0, 0))],
        out_specs=pl.BlockSpec((tm, d), lambda i: (i, 0)),
        out_shape=jax.ShapeDtypeStruct((m, d), out_dtype),
        compiler_params=_params("parallel"),
        name="rmsnorm",
    )(x, g.reshape(1, d))


def _mm_scale_kernel(x_ref, w_ref, s_ref, o_ref):
    acc = jnp.dot(x_ref[...], w_ref[...], preferred_element_type=_f32)
    o_ref[...] = (acc * s_ref[...]).astype(o_ref.dtype)


def _matmul_colscale(x, w, colscale, out_dtype, tm, tn):
    m, k = x.shape
    n = w.shape[1]
    return pl.pallas_call(
        _mm_scale_kernel,
        grid=(n // tn, m // tm),
        in_specs=[pl.BlockSpec((tm, k), lambda j, i: (i, 0)),
                  pl.BlockSpec((k, tn), lambda j, i: (0, j)),
                  pl.BlockSpec((1, tn), lambda j, i: (0, j))],
        out_specs=pl.BlockSpec((tm, tn), lambda j, i: (i, j)),
        out_shape=jax.ShapeDtypeStruct((m, n), out_dtype),
        compiler_params=_params("parallel", "parallel"),
        name="matmul_colscale",
    )(x, w, colscale)


def _mm_kernel(x_ref, w_ref, o_ref):
    o_ref[...] = jnp.dot(x_ref[...], w_ref[...], preferred_element_type=_f32).astype(o_ref.dtype)


def _matmul_small(x, w, tm):
    m, k = x.shape
    n = w.shape[1]
    return pl.pallas_call(
        _mm_kernel,
        grid=(m // tm,),
        in_specs=[pl.BlockSpec((tm, k), lambda i: (i, 0)),
                  pl.BlockSpec((k, n), lambda i: (0, 0))],
        out_specs=pl.BlockSpec((tm, n), lambda i: (i, 0)),
        out_shape=jax.ShapeDtypeStruct((m, n), _f32),
        compiler_params=_params("parallel"),
        name="matmul_small",
    )(x, w)


def _mm_res_kernel(x_ref, w_ref, r_ref, o_ref):
    o_ref[...] = r_ref[...] + jnp.dot(x_ref[...], w_ref[...], preferred_element_type=_f32)


def _matmul_residual(x, w, res, tm, tn):
    m, k = x.shape
    n = w.shape[1]
    return pl.pallas_call(
        _mm_res_kernel,
        grid=(n // tn, m // tm),
        in_specs=[pl.BlockSpec((tm, k), lambda j, i: (i, 0)),
                  pl.BlockSpec((k, tn), lambda j, i: (0, j)),
                  pl.BlockSpec((tm, tn), lambda j, i: (i, j))],
        out_specs=pl.BlockSpec((tm, tn), lambda j, i: (i, j)),
        out_shape=jax.ShapeDtypeStruct((m, n), _f32),
        compiler_params=_params("parallel", "parallel"),
        name="matmul_residual",
    )(x, w, res)


def _rope128(x, cos, sin_signed):
    lane = lax.broadcasted_iota(jnp.int32, x.shape, 1)
    partner = jnp.where(lane < ROPE // 2,
                        pltpu.roll(x, LANE - ROPE // 2, 1),
                        pltpu.roll(x, ROPE // 2, 1))
    return x * cos + partner * sin_signed


def _qproj_kernel(x_ref, g_ref, w_ref, cos_ref, sin_ref, o_ref, xn_ref, *, heads_per_step):
    @pl.when(pl.program_id(1) == 0)
    def _():
        xn_ref[...] = _rms(x_ref[...], g_ref[...]).astype(_bf16)

    acc = jnp.dot(xn_ref[...], w_ref[...], preferred_element_type=_f32)
    cos = cos_ref[...]
    sin = sin_ref[...]
    for hh in range(heads_per_step):
        c0 = hh * QK_CAT
        o_ref[:, c0:c0 + NOPE] = (acc[:, c0:c0 + NOPE] * MLA_SCALE2).astype(o_ref.dtype)
        roped = _rope128(acc[:, c0 + NOPE:c0 + QK_CAT], cos, sin)
        o_ref[:, c0 + NOPE:c0 + QK_CAT] = (roped * MLA_SCALE2).astype(o_ref.dtype)


def _q_proj(small, g_q, w_q, cos, sin, lp, tm, ql):
    m = small.shape[0]
    n = w_q.shape[1]
    hps = 2 if (n // QK_CAT) % 2 == 0 else 1
    tn = hps * QK_CAT
    nrow = lp // tm
    return pl.pallas_call(
        functools.partial(_qproj_kernel, heads_per_step=hps),
        grid=(m // tm, n // tn),
        in_specs=[pl.BlockSpec((tm, ql), lambda i, j: (i, 0)),
                  pl.BlockSpec((1, ql), lambda i, j: (0, 0)),
                  pl.BlockSpec((ql, tn), lambda i, j: (0, j)),
                  pl.BlockSpec((tm, LANE), lambda i, j: (i % nrow, 0)),
                  pl.BlockSpec((tm, LANE), lambda i, j: (i % nrow, 0))],
        out_specs=pl.BlockSpec((tm, tn), lambda i, j: (i, j)),
        out_shape=jax.ShapeDtypeStruct((m, n), _bf16),
        scratch_shapes=[pltpu.VMEM((tm, ql), _bf16)],
        compiler_params=_params("parallel", "arbitrary"),
        name="q_proj",
    )(small, g_q.reshape(1, ql), w_q, cos, sin)


def _kvproj_kernel(x_ref, kr_ref, g_ref, wk_ref, wv_ref, cos_ref, sin_ref,
                   k_ref, v_ref, xn_ref, krr_ref, *, heads_per_step):
    @pl.when(pl.program_id(1) == 0)
    def _():
        xn_ref[...] = _rms(x_ref[...], g_ref[...]).astype(_bf16)
        krr_ref[...] = _rope128(kr_ref[...], cos_ref[...], sin_ref[...]).astype(_bf16)

    xn = xn_ref[...]
    kn = jnp.dot(xn, wk_ref[...], preferred_element_type=_f32)
    v_ref[...] = jnp.dot(xn, wv_ref[...], preferred_element_type=_f32).astype(v_ref.dtype)
    krr = krr_ref[...]
    for hh in range(heads_per_step):
        k_ref[:, hh * QK_CAT:hh * QK_CAT + NOPE] = kn[:, hh * NOPE:(hh + 1) * NOPE].astype(k_ref.dtype)
        k_ref[:, hh * QK_CAT + NOPE:(hh + 1) * QK_CAT] = krr


def _kv_proj(small, g_kv, w_k, w_v, cos, sin, lp, tm, ql, kvl):
    m = small.shape[0]
    heads = w_k.shape[1] // NOPE
    hps = 2 if heads % 2 == 0 else 1
    nrow = lp // tm
    return pl.pallas_call(
        functools.partial(_kvproj_kernel, heads_per_step=hps),
        grid=(m // tm, heads // hps),
        in_specs=[pl.BlockSpec((tm, kvl), lambda i, j: (i, ql // kvl)),
                  pl.BlockSpec((tm, LANE), lambda i, j: (i, (ql + kvl) // LANE)),
                  pl.BlockSpec((1, kvl), lambda i, j: (0, 0)),
                  pl.BlockSpec((kvl, hps * NOPE), lambda i, j: (0, j)),
                  pl.BlockSpec((kvl, hps * VDIM), lambda i, j: (0, j)),
                  pl.BlockSpec((tm, LANE), lambda i, j: (i % nrow, 0)),
                  pl.BlockSpec((tm, LANE), lambda i, j: (i % nrow, 0))],
        out_specs=[pl.BlockSpec((tm, hps * QK_CAT), lambda i, j: (i, j)),
                   pl.BlockSpec((tm, hps * VDIM), lambda i, j: (i, j))],
        out_shape=[jax.ShapeDtypeStruct((m, heads * QK_CAT), _bf16),
                   jax.ShapeDtypeStruct((m, heads * VDIM), _bf16)],
        scratch_shapes=[pltpu.VMEM((tm, kvl), _bf16), pltpu.VMEM((tm, LANE), _bf16)],
        compiler_params=_params("parallel", "arbitrary"),
        name="kv_proj",
    )(small, small, g_kv.reshape(1, kvl), w_k, w_v, cos, sin)


def _dot_nt(a, b):
    return lax.dot_general(a, b, (((1,), (1,)), ((), ())), preferred_element_type=_f32)


def _dot_tn(a, b):
    return lax.dot_general(a, b, (((0,), (0,)), ((), ())), preferred_element_type=_f32)


def _row_blocks(lp):
    n_main = lp // ATT_BLK
    tail = lp - n_main * ATT_BLK
    return n_main, tail


def _key_query_iota(bk, bq):
    key = lax.broadcasted_iota(jnp.int32, (bk, bq), 0)
    qry = lax.broadcasted_iota(jnp.int32, (bk, bq), 1)
    return key, qry


def _mla_kernel(q_ref, k_ref, v_ref, o_ref, s_ref, p_ref, al_ref, m_ref, l_ref, acc_ref, *, lp, hps):
    n_main, tail = _row_blocks(lp)

    def q_block(q0, bq, n_full, is_tail):
        qs = [q_ref[pl.ds(q0, bq), a * QK_CAT:(a + 1) * QK_CAT] for a in range(hps)]

        def scores(a, k0, bk):
            return _dot_nt(k_ref[pl.ds(k0, bk), a * QK_CAT:(a + 1) * QK_CAT], qs[a])

        def probs(a, s, slot, bk, masked):
            if masked:
                key, qry = _key_query_iota(bk, bq)
                s = jnp.where(key <= qry, s, NEG_BIG)
            m_prev = m_ref[a, :, 0:bq]
            m_new = jnp.maximum(m_prev, jnp.max(s, axis=0, keepdims=True))
            alpha = jnp.exp2(m_prev - m_new)
            p = jnp.exp2(s - m_new)
            l_ref[a, :, 0:bq] = alpha * l_ref[a, :, 0:bq] + jnp.sum(p, axis=0, keepdims=True)
            m_ref[a, :, 0:bq] = m_new
            p_ref[slot, a, 0:bk, 0:bq] = p.astype(_bf16)
            al_ref[slot, a, :, 0:bq] = alpha

        def values(a, slot, k0, bk):
            pv = _dot_tn(v_ref[pl.ds(k0, bk), a * VDIM:(a + 1) * VDIM], p_ref[slot, a, 0:bk, 0:bq])
            acc_ref[a, :, 0:bq] = al_ref[slot, a, :, 0:bq] * acc_ref[a, :, 0:bq] + pv

        for a in range(hps):
            m_ref[a, :, 0:bq] = jnp.full((1, bq), NEG_BIG, _f32)
            l_ref[a, :, 0:bq] = jnp.zeros((1, bq), _f32)
            acc_ref[a, :, 0:bq] = jnp.zeros((VDIM, bq), _f32)
            s0 = scores(a, 0, ATT_BLK)
            for slot in range(2):
                p_ref[slot, a, :, 0:bq] = jnp.zeros((ATT_BLK, bq), _bf16)
                al_ref[slot, a, :, 0:bq] = jnp.ones((1, bq), _f32)
                s_ref[slot, a, :, 0:bq] = s0

        last = n_full - 1 if is_tail else n_full

        def step(c, slot):
            nxt = pl.multiple_of(jnp.minimum(c + 1, last) * ATT_BLK, ATT_BLK)
            prv = pl.multiple_of(jnp.maximum(c - 1, 0) * ATT_BLK, ATT_BLK)
            for a in range(hps):
                s_ref[1 - slot, a, :, 0:bq] = scores(a, nxt, ATT_BLK)
                probs(a, s_ref[slot, a, :, 0:bq], slot, ATT_BLK, False)
                values(a, 1 - slot, prv, ATT_BLK)

        odd = n_full % 2
        if is_tail:
            if odd:
                step(0, 1)
        else:
            @pl.when(odd == 1)
            def _():
                step(0, 1)

        def pair(t, carry):
            c = odd + 2 * t
            step(c, 0)
            step(c + 1, 1)
            return carry

        lax.fori_loop(0, n_full // 2, pair, 0)
        prv = pl.multiple_of(jnp.maximum(n_full - 1, 0) * ATT_BLK, ATT_BLK)
        for a in range(hps):
            s = scores(a, q0, bq) if is_tail else s_ref[0, a, :, 0:bq]
            probs(a, s, 0, bq, True)
            values(a, 1, prv, ATT_BLK)
            values(a, 0, q0, bq)
            out_t = acc_ref[a, :, 0:bq] * (1.0 / l_ref[a, :, 0:bq])
            o_ref[pl.ds(q0, bq), a * VDIM:(a + 1) * VDIM] = out_t.T.astype(o_ref.dtype)

    def main_block(i, carry):
        q_block(pl.multiple_of(i * ATT_BLK, ATT_BLK), ATT_BLK, i, False)
        return carry

    lax.fori_loop(0, n_main, main_block, 0)
    if tail:
        q_block(n_main * ATT_BLK, tail, n_main, True)


def _sb_kernel(q_ref, k_ref, v_ref, o_ref, s_ref, w_ref, r_ref, acc_ref, *, lp, hps):
    n_main, tail = _row_blocks(lp)
    key_i, key_j = _key_query_iota(ATT_BLK, ATT_BLK)
    later = (key_j > key_i).astype(_bf16)

    def q_block(q0, bq, n_full, is_tail):
        qs = [q_ref[pl.ds(q0, bq), a * VDIM:(a + 1) * VDIM] for a in range(hps)]

        def scores(a, k0, bk):
            return _dot_nt(k_ref[pl.ds(k0, bk), a * VDIM:(a + 1) * VDIM], qs[a])

        def weights(a, z, slot, bk, masked):
            tail_sp = jnp.log2(1.0 + jnp.exp2(-jnp.abs(z)))
            log_beta = jnp.minimum(z, 0.0) - tail_sp
            log_rest = log_beta - z
            if masked:
                key, qry = _key_query_iota(bk, bq)
                strict = key < qry
                log_rest = jnp.where(strict, log_rest, 0.0)
            within = jnp.dot(later[0:bk, 0:bk], log_rest.astype(_bf16), preferred_element_type=_f32)
            r = r_ref[a, :, 0:bq]
            w = jnp.exp2(log_beta + within + r)
            if masked:
                w = jnp.where(strict, w, 0.0)
            w_ref[slot, a, 0:bk, 0:bq] = w.astype(_bf16)
            r_ref[a, :, 0:bq] = r + jnp.sum(log_rest, axis=0, keepdims=True)

        def values(a, slot, k0, bk):
            acc_ref[a, :, 0:bq] = acc_ref[a, :, 0:bq] + _dot_tn(
                v_ref[pl.ds(k0, bk), a * VDIM:(a + 1) * VDIM], w_ref[slot, a, 0:bk, 0:bq])

        k_first = pl.multiple_of(jnp.maximum(n_full - 1, 0) * ATT_BLK, ATT_BLK)
        for a in range(hps):
            r_ref[a, :, 0:bq] = jnp.zeros((1, bq), _f32)
            acc_ref[a, :, 0:bq] = jnp.zeros((VDIM, bq), _f32)
            z = scores(a, q0, bq)
            z_first = scores(a, k_first, ATT_BLK)
            s_ref[0, a, :, 0:bq] = z_first
            s_ref[1, a, :, 0:bq] = z_first
            weights(a, z, 0, bq, True)
            if is_tail:
                values(a, 0, q0, bq)
                w_ref[0, a, :, 0:bq] = jnp.zeros((ATT_BLK, bq), _bf16)
                w_ref[1, a, :, 0:bq] = jnp.zeros((ATT_BLK, bq), _bf16)
            else:
                w_ref[1, a, :, 0:bq] = w_ref[0, a, :, 0:bq]

        v_last = n_full - 1 if is_tail else n_full

        def step(c, slot):
            k_before = pl.multiple_of(jnp.maximum(c - 1, 0) * ATT_BLK, ATT_BLK)
            k_after = pl.multiple_of(jnp.minimum(c + 1, v_last) * ATT_BLK, ATT_BLK)
            for a in range(hps):
                z = s_ref[slot, a, :, 0:bq]
                s_ref[1 - slot, a, :, 0:bq] = scores(a, k_before, ATT_BLK)
                weights(a, z, slot, ATT_BLK, False)
                values(a, 1 - slot, k_after, ATT_BLK)

        odd = n_full % 2
        if is_tail:
            if odd:
                step(n_full - 1, 0)
        else:
            @pl.when(odd == 1)
            def _():
                step(n_full - 1, 0)

        n_pairs = n_full // 2

        def pair(tt, carry):
            c = 2 * (n_pairs - 1 - tt) + 1
            step(c, 1)
            step(c - 1, 0)
            return carry

        lax.fori_loop(0, n_pairs, pair, 0)
        for a in range(hps):
            values(a, 0, 0, ATT_BLK)
            o_ref[pl.ds(q0, bq), a * VDIM:(a + 1) * VDIM] = acc_ref[a, :, 0:bq].T.astype(o_ref.dtype)

    def main_block(i, carry):
        q_block(pl.multiple_of(i * ATT_BLK, ATT_BLK), ATT_BLK, i, False)
        return carry

    lax.fori_loop(0, n_main, main_block, 0)
    if tail:
        q_block(n_main * ATT_BLK, tail, n_main, True)


def _att_scratch(hps, slot_stats, n_stat):
    return ([pltpu.VMEM((2, hps, ATT_BLK, ATT_BLK), _f32),
             pltpu.VMEM((2, hps, ATT_BLK, ATT_BLK), _bf16)]
            + [pltpu.VMEM((2, hps, 1, ATT_BLK), _f32)] * slot_stats
            + [pltpu.VMEM((hps, 1, ATT_BLK), _f32)] * n_stat
            + [pltpu.VMEM((hps, VDIM, ATT_BLK), _f32)])


def _mla_attention(q_cat, k_cat, v, heads):
    b, lp, _ = q_cat.shape
    hps = MLA_HEADS_PER_STEP if heads % MLA_HEADS_PER_STEP == 0 else 1
    qk_spec = pl.BlockSpec((None, lp, hps * QK_CAT), lambda bi, h: (bi, 0, h))
    v_spec = pl.BlockSpec((None, lp, hps * VDIM), lambda bi, h: (bi, 0, h))
    return pl.pallas_call(
        functools.partial(_mla_kernel, lp=lp, hps=hps),
        grid=(b, heads // hps),
        in_specs=[qk_spec, qk_spec, v_spec],
        out_specs=v_spec,
        out_shape=jax.ShapeDtypeStruct((b, lp, heads * VDIM), _bf16),
        scratch_shapes=_att_scratch(hps, 1, 2),
        compiler_params=_params("parallel", "parallel"),
        name="mla_attention",
    )(q_cat, k_cat, v)


def _sb_attention(big, heads, q_col, k_col, v_col):
    b, lp, _ = big.shape
    hps = SB_HEADS_PER_STEP if heads % SB_HEADS_PER_STEP == 0 else 1
    w = hps * VDIM

    def spec(col0):
        return pl.BlockSpec((None, lp, w), lambda bi, h: (bi, 0, col0 // w + h))

    return pl.pallas_call(
        functools.partial(_sb_kernel, lp=lp, hps=hps),
        grid=(b, heads // hps),
        in_specs=[spec(q_col), spec(k_col), spec(v_col)],
        out_specs=pl.BlockSpec((None, lp, w), lambda bi, h: (bi, 0, h)),
        out_shape=jax.ShapeDtypeStruct((b, lp, heads * VDIM), _bf16),
        scratch_shapes=_att_scratch(hps, 0, 1),
        compiler_params=_params("parallel", "parallel"),
        name="sb_attention",
    )(big, big, big)


def _gate_kernel(ym_ref, ys_ref, zm_ref, zs_ref, gm_ref, gs_ref, o_ref, *, w):
    def one(y_ref, z_ref, g_ref):
        z = z_ref[...].astype(_f32)
        silu = z / (1.0 + jnp.exp(-z))
        return (_rms(y_ref[...].astype(_f32), g_ref[...]) * silu).astype(o_ref.dtype)

    o_ref[:, 0:w] = one(ym_ref, zm_ref, gm_ref)
    o_ref[:, w:2 * w] = one(ys_ref, zs_ref, gs_ref)


def _gate(y_mla, y_sb, big, g_mla, g_sb, zm_col, zs_col, tm):
    m, w = y_mla.shape
    yspec = pl.BlockSpec((tm, w), lambda i: (i, 0))
    gspec = pl.BlockSpec((1, w), lambda i: (0, 0))
    return pl.pallas_call(
        functools.partial(_gate_kernel, w=w),
        grid=(m // tm,),
        in_specs=[yspec, yspec,
                  pl.BlockSpec((tm, w), lambda i: (i, zm_col // w)),
                  pl.BlockSpec((tm, w), lambda i: (i, zs_col // w)),
                  gspec, gspec],
        out_specs=pl.BlockSpec((tm, 2 * w), lambda i: (i, 0)),
        out_shape=jax.ShapeDtypeStruct((m, 2 * w), _bf16),
        compiler_params=_params("parallel"),
        name="gate",
    )(y_mla, y_sb, big, big, g_mla.reshape(1, w), g_sb.reshape(1, w))


def _rope_tables(lp):
    inv_freq = ROPE_THETA ** (-jnp.arange(0, ROPE, 2, dtype=_f32) / ROPE)
    ang = jnp.arange(lp, dtype=jnp.int32).astype(_f32)[:, None] * inv_freq[None, :]
    cos, sin = jnp.cos(ang), jnp.sin(ang)
    zeros = jnp.zeros((lp, LANE - ROPE), _f32)
    return (jnp.concatenate([cos, cos, zeros], axis=1),
            jnp.concatenate([-sin, sin, zeros], axis=1))


@jax.jit
def _forward(x, meta_tokens, g_norm, w_in, g_q, g_kv, w_uq, w_ukv, g_out_mla, g_out_sb,
             w_o, g_final):
    b, seq, d = x.shape
    depth = w_in.shape[0]
    ql, kvl = g_q.shape[1], g_kv.shape[1]
    heads = w_uq.shape[2] // (NOPE + ROPE)
    w_grp = heads * VDIM
    assert w_ukv.shape[2] == heads * (NOPE + VDIM)
    assert w_in.shape[2] == ql + kvl + ROPE + 5 * w_grp
    assert ql % kvl == 0 and (ql + kvl) % LANE == 0 and kvl % LANE == 0

    l_real = N_META + seq
    lp = -(-l_real // LANE) * LANE
    mp = b * lp

    n_small = ql + kvl + ROPE
    w_small = jnp.pad(w_in[:, :, :n_small], ((0, 0), (0, 0), (0, LANE - ROPE))).astype(_bf16)
    w_big = w_in[:, :, n_small:].astype(_bf16)
    zm_col, qs_col, ks_col, vs_col, zs_col = (i * w_grp for i in range(5))
    colscale = jnp.ones((1, 5 * w_grp), _f32).at[:, qs_col:qs_col + w_grp].set(SB_SCALE2)
    wq = w_uq.reshape(depth, ql, heads, NOPE + ROPE)
    wq = jnp.pad(wq, ((0, 0), (0, 0), (0, 0), (0, QK_CAT - NOPE - ROPE)))
    wq = wq.reshape(depth, ql, heads * QK_CAT).astype(_bf16)
    wkv = w_ukv.reshape(depth, kvl, heads, NOPE + VDIM)
    wk = wkv[..., :NOPE].reshape(depth, kvl, heads * NOPE).astype(_bf16)
    wv = wkv[..., NOPE:].reshape(depth, kvl, heads * VDIM).astype(_bf16)
    wo = w_o.astype(_bf16)
    cos, sin = _rope_tables(lp)

    meta = jnp.broadcast_to(meta_tokens[None].astype(x.dtype), (b, N_META, d))
    h = jnp.concatenate([meta, x, jnp.zeros((b, lp - l_real, d), x.dtype)], axis=1)
    h = h.reshape(mp, d)

    tm_norm = _pick_tile(lp, 384)
    tm_mm = _pick_tile(lp, 528)
    tm_proj = _pick_tile(lp, 1056)
    tn_big = 1024 if (5 * w_grp) % 1024 == 0 else 512
    tn_out = 1024 if d % 1024 == 0 else 512

    for i in range(depth):
        u = _rmsnorm(h, g_norm[i], _bf16, tm_norm)
        small = _matmul_small(u, w_small[i], tm_mm)
        big = _matmul_colscale(u, w_big[i], colscale, _bf16, tm_mm, tn_big)
        q_cat = _q_proj(small, g_q[i], wq[i], cos, sin, lp, tm_proj, ql)
        k_cat, v_mla = _kv_proj(small, g_kv[i], wk[i], wv[i], cos, sin, lp, tm_proj, ql, kvl)
        y_mla = _mla_attention(q_cat.reshape(b, lp, -1), k_cat.reshape(b, lp, -1),
                               v_mla.reshape(b, lp, -1), heads)
        y_sb = _sb_attention(big.reshape(b, lp, -1), heads, qs_col, ks_col, vs_col)
        y = _gate(y_mla.reshape(mp, w_grp), y_sb.reshape(mp, w_grp), big,
                  g_out_mla[i], g_out_sb[i], zm_col, zs_col, tm_norm)
        h = _matmul_residual(y, wo[i], h, tm_mm, tn_out)

    h_real = h.reshape(b, lp, d)[:, N_META:l_real].reshape(b * seq, d)
    out = _rmsnorm(h_real, g_final, x.dtype, _pick_tile(seq, 512))
    return out.reshape(b, seq, d)


def kernel(x, meta_tokens, g_norm, w_in, g_q, g_kv, w_uq, w_ukv, g_out_mla, g_out_sb, w_o, g_final):
    return _forward(x, meta_tokens, g_norm, w_in, g_q, g_kv, w_uq, w_ukv, g_out_mla, g_out_sb,
                    w_o, g_final)
```

```python
import functools
import math

import jax
import jax.numpy as jnp
from jax import lax
from jax.experimental import pallas as pl
from jax.experimental.pallas import tpu as pltpu

N_META = 16
NOPE = 128
ROPE = 64
VDIM = 128
QK_CAT = 256
ROPE_THETA = 10000.0
EPS = 1e-6
LOG2E = math.log2(math.e)
MLA_SCALE2 = LOG2E / math.sqrt(NOPE + ROPE)
SB_SCALE2 = LOG2E / math.sqrt(VDIM)
LANE = 128
BF16_SUBLANE = 16
ATT_BLK = 256
MLA_HEADS_PER_STEP = 4
SB_HEADS_PER_STEP = 4
VMEM_LIMIT = 56 * 1024 * 1024
NEG_BIG = -1e30

_f32 = jnp.float32
_bf16 = jnp.bfloat16


def _pick_tile(n, target):
    best = None
    for t in range(BF16_SUBLANE, min(n, target) + 1, BF16_SUBLANE):
        if n % t == 0:
            best = t
    assert best is not None, (n, target)
    return best


def _params(*sem):
    return pltpu.CompilerParams(dimension_semantics=sem, vmem_limit_bytes=VMEM_LIMIT)


def _rms(xf, g):
    return xf * lax.rsqrt(jnp.mean(xf * xf, axis=-1, keepdims=True) + EPS) * g


def _rmsnorm_kernel(x_ref, g_ref, o_ref):
    o_ref[...] = _rms(x_ref[...].astype(_f32), g_ref[...]).astype(o_ref.dtype)


def _rmsnorm(x, g, out_dtype, tm):
    m, d = x.shape
    return pl.pallas_call(
        _rmsnorm_kernel,
        grid=(m // tm,),
        in_specs=[pl.BlockSpec((tm, d), lambda i: (i, 0)),
                  pl.BlockSpec((1, d), lambda i: (0, 0))],
        out_specs=pl.BlockSpec((tm, d), lambda i: (i, 0)),
        out_shape=jax.ShapeDtypeStruct((m, d), out_dtype),
        compiler_params=_params("parallel"),
        name="rmsnorm",
    )(x, g.reshape(1, d))


def _final_norm(h, g, seq, out_dtype, tm):
    b, lp, d = h.shape
    sub = 8
    assert N_META % sub == 0 and lp % sub == 0 and tm % sub == 0

    return pl.pallas_call(
        _rmsnorm_kernel,
        grid=(b, seq // tm),
        in_specs=[pl.BlockSpec((pl.Element(tm), pl.Element(d)),
                               lambda bi, i: (pl.multiple_of(bi * lp + N_META + i * tm, sub), 0)),
                  pl.BlockSpec((1, d), lambda bi, i: (0, 0))],
        out_specs=pl.BlockSpec((None, tm, d), lambda bi, i: (bi, i, 0)),
        out_shape=jax.ShapeDtypeStruct((b, seq, d), out_dtype),
        compiler_params=_params("parallel", "parallel"),
        name="final_norm",
    )(h.reshape(b * lp, d), g.reshape(1, d))


def _mm_scale_kernel(x_ref, w_ref, s_ref, o_ref):
    acc = jnp.dot(x_ref[...], w_ref[...], preferred_element_type=_f32)
    o_ref[...] = (acc * s_ref[...]).astype(o_ref.dtype)


def _matmul_colscale(x, w, layer, colscale, out_dtype, tm, tn):
    m, k = x.shape
    n = w.shape[2]
    return pl.pallas_call(
        _mm_scale_kernel,
        grid=(n // tn, m // tm),
        in_specs=[pl.BlockSpec((tm, k), lambda j, i: (i, 0)),
                  pl.BlockSpec((None, k, tn), lambda j, i: (layer, 0, j)),
                  pl.BlockSpec((1, tn), lambda j, i: (0, j))],
        out_specs=pl.BlockSpec((tm, tn), lambda j, i: (i, j)),
        out_shape=jax.ShapeDtypeStruct((m, n), out_dtype),
        compiler_params=_params("parallel", "parallel"),
        name="matmul_colscale",
    )(x, w, colscale)


def _mm_kernel(x_ref, w_ref, o_ref):
    o_ref[...] = jnp.dot(x_ref[...], w_ref[...], preferred_element_type=_f32).astype(o_ref.dtype)


def _matmul_small(x, w, layer, tm):
    m, k = x.shape
    n = w.shape[2]
    return pl.pallas_call(
        _mm_kernel,
        grid=(m // tm,),
        in_specs=[pl.BlockSpec((tm, k), lambda i: (i, 0)),
                  pl.BlockSpec((None, k, n), lambda i: (layer, 0, 0))],
        out_specs=pl.BlockSpec((tm, n), lambda i: (i, 0)),
        out_shape=jax.ShapeDtypeStruct((m, n), _f32),
        compiler_params=_params("parallel"),
        name="matmul_small",
    )(x, w)


def _mm_res_kernel(x_ref, w_ref, r_ref, o_ref):
    o_ref[...] = r_ref[...] + jnp.dot(x_ref[...], w_ref[...], preferred_element_type=_f32)


def _matmul_residual(x, w, layer, res, tm, tn):
    m, k = x.shape
    n = w.shape[2]
    return pl.pallas_call(
        _mm_res_kernel,
        grid=(n // tn, m // tm),
        in_specs=[pl.BlockSpec((tm, k), lambda j, i: (i, 0)),
                  pl.BlockSpec((None, k, tn), lambda j, i: (layer, 0, j)),
                  pl.BlockSpec((tm, tn), lambda j, i: (i, j))],
        out_specs=pl.BlockSpec((tm, tn), lambda j, i: (i, j)),
        out_shape=jax.ShapeDtypeStruct((m, n), _f32),
        compiler_params=_params("parallel", "parallel"),
        name="matmul_residual",
    )(x, w, res)


def _rope128(x, cos, sin_signed):
    lane = lax.broadcasted_iota(jnp.int32, x.shape, 1)
    partner = jnp.where(lane < ROPE // 2,
                        pltpu.roll(x, LANE - ROPE // 2, 1),
                        pltpu.roll(x, ROPE // 2, 1))
    return x * cos + partner * sin_signed


def _qproj_kernel(x_ref, g_ref, w_ref, cos_ref, sin_ref, o_ref, xn_ref, *, heads_per_step):
    @pl.when(pl.program_id(1) == 0)
    def _():
        xn_ref[...] = _rms(x_ref[...], g_ref[...]).astype(_bf16)

    acc = jnp.dot(xn_ref[...], w_ref[...], preferred_element_type=_f32)
    cos = cos_ref[...]
    sin = sin_ref[...]
    for hh in range(heads_per_step):
        c0 = hh * QK_CAT
        o_ref[:, c0:c0 + NOPE] = (acc[:, c0:c0 + NOPE] * MLA_SCALE2).astype(o_ref.dtype)
        roped = _rope128(acc[:, c0 + NOPE:c0 + QK_CAT], cos, sin)
        o_ref[:, c0 + NOPE:c0 + QK_CAT] = (roped * MLA_SCALE2).astype(o_ref.dtype)


def _q_proj(small, g_q, w_q, layer, cos, sin, lp, tm, ql):
    m = small.shape[0]
    n = w_q.shape[2]
    hps = 2 if (n // QK_CAT) % 2 == 0 else 1
    tn = hps * QK_CAT
    nrow = lp // tm
    return pl.pallas_call(
        functools.partial(_qproj_kernel, heads_per_step=hps),
        grid=(m // tm, n // tn),
        in_specs=[pl.BlockSpec((tm, ql), lambda i, j: (i, 0)),
                  pl.BlockSpec((1, ql), lambda i, j: (0, 0)),
                  pl.BlockSpec((None, ql, tn), lambda i, j: (layer, 0, j)),
                  pl.BlockSpec((tm, LANE), lambda i, j: (i % nrow, 0)),
                  pl.BlockSpec((tm, LANE), lambda i, j: (i % nrow, 0))],
        out_specs=pl.BlockSpec((tm, tn), lambda i, j: (i, j)),
        out_shape=jax.ShapeDtypeStruct((m, n), _bf16),
        scratch_shapes=[pltpu.VMEM((tm, ql), _bf16)],
        compiler_params=_params("parallel", "arbitrary"),
        name="q_proj",
    )(small, g_q.reshape(1, ql), w_q, cos, sin)


def _kvproj_kernel(x_ref, kr_ref, g_ref, wk_ref, wv_ref, cos_ref, sin_ref,
                   k_ref, v_ref, krr_ref, xn_ref):
    @pl.when(pl.program_id(1) == 0)
    def _():
        xn_ref[...] = _rms(x_ref[...], g_ref[...]).astype(_bf16)
        krr_ref[...] = _rope128(kr_ref[...], cos_ref[...], sin_ref[...]).astype(krr_ref.dtype)

    xn = xn_ref[...]
    k_ref[...] = jnp.dot(xn, wk_ref[...], preferred_element_type=_f32).astype(k_ref.dtype)
    v_ref[...] = jnp.dot(xn, wv_ref[...], preferred_element_type=_f32).astype(v_ref.dtype)


def _kv_proj(small, g_kv, w_k, w_v, layer, cos, sin, lp, tm, ql, kvl):
    m = small.shape[0]
    heads = w_k.shape[2] // NOPE
    hps = 4 if heads % 4 == 0 else 1
    nrow = lp // tm
    return pl.pallas_call(
        _kvproj_kernel,
        grid=(m // tm, heads // hps),
        in_specs=[pl.BlockSpec((tm, kvl), lambda i, j: (i, ql // kvl)),
                  pl.BlockSpec((tm, LANE), lambda i, j: (i, (ql + kvl) // LANE)),
                  pl.BlockSpec((1, kvl), lambda i, j: (0, 0)),
                  pl.BlockSpec((None, kvl, hps * NOPE), lambda i, j: (layer, 0, j)),
                  pl.BlockSpec((None, kvl, hps * VDIM), lambda i, j: (layer, 0, j)),
                  pl.BlockSpec((tm, LANE), lambda i, j: (i % nrow, 0)),
                  pl.BlockSpec((tm, LANE), lambda i, j: (i % nrow, 0))],
        out_specs=[pl.BlockSpec((tm, hps * NOPE), lambda i, j: (i, j)),
                   pl.BlockSpec((tm, hps * VDIM), lambda i, j: (i, j)),
                   pl.BlockSpec((tm, LANE), lambda i, j: (i, 0))],
        out_shape=[jax.ShapeDtypeStruct((m, heads * NOPE), _bf16),
                   jax.ShapeDtypeStruct((m, heads * VDIM), _bf16),
                   jax.ShapeDtypeStruct((m, LANE), _bf16)],
        scratch_shapes=[pltpu.VMEM((tm, kvl), _bf16)],
        compiler_params=_params("parallel", "arbitrary"),
        name="kv_proj",
    )(small, small, g_kv.reshape(1, kvl), w_k, w_v, cos, sin)


def _dot_nt(a, b):
    return lax.dot_general(a, b, (((1,), (1,)), ((), ())), preferred_element_type=_f32)


def _dot_tn(a, b):
    return lax.dot_general(a, b, (((0,), (0,)), ((), ())), preferred_element_type=_f32)


def _row_blocks(lp):
    n_main = lp // ATT_BLK
    tail = lp - n_main * ATT_BLK
    return n_main, tail


def _key_query_iota(bk, bq):
    key = lax.broadcasted_iota(jnp.int32, (bk, bq), 0)
    qry = lax.broadcasted_iota(jnp.int32, (bk, bq), 1)
    return key, qry


def _mla_kernel(q_ref, k_ref, kr_ref, v_ref, o_ref, s_ref, p_ref, al_ref, m_ref, l_ref, acc_ref,
                *, lp, hps):
    n_main, tail = _row_blocks(lp)

    def q_block(q0, bq, n_full, is_tail):
        qs = [q_ref[pl.ds(q0, bq), a * QK_CAT:(a + 1) * QK_CAT] for a in range(hps)]

        def scores(a, k0, bk):
            k_cat = jnp.concatenate([k_ref[pl.ds(k0, bk), a * NOPE:(a + 1) * NOPE],
                                     kr_ref[pl.ds(k0, bk), :]], axis=1)
            return _dot_nt(k_cat, qs[a])

        def probs(a, s, slot, bk, masked):
            if masked:
                key, qry = _key_query_iota(bk, bq)
                s = jnp.where(key <= qry, s, NEG_BIG)
            m_prev = m_ref[a, :, 0:bq]
            m_new = jnp.maximum(m_prev, jnp.max(s, axis=0, keepdims=True))
            alpha = jnp.exp2(m_prev - m_new)
            p = jnp.exp2(s - m_new)
            l_ref[a, :, 0:bq] = alpha * l_ref[a, :, 0:bq] + jnp.sum(p, axis=0, keepdims=True)
            m_ref[a, :, 0:bq] = m_new
            p_ref[slot, a, 0:bk, 0:bq] = p.astype(_bf16)
            al_ref[slot, a, :, 0:bq] = alpha

        def values(a, slot, k0, bk):
            pv = _dot_tn(v_ref[pl.ds(k0, bk), a * VDIM:(a + 1) * VDIM], p_ref[slot, a, 0:bk, 0:bq])
            acc_ref[a, :, 0:bq] = al_ref[slot, a, :, 0:bq] * acc_ref[a, :, 0:bq] + pv

        for a in range(hps):
            m_ref[a, :, 0:bq] = jnp.full((1, bq), NEG_BIG, _f32)
            l_ref[a, :, 0:bq] = jnp.zeros((1, bq), _f32)
            acc_ref[a, :, 0:bq] = jnp.zeros((VDIM, bq), _f32)
            s0 = scores(a, 0, ATT_BLK)
            for slot in range(2):
                p_ref[slot, a, :, 0:bq] = jnp.zeros((ATT_BLK, bq), _bf16)
                al_ref[slot, a, :, 0:bq] = jnp.ones((1, bq), _f32)
                s_ref[slot, a, :, 0:bq] = s0

        last = n_full - 1 if is_tail else n_full

        def step(c, slot):
            nxt = pl.multiple_of(jnp.minimum(c + 1, last) * ATT_BLK, ATT_BLK)
            prv = pl.multiple_of(jnp.maximum(c - 1, 0) * ATT_BLK, ATT_BLK)
            for a in range(hps):
                s_ref[1 - slot, a, :, 0:bq] = scores(a, nxt, ATT_BLK)
                probs(a, s_ref[slot, a, :, 0:bq], slot, ATT_BLK, False)
                values(a, 1 - slot, prv, ATT_BLK)

        odd = n_full % 2
        if is_tail:
            if odd:
                step(0, 1)
        else:
            @pl.when(odd == 1)
            def _():
                step(0, 1)

        def pair(t, carry):
            c = odd + 2 * t
            step(c, 0)
            step(c + 1, 1)
            return carry

        lax.fori_loop(0, n_full // 2, pair, 0)
        prv = pl.multiple_of(jnp.maximum(n_full - 1, 0) * ATT_BLK, ATT_BLK)
        for a in range(hps):
            s = scores(a, q0, bq) if is_tail else s_ref[0, a, :, 0:bq]
            probs(a, s, 0, bq, True)
            values(a, 1, prv, ATT_BLK)
            values(a, 0, q0, bq)
            out_t = acc_ref[a, :, 0:bq] * (1.0 / l_ref[a, :, 0:bq])
            o_ref[pl.ds(q0, bq), a * VDIM:(a + 1) * VDIM] = out_t.T.astype(o_ref.dtype)

    def main_block(i, carry):
        q_block(pl.multiple_of(i * ATT_BLK, ATT_BLK), ATT_BLK, i, False)
        return carry

    lax.fori_loop(0, n_main, main_block, 0)
    if tail:
        q_block(n_main * ATT_BLK, tail, n_main, True)


def _sb_kernel(q_ref, k_ref, v_ref, o_ref, s_ref, w_ref, r_ref, acc_ref, *, lp, hps):
    n_main, tail = _row_blocks(lp)
    key_i, key_j = _key_query_iota(ATT_BLK, ATT_BLK)
    later = (key_j > key_i).astype(_bf16)

    def q_block(q0, bq, n_full, is_tail):
        qs = [q_ref[pl.ds(q0, bq), a * VDIM:(a + 1) * VDIM] for a in range(hps)]

        def scores(a, k0, bk):
            return _dot_nt(k_ref[pl.ds(k0, bk), a * VDIM:(a + 1) * VDIM], qs[a])

        def weights(a, z, slot, bk, masked):
            tail_sp = jnp.log2(1.0 + jnp.exp2(-jnp.abs(z)))
            log_beta = jnp.minimum(z, 0.0) - tail_sp
            log_rest = log_beta - z
            if masked:
                key, qry = _key_query_iota(bk, bq)
                strict = key < qry
                log_rest = jnp.where(strict, log_rest, 0.0)
            within = jnp.dot(later[0:bk, 0:bk], log_rest.astype(_bf16), preferred_element_type=_f32)
            r = r_ref[a, :, 0:bq]
            w = jnp.exp2(log_beta + within + r)
            if masked:
                w = jnp.where(strict, w, 0.0)
            w_ref[slot, a, 0:bk, 0:bq] = w.astype(_bf16)
            r_ref[a, :, 0:bq] = r + within[0:1, :] + log_rest[0:1, :]

        def values(a, slot, k0, bk):
            acc_ref[a, :, 0:bq] = acc_ref[a, :, 0:bq] + _dot_tn(
                v_ref[pl.ds(k0, bk), a * VDIM:(a + 1) * VDIM], w_ref[slot, a, 0:bk, 0:bq])

        k_first = pl.multiple_of(jnp.maximum(n_full - 1, 0) * ATT_BLK, ATT_BLK)
        for a in range(hps):
            r_ref[a, :, 0:bq] = jnp.zeros((1, bq), _f32)
            acc_ref[a, :, 0:bq] = jnp.zeros((VDIM, bq), _f32)
            z = scores(a, q0, bq)
            z_first = scores(a, k_first, ATT_BLK)
            s_ref[0, a, :, 0:bq] = z_first
            s_ref[1, a, :, 0:bq] = z_first
            weights(a, z, 0, bq, True)
            if is_tail:
                values(a, 0, q0, bq)
                w_ref[0, a, :, 0:bq] = jnp.zeros((ATT_BLK, bq), _bf16)
                w_ref[1, a, :, 0:bq] = jnp.zeros((ATT_BLK, bq), _bf16)
            else:
                w_ref[1, a, :, 0:bq] = w_ref[0, a, :, 0:bq]

        v_last = n_full - 1 if is_tail else n_full

        def step(c, slot):
            k_before = pl.multiple_of(jnp.maximum(c - 1, 0) * ATT_BLK, ATT_BLK)
            k_after = pl.multiple_of(jnp.minimum(c + 1, v_last) * ATT_BLK, ATT_BLK)
            for a in range(hps):
                z = s_ref[slot, a, :, 0:bq]
                s_ref[1 - slot, a, :, 0:bq] = scores(a, k_before, ATT_BLK)
                weights(a, z, slot, ATT_BLK, False)
                values(a, 1 - slot, k_after, ATT_BLK)

        odd = n_full % 2
        if is_tail:
            if odd:
                step(n_full - 1, 0)
        else:
            @pl.when(odd == 1)
            def _():
                step(n_full - 1, 0)

        n_pairs = n_full // 2

        def pair(tt, carry):
            c = 2 * (n_pairs - 1 - tt) + 1
            step(c, 1)
            step(c - 1, 0)
            return carry

        lax.fori_loop(0, n_pairs, pair, 0)
        for a in range(hps):
            values(a, 0, 0, ATT_BLK)
            o_ref[pl.ds(q0, bq), a * VDIM:(a + 1) * VDIM] = acc_ref[a, :, 0:bq].T.astype(o_ref.dtype)

    def main_block(i, carry):
        q_block(pl.multiple_of(i * ATT_BLK, ATT_BLK), ATT_BLK, i, False)
        return carry

    lax.fori_loop(0, n_main, main_block, 0)
    if tail:
        q_block(n_main * ATT_BLK, tail, n_main, True)


def _att_scratch(hps, slot_stats, n_stat):
    return ([pltpu.VMEM((2, hps, ATT_BLK, ATT_BLK), _f32),
             pltpu.VMEM((2, hps, ATT_BLK, ATT_BLK), _bf16)]
            + [pltpu.VMEM((2, hps, 1, ATT_BLK), _f32)] * slot_stats
            + [pltpu.VMEM((hps, 1, ATT_BLK), _f32)] * n_stat
            + [pltpu.VMEM((hps, VDIM, ATT_BLK), _f32)])


def _mla_attention(q_cat, k_nope, k_rope, v, heads):
    b, lp, _ = q_cat.shape
    hps = MLA_HEADS_PER_STEP if heads % MLA_HEADS_PER_STEP == 0 else 1
    q_spec = pl.BlockSpec((None, lp, hps * QK_CAT), lambda bi, h: (bi, 0, h))
    kr_spec = pl.BlockSpec((None, lp, LANE), lambda bi, h: (bi, 0, 0))
    v_spec = pl.BlockSpec((None, lp, hps * VDIM), lambda bi, h: (bi, 0, h))
    return pl.pallas_call(
        functools.partial(_mla_kernel, lp=lp, hps=hps),
        grid=(b, heads // hps),
        in_specs=[q_spec, v_spec, kr_spec, v_spec],
        out_specs=v_spec,
        out_shape=jax.ShapeDtypeStruct((b, lp, heads * VDIM), _bf16),
        scratch_shapes=_att_scratch(hps, 1, 2),
        compiler_params=_params("parallel", "parallel"),
        name="mla_attention",
    )(q_cat, k_nope, k_rope, v)


def _sb_attention(big, heads, q_col, k_col, v_col):
    b, lp, _ = big.shape
    hps = SB_HEADS_PER_STEP if heads % SB_HEADS_PER_STEP == 0 else 1
    w = hps * VDIM

    def spec(col0):
        return pl.BlockSpec((None, lp, w), lambda bi, h: (bi, 0, col0 // w + h))

    return pl.pallas_call(
        functools.partial(_sb_kernel, lp=lp, hps=hps),
        grid=(b, heads // hps),
        in_specs=[spec(q_col), spec(k_col), spec(v_col)],
        out_specs=pl.BlockSpec((None, lp, w), lambda bi, h: (bi, 0, h)),
        out_shape=jax.ShapeDtypeStruct((b, lp, heads * VDIM), _bf16),
        scratch_shapes=_att_scratch(hps, 0, 1),
        compiler_params=_params("parallel", "parallel"),
        name="sb_attention",
    )(big, big, big)


def _gate_kernel(ym_ref, ys_ref, zm_ref, zs_ref, gm_ref, gs_ref, o_ref, *, w):
    def one(y_ref, z_ref, g_ref):
        z = z_ref[...].astype(_f32)
        silu = z / (1.0 + jnp.exp(-z))
        return (_rms(y_ref[...].astype(_f32), g_ref[...]) * silu).astype(o_ref.dtype)

    o_ref[:, 0:w] = one(ym_ref, zm_ref, gm_ref)
    o_ref[:, w:2 * w] = one(ys_ref, zs_ref, gs_ref)


def _gate(y_mla, y_sb, big, g_mla, g_sb, zm_col, zs_col, tm):
    m, w = y_mla.shape
    yspec = pl.BlockSpec((tm, w), lambda i: (i, 0))
    gspec = pl.BlockSpec((1, w), lambda i: (0, 0))
    return pl.pallas_call(
        functools.partial(_gate_kernel, w=w),
        grid=(m // tm,),
        in_specs=[yspec, yspec,
                  pl.BlockSpec((tm, w), lambda i: (i, zm_col // w)),
                  pl.BlockSpec((tm, w), lambda i: (i, zs_col // w)),
                  gspec, gspec],
        out_specs=pl.BlockSpec((tm, 2 * w), lambda i: (i, 0)),
        out_shape=jax.ShapeDtypeStruct((m, 2 * w), _bf16),
        compiler_params=_params("parallel"),
        name="gate",
    )(y_mla, y_sb, big, big, g_mla.reshape(1, w), g_sb.reshape(1, w))


def _rope_tables(lp):
    inv_freq = ROPE_THETA ** (-jnp.arange(0, ROPE, 2, dtype=_f32) / ROPE)
    ang = jnp.arange(lp, dtype=jnp.int32).astype(_f32)[:, None] * inv_freq[None, :]
    cos, sin = jnp.cos(ang), jnp.sin(ang)
    zeros = jnp.zeros((lp, LANE - ROPE), _f32)
    return (jnp.concatenate([cos, cos, zeros], axis=1),
            jnp.concatenate([-sin, sin, zeros], axis=1))


@jax.jit
def _forward(x, meta_tokens, g_norm, w_in, g_q, g_kv, w_uq, w_ukv, g_out_mla, g_out_sb,
             w_o, g_final):
    b, seq, d = x.shape
    depth = w_in.shape[0]
    ql, kvl = g_q.shape[1], g_kv.shape[1]
    heads = w_uq.shape[2] // (NOPE + ROPE)
    w_grp = heads * VDIM
    assert w_ukv.shape[2] == heads * (NOPE + VDIM)
    assert w_in.shape[2] == ql + kvl + ROPE + 5 * w_grp
    assert ql % kvl == 0 and (ql + kvl) % LANE == 0 and kvl % LANE == 0

    l_real = N_META + seq
    lp = -(-l_real // LANE) * LANE
    mp = b * lp

    n_small = ql + kvl + ROPE
    w_small = jnp.pad(w_in[:, :, :n_small], ((0, 0), (0, 0), (0, LANE - ROPE))).astype(_bf16)
    w_big = w_in[:, :, n_small:].astype(_bf16)
    zm_col, qs_col, ks_col, vs_col, zs_col = (i * w_grp for i in range(5))
    colscale = jnp.ones((1, 5 * w_grp), _f32).at[:, qs_col:qs_col + w_grp].set(SB_SCALE2)
    wq = w_uq.reshape(depth, ql, heads, NOPE + ROPE)
    wq = jnp.pad(wq, ((0, 0), (0, 0), (0, 0), (0, QK_CAT - NOPE - ROPE)))
    wq = wq.reshape(depth, ql, heads * QK_CAT).astype(_bf16)
    wkv = w_ukv.reshape(depth, kvl, heads, NOPE + VDIM)
    wk = wkv[..., :NOPE].reshape(depth, kvl, heads * NOPE).astype(_bf16)
    wv = wkv[..., NOPE:].reshape(depth, kvl, heads * VDIM).astype(_bf16)
    wo = w_o.astype(_bf16)
    cos, sin = _rope_tables(lp)

    meta = jnp.broadcast_to(meta_tokens[None].astype(x.dtype), (b, N_META, d))
    h = jnp.concatenate([meta, x, jnp.zeros((b, lp - l_real, d), x.dtype)], axis=1)
    h = h.reshape(mp, d)

    tm_norm = _pick_tile(lp, 384)
    tm_mm = _pick_tile(lp, 528)
    tm_proj = _pick_tile(lp, 1056)
    tn_big = 1024 if (5 * w_grp) % 1024 == 0 else 512
    tn_out = 1024 if d % 1024 == 0 else 512

    for i in range(depth):
        u = _rmsnorm(h, g_norm[i], _bf16, tm_norm)
        small = _matmul_small(u, w_small, i, tm_mm)
        big = _matmul_colscale(u, w_big, i, colscale, _bf16, tm_mm, tn_big)
        q_cat = _q_proj(small, g_q[i], wq, i, cos, sin, lp, tm_proj, ql)
        k_nope, v_mla, k_rope = _kv_proj(small, g_kv[i], wk, wv, i, cos, sin, lp, tm_proj, ql, kvl)
        y_mla = _mla_attention(q_cat.reshape(b, lp, -1), k_nope.reshape(b, lp, -1),
                               k_rope.reshape(b, lp, -1), v_mla.reshape(b, lp, -1), heads)
        y_sb = _sb_attention(big.reshape(b, lp, -1), heads, qs_col, ks_col, vs_col)
        y = _gate(y_mla.reshape(mp, w_grp), y_sb.reshape(mp, w_grp), big,
                  g_out_mla[i], g_out_sb[i], zm_col, zs_col, tm_norm)
        h = _matmul_residual(y, wo, i, h, tm_mm, tn_out)

    return _final_norm(h.reshape(b, lp, d), g_final, seq, x.dtype, _pick_tile(seq, 512))


def kernel(x, meta_tokens, g_norm, w_in, g_q, g_kv, w_uq, w_ukv, g_out_mla, g_out_sb, w_o, g_final):
    return _forward(x, meta_tokens, g_norm, w_in, g_q, g_kv, w_uq, w_ukv, g_out_mla, g_out_sb,
                    w_o, g_final)
```

```python
import functools
import math

import jax
import jax.numpy as jnp
from jax import lax
from jax.experimental import pallas as pl
from jax.experimental.pallas import tpu as pltpu

N_META = 16
NOPE = 128
ROPE = 64
VDIM = 128
QK_CAT = 256
ROPE_THETA = 10000.0
EPS = 1e-6
LOG2E = math.log2(math.e)
MLA_SCALE2 = LOG2E / math.sqrt(NOPE + ROPE)
SB_SCALE2 = LOG2E / math.sqrt(VDIM)
LANE = 128
BF16_SUBLANE = 16
ATT_BLK = 256
Q_BLK = 2 * ATT_BLK
MLA_HEADS_PER_STEP = 4
SB_HEADS_PER_STEP = 4
VMEM_LIMIT = 56 * 1024 * 1024
NEG_BIG = -1e30

_f32 = jnp.float32
_bf16 = jnp.bfloat16


def _pick_tile(n, target):
    best = None
    for t in range(BF16_SUBLANE, min(n, target) + 1, BF16_SUBLANE):
        if n % t == 0:
            best = t
    assert best is not None, (n, target)
    return best


def _params(*sem):
    return pltpu.CompilerParams(dimension_semantics=sem, vmem_limit_bytes=VMEM_LIMIT)


def _rms(xf, g):
    return xf * lax.rsqrt(jnp.mean(xf * xf, axis=-1, keepdims=True) + EPS) * g


def _rmsnorm_kernel(x_ref, g_ref, o_ref):
    o_ref[...] = _rms(x_ref[...].astype(_f32), g_ref[...]).astype(o_ref.dtype)


def _rmsnorm(x, g, out_dtype, tm):
    m, d = x.shape
    return pl.pallas_call(
        _rmsnorm_kernel,
        grid=(m // tm,),
        in_specs=[pl.BlockSpec((tm, d), lambda i: (i, 0)),
                  pl.BlockSpec((1, d), lambda i: (0, 0))],
        out_specs=pl.BlockSpec((tm, d), lambda i: (i, 0)),
        out_shape=jax.ShapeDtypeStruct((m, d), out_dtype),
        compiler_params=_params("parallel"),
        name="rmsnorm",
    )(x, g.reshape(1, d))


def _final_norm(h, g, seq, out_dtype, tm):
    b, lp, d = h.shape
    sub = 8
    assert N_META % sub == 0 and lp % sub == 0 and tm % sub == 0

    return pl.pallas_call(
        _rmsnorm_kernel,
        grid=(b, seq // tm),
        in_specs=[pl.BlockSpec((pl.Element(tm), pl.Element(d)),
                               lambda bi, i: (pl.multiple_of(bi * lp + N_META + i * tm, sub), 0)),
                  pl.BlockSpec((1, d), lambda bi, i: (0, 0))],
        out_specs=pl.BlockSpec((None, tm, d), lambda bi, i: (bi, i, 0)),
        out_shape=jax.ShapeDtypeStruct((b, seq, d), out_dtype),
        compiler_params=_params("parallel", "parallel"),
        name="final_norm",
    )(h.reshape(b * lp, d), g.reshape(1, d))


def _mm_scale_kernel(x_ref, w_ref, s_ref, o_ref):
    acc = jnp.dot(x_ref[...], w_ref[...], preferred_element_type=_f32)
    o_ref[...] = (acc * s_ref[...]).astype(o_ref.dtype)


def _matmul_colscale(x, w, layer, colscale, out_dtype, tm, tn):
    m, k = x.shape
    n = w.shape[2]
    return pl.pallas_call(
        _mm_scale_kernel,
        grid=(n // tn, m // tm),
        in_specs=[pl.BlockSpec((tm, k), lambda j, i: (i, 0)),
                  pl.BlockSpec((None, k, tn), lambda j, i: (layer, 0, j)),
                  pl.BlockSpec((1, tn), lambda j, i: (0, j))],
        out_specs=pl.BlockSpec((tm, tn), lambda j, i: (i, j)),
        out_shape=jax.ShapeDtypeStruct((m, n), out_dtype),
        compiler_params=_params("parallel", "parallel"),
        name="matmul_colscale",
    )(x, w, colscale)


def _mm_kernel(x_ref, w_ref, o_ref):
    o_ref[...] = jnp.dot(x_ref[...], w_ref[...], preferred_element_type=_f32).astype(o_ref.dtype)


def _matmul_small(x, w, layer, tm):
    m, k = x.shape
    n = w.shape[2]
    return pl.pallas_call(
        _mm_kernel,
        grid=(m // tm,),
        in_specs=[pl.BlockSpec((tm, k), lambda i: (i, 0)),
                  pl.BlockSpec((None, k, n), lambda i: (layer, 0, 0))],
        out_specs=pl.BlockSpec((tm, n), lambda i: (i, 0)),
        out_shape=jax.ShapeDtypeStruct((m, n), _f32),
        compiler_params=_params("parallel"),
        name="matmul_small",
    )(x, w)


def _mm_res_kernel(x_ref, w_ref, r_ref, o_ref):
    o_ref[...] = r_ref[...] + jnp.dot(x_ref[...], w_ref[...], preferred_element_type=_f32)


def _matmul_residual(x, w, layer, res, tm, tn):
    m, k = x.shape
    n = w.shape[2]
    return pl.pallas_call(
        _mm_res_kernel,
        grid=(n // tn, m // tm),
        in_specs=[pl.BlockSpec((tm, k), lambda j, i: (i, 0)),
                  pl.BlockSpec((None, k, tn), lambda j, i: (layer, 0, j)),
                  pl.BlockSpec((tm, tn), lambda j, i: (i, j))],
        out_specs=pl.BlockSpec((tm, tn), lambda j, i: (i, j)),
        out_shape=jax.ShapeDtypeStruct((m, n), _f32),
        compiler_params=_params("parallel", "parallel"),
        name="matmul_residual",
    )(x, w, res)


def _rope128(x, cos, sin_signed):
    lane = lax.broadcasted_iota(jnp.int32, x.shape, 1)
    partner = jnp.where(lane < ROPE // 2,
                        pltpu.roll(x, LANE - ROPE // 2, 1),
                        pltpu.roll(x, ROPE // 2, 1))
    return x * cos + partner * sin_signed


def _qproj_kernel(x_ref, g_ref, w_ref, cos_ref, sin_ref, o_ref, xn_ref, *, heads_per_step):
    @pl.when(pl.program_id(1) == 0)
    def _():
        xn_ref[...] = _rms(x_ref[...], g_ref[...]).astype(_bf16)

    acc = jnp.dot(xn_ref[...], w_ref[...], preferred_element_type=_f32)
    cos = cos_ref[...]
    sin = sin_ref[...]
    for hh in range(heads_per_step):
        c0 = hh * QK_CAT
        o_ref[:, c0:c0 + NOPE] = (acc[:, c0:c0 + NOPE] * MLA_SCALE2).astype(o_ref.dtype)
        roped = _rope128(acc[:, c0 + NOPE:c0 + QK_CAT], cos, sin)
        o_ref[:, c0 + NOPE:c0 + QK_CAT] = (roped * MLA_SCALE2).astype(o_ref.dtype)


def _q_proj(small, g_q, w_q, layer, cos, sin, lp, tm, ql):
    m = small.shape[0]
    n = w_q.shape[2]
    hps = 2 if (n // QK_CAT) % 2 == 0 else 1
    tn = hps * QK_CAT
    nrow = lp // tm
    return pl.pallas_call(
        functools.partial(_qproj_kernel, heads_per_step=hps),
        grid=(m // tm, n // tn),
        in_specs=[pl.BlockSpec((tm, ql), lambda i, j: (i, 0)),
                  pl.BlockSpec((1, ql), lambda i, j: (0, 0)),
                  pl.BlockSpec((None, ql, tn), lambda i, j: (layer, 0, j)),
                  pl.BlockSpec((tm, LANE), lambda i, j: (i % nrow, 0)),
                  pl.BlockSpec((tm, LANE), lambda i, j: (i % nrow, 0))],
        out_specs=pl.BlockSpec((tm, tn), lambda i, j: (i, j)),
        out_shape=jax.ShapeDtypeStruct((m, n), _bf16),
        scratch_shapes=[pltpu.VMEM((tm, ql), _bf16)],
        compiler_params=_params("parallel", "arbitrary"),
        name="q_proj",
    )(small, g_q.reshape(1, ql), w_q, cos, sin)


def _kvproj_kernel(x_ref, kr_ref, g_ref, wk_ref, wv_ref, cos_ref, sin_ref,
                   k_ref, v_ref, krr_ref, xn_ref):
    @pl.when(pl.program_id(1) == 0)
    def _():
        xn_ref[...] = _rms(x_ref[...], g_ref[...]).astype(_bf16)
        krr_ref[...] = _rope128(kr_ref[...], cos_ref[...], sin_ref[...]).astype(krr_ref.dtype)

    xn = xn_ref[...]
    k_ref[...] = jnp.dot(xn, wk_ref[...], preferred_element_type=_f32).astype(k_ref.dtype)
    v_ref[...] = jnp.dot(xn, wv_ref[...], preferred_element_type=_f32).astype(v_ref.dtype)


def _kv_proj(small, g_kv, w_k, w_v, layer, cos, sin, lp, tm, ql, kvl):
    m = small.shape[0]
    heads = w_k.shape[2] // NOPE
    hps = 4 if heads % 4 == 0 else 1
    nrow = lp // tm
    return pl.pallas_call(
        _kvproj_kernel,
        grid=(m // tm, heads // hps),
        in_specs=[pl.BlockSpec((tm, kvl), lambda i, j: (i, ql // kvl)),
                  pl.BlockSpec((tm, LANE), lambda i, j: (i, (ql + kvl) // LANE)),
                  pl.BlockSpec((1, kvl), lambda i, j: (0, 0)),
                  pl.BlockSpec((None, kvl, hps * NOPE), lambda i, j: (layer, 0, j)),
                  pl.BlockSpec((None, kvl, hps * VDIM), lambda i, j: (layer, 0, j)),
                  pl.BlockSpec((tm, LANE), lambda i, j: (i % nrow, 0)),
                  pl.BlockSpec((tm, LANE), lambda i, j: (i % nrow, 0))],
        out_specs=[pl.BlockSpec((tm, hps * NOPE), lambda i, j: (i, j)),
                   pl.BlockSpec((tm, hps * VDIM), lambda i, j: (i, j)),
                   pl.BlockSpec((tm, LANE), lambda i, j: (i, 0))],
        out_shape=[jax.ShapeDtypeStruct((m, heads * NOPE), _bf16),
                   jax.ShapeDtypeStruct((m, heads * VDIM), _bf16),
                   jax.ShapeDtypeStruct((m, LANE), _bf16)],
        scratch_shapes=[pltpu.VMEM((tm, kvl), _bf16)],
        compiler_params=_params("parallel", "arbitrary"),
        name="kv_proj",
    )(small, small, g_kv.reshape(1, kvl), w_k, w_v, cos, sin)


def _dot_nt(a, b):
    return lax.dot_general(a, b, (((1,), (1,)), ((), ())), preferred_element_type=_f32)


def _dot_tn(a, b):
    return lax.dot_general(a, b, (((0,), (0,)), ((), ())), preferred_element_type=_f32)


def _neg_abs(x):
    bits = lax.bitcast_convert_type(x, jnp.uint32) | jnp.uint32(0x80000000)
    return lax.bitcast_convert_type(bits, _f32)


def _row_blocks(lp):
    n_main = lp // Q_BLK
    tail = lp - n_main * Q_BLK
    assert tail <= ATT_BLK, (lp, tail)
    return n_main, tail


def _key_query_iota(bk, bq):
    key = lax.broadcasted_iota(jnp.int32, (bk, bq), 0)
    qry = lax.broadcasted_iota(jnp.int32, (bk, bq), 1)
    return key, qry


def _mla_kernel(q_ref, k_ref, kr_ref, v_ref, o_ref, s_ref, p_ref, al_ref, m_ref, l_ref, acc_ref,
                *, lp, hps):
    n_main, tail = _row_blocks(lp)

    def q_block(q0, bq, n_full, is_tail):
        qs = [q_ref[pl.ds(q0, bq), a * QK_CAT:(a + 1) * QK_CAT] for a in range(hps)]

        def scores(a, k0, bk):
            k_cat = jnp.concatenate([k_ref[pl.ds(k0, bk), a * NOPE:(a + 1) * NOPE],
                                     kr_ref[pl.ds(k0, bk), :]], axis=1)
            return _dot_nt(k_cat, qs[a])

        def probs(a, s, slot, bk, mask_off):
            if mask_off is not None:
                key, qry = _key_query_iota(bk, bq)
                s = jnp.where(key + mask_off <= qry, s, NEG_BIG)
            m_prev = m_ref[a, :, 0:bq]
            m_new = jnp.maximum(m_prev, jnp.max(s, axis=0, keepdims=True))
            alpha = jnp.exp2(m_prev - m_new)
            p = jnp.exp2(s - m_new)
            l_ref[a, :, 0:bq] = alpha * l_ref[a, :, 0:bq] + jnp.sum(p, axis=0, keepdims=True)
            m_ref[a, :, 0:bq] = m_new
            p_ref[slot, a, 0:bk, 0:bq] = p.astype(_bf16)
            al_ref[slot, a, :, 0:bq] = alpha

        def values(a, slot, k0, bk):
            pv = _dot_tn(v_ref[pl.ds(k0, bk), a * VDIM:(a + 1) * VDIM], p_ref[slot, a, 0:bk, 0:bq])
            acc_ref[a, :, 0:bq] = al_ref[slot, a, :, 0:bq] * acc_ref[a, :, 0:bq] + pv

        first = 0 if is_tail else 1
        last = n_full - 1 if is_tail else n_full + 1
        for a in range(hps):
            m_ref[a, :, 0:bq] = jnp.full((1, bq), NEG_BIG, _f32)
            l_ref[a, :, 0:bq] = jnp.zeros((1, bq), _f32)
            acc_ref[a, :, 0:bq] = jnp.zeros((VDIM, bq), _f32)
            p_ref[1 - first, a, :, 0:bq] = jnp.zeros((ATT_BLK, bq), _bf16)
            al_ref[1 - first, a, :, 0:bq] = jnp.ones((1, bq), _f32)
            s_ref[first, a, :, 0:bq] = scores(a, 0, ATT_BLK)

        def step(c, slot, mask_off=None, prefetch=True):
            nxt = pl.multiple_of(jnp.minimum(c + 1, last) * ATT_BLK, ATT_BLK)
            prv = pl.multiple_of(jnp.maximum(c - 1, 0) * ATT_BLK, ATT_BLK)
            for a in range(hps):
                if prefetch:
                    s_ref[1 - slot, a, :, 0:bq] = scores(a, nxt, ATT_BLK)
                probs(a, s_ref[slot, a, :, 0:bq], slot, ATT_BLK, mask_off)
                values(a, 1 - slot, prv, ATT_BLK)

        def pair(t, carry):
            step(2 * t, first)
            step(2 * t + 1, 1 - first)
            return carry

        lax.fori_loop(0, n_full // 2, pair, 0)
        prv = pl.multiple_of(jnp.maximum(n_full - 1, 0) * ATT_BLK, ATT_BLK)
        if is_tail:
            for a in range(hps):
                probs(a, scores(a, q0, bq), 0, bq, 0)
                values(a, 1, prv, ATT_BLK)
                values(a, 0, q0, bq)
        else:
            step(n_full, 1, mask_off=0)
            step(n_full + 1, 0, mask_off=ATT_BLK, prefetch=False)
            k_last = pl.multiple_of((n_full + 1) * ATT_BLK, ATT_BLK)
            for a in range(hps):
                values(a, 0, k_last, ATT_BLK)
        for a in range(hps):
            out_t = acc_ref[a, :, 0:bq] * (1.0 / l_ref[a, :, 0:bq])
            o_ref[pl.ds(q0, bq), a * VDIM:(a + 1) * VDIM] = out_t.T.astype(o_ref.dtype)

    def main_block(i, carry):
        q_block(pl.multiple_of(i * Q_BLK, Q_BLK), Q_BLK, 2 * i, False)
        return carry

    lax.fori_loop(0, n_main, main_block, 0)
    if tail:
        q_block(n_main * Q_BLK, tail, n_main * (Q_BLK // ATT_BLK), True)


def _sb_kernel(q_ref, k_ref, v_ref, o_ref, s_ref, w_ref, r_ref, acc_ref, *, lp, hps):
    n_main, tail = _row_blocks(lp)
    key_i, key_j = _key_query_iota(ATT_BLK, ATT_BLK)
    later = (key_j > key_i).astype(_bf16)

    def q_block(q0, bq, n_full, is_tail):
        qs = [q_ref[pl.ds(q0, bq), a * VDIM:(a + 1) * VDIM] for a in range(hps)]

        def scores(a, k0, bk):
            return _dot_nt(k_ref[pl.ds(k0, bk), a * VDIM:(a + 1) * VDIM], qs[a])

        def weights(a, z, slot, bk, mask_off):
            tail_sp = jnp.log(1.0 + jnp.exp2(_neg_abs(z))) * LOG2E
            log_beta = jnp.minimum(z, 0.0) - tail_sp
            log_rest = log_beta - z
            masked = mask_off is not None
            if masked:
                key, qry = _key_query_iota(bk, bq)
                strict = key + mask_off < qry
                log_rest = jnp.where(strict, log_rest, 0.0)
            within = jnp.dot(later[0:bk, 0:bk], log_rest.astype(_bf16), preferred_element_type=_f32)
            r = r_ref[a, :, 0:bq]
            w = jnp.exp2(log_beta + within + r)
            if masked:
                w = jnp.where(strict, w, 0.0)
            w_ref[slot, a, 0:bk, 0:bq] = w.astype(_bf16)
            r_ref[a, :, 0:bq] = r + within[0:1, :] + log_rest[0:1, :]

        def values(a, slot, k0, bk):
            acc_ref[a, :, 0:bq] = acc_ref[a, :, 0:bq] + _dot_tn(
                v_ref[pl.ds(k0, bk), a * VDIM:(a + 1) * VDIM], w_ref[slot, a, 0:bk, 0:bq])

        v_last = n_full - 1 if is_tail else n_full + 1

        def step(c, slot, mask_off=None):
            k_before = pl.multiple_of(jnp.maximum(c - 1, 0) * ATT_BLK, ATT_BLK)
            k_after = pl.multiple_of(jnp.minimum(c + 1, v_last) * ATT_BLK, ATT_BLK)
            for a in range(hps):
                z = s_ref[slot, a, :, 0:bq]
                s_ref[1 - slot, a, :, 0:bq] = scores(a, k_before, ATT_BLK)
                weights(a, z, slot, ATT_BLK, mask_off)
                values(a, 1 - slot, k_after, ATT_BLK)

        k_top = pl.multiple_of(jnp.maximum(n_full - 1, 0) * ATT_BLK, ATT_BLK)
        for a in range(hps):
            r_ref[a, :, 0:bq] = jnp.zeros((1, bq), _f32)
            acc_ref[a, :, 0:bq] = jnp.zeros((VDIM, bq), _f32)
            if is_tail:
                s_ref[1, a, :, 0:bq] = scores(a, k_top, ATT_BLK)
                weights(a, scores(a, q0, bq), 0, bq, 0)
                values(a, 0, q0, bq)
                w_ref[0, a, :, 0:bq] = jnp.zeros((ATT_BLK, bq), _bf16)
            else:
                k_hi = pl.multiple_of((n_full + 1) * ATT_BLK, ATT_BLK)
                s_ref[0, a, :, 0:bq] = scores(a, q0, ATT_BLK)
                weights(a, scores(a, k_hi, ATT_BLK), 1, ATT_BLK, ATT_BLK)
        if not is_tail:
            step(n_full, 0, mask_off=0)

        n_pairs = n_full // 2

        def pair(tt, carry):
            c = 2 * (n_pairs - 1 - tt) + 1
            step(c, 1)
            step(c - 1, 0)
            return carry

        lax.fori_loop(0, n_pairs, pair, 0)
        for a in range(hps):
            values(a, 0, 0, ATT_BLK)
            o_ref[pl.ds(q0, bq), a * VDIM:(a + 1) * VDIM] = acc_ref[a, :, 0:bq].T.astype(o_ref.dtype)

    def main_block(i, carry):
        q_block(pl.multiple_of(i * Q_BLK, Q_BLK), Q_BLK, 2 * i, False)
        return carry

    lax.fori_loop(0, n_main, main_block, 0)
    if tail:
        q_block(n_main * Q_BLK, tail, n_main * (Q_BLK // ATT_BLK), True)


def _att_scratch(hps, slot_stats, n_stat):
    return ([pltpu.VMEM((2, hps, ATT_BLK, Q_BLK), _f32),
             pltpu.VMEM((2, hps, ATT_BLK, Q_BLK), _bf16)]
            + [pltpu.VMEM((2, hps, 1, Q_BLK), _f32)] * slot_stats
            + [pltpu.VMEM((hps, 1, Q_BLK), _f32)] * n_stat
            + [pltpu.VMEM((hps, VDIM, Q_BLK), _f32)])


def _mla_attention(q_cat, k_nope, k_rope, v, heads):
    b, lp, _ = q_cat.shape
    hps = MLA_HEADS_PER_STEP if heads % MLA_HEADS_PER_STEP == 0 else 1
    q_spec = pl.BlockSpec((None, lp, hps * QK_CAT), lambda bi, h: (bi, 0, h))
    kr_spec = pl.BlockSpec((None, lp, LANE), lambda bi, h: (bi, 0, 0))
    v_spec = pl.BlockSpec((None, lp, hps * VDIM), lambda bi, h: (bi, 0, h))
    return pl.pallas_call(
        functools.partial(_mla_kernel, lp=lp, hps=hps),
        grid=(b, heads // hps),
        in_specs=[q_spec, v_spec, kr_spec, v_spec],
        out_specs=v_spec,
        out_shape=jax.ShapeDtypeStruct((b, lp, heads * VDIM), _bf16),
        scratch_shapes=_att_scratch(hps, 1, 2),
        compiler_params=_params("parallel", "parallel"),
        name="mla_attention",
    )(q_cat, k_nope, k_rope, v)


def _sb_attention(big, heads, q_col, k_col, v_col):
    b, lp, _ = big.shape
    hps = SB_HEADS_PER_STEP if heads % SB_HEADS_PER_STEP == 0 else 1
    w = hps * VDIM

    def spec(col0):
        return pl.BlockSpec((None, lp, w), lambda bi, h: (bi, 0, col0 // w + h))

    return pl.pallas_call(
        functools.partial(_sb_kernel, lp=lp, hps=hps),
        grid=(b, heads // hps),
        in_specs=[spec(q_col), spec(k_col), spec(v_col)],
        out_specs=pl.BlockSpec((None, lp, w), lambda bi, h: (bi, 0, h)),
        out_shape=jax.ShapeDtypeStruct((b, lp, heads * VDIM), _bf16),
        scratch_shapes=_att_scratch(hps, 0, 1),
        compiler_params=_params("parallel", "parallel"),
        name="sb_attention",
    )(big, big, big)


def _gate_kernel(ym_ref, ys_ref, zm_ref, zs_ref, gm_ref, gs_ref, o_ref, *, w):
    def one(y_ref, z_ref, g_ref):
        z = z_ref[...].astype(_f32)
        silu = z / (1.0 + jnp.exp(-z))
        return (_rms(y_ref[...].astype(_f32), g_ref[...]) * silu).astype(o_ref.dtype)

    o_ref[:, 0:w] = one(ym_ref, zm_ref, gm_ref)
    o_ref[:, w:2 * w] = one(ys_ref, zs_ref, gs_ref)


def _gate(y_mla, y_sb, big, g_mla, g_sb, zm_col, zs_col, tm):
    m, w = y_mla.shape
    yspec = pl.BlockSpec((tm, w), lambda i: (i, 0))
    gspec = pl.BlockSpec((1, w), lambda i: (0, 0))
    return pl.pallas_call(
        functools.partial(_gate_kernel, w=w),
        grid=(m // tm,),
        in_specs=[yspec, yspec,
                  pl.BlockSpec((tm, w), lambda i: (i, zm_col // w)),
                  pl.BlockSpec((tm, w), lambda i: (i, zs_col // w)),
                  gspec, gspec],
        out_specs=pl.BlockSpec((tm, 2 * w), lambda i: (i, 0)),
        out_shape=jax.ShapeDtypeStruct((m, 2 * w), _bf16),
        compiler_params=_params("parallel"),
        name="gate",
    )(y_mla, y_sb, big, big, g_mla.reshape(1, w), g_sb.reshape(1, w))


def _rope_tables(lp):
    inv_freq = ROPE_THETA ** (-jnp.arange(0, ROPE, 2, dtype=_f32) / ROPE)
    ang = jnp.arange(lp, dtype=jnp.int32).astype(_f32)[:, None] * inv_freq[None, :]
    cos, sin = jnp.cos(ang), jnp.sin(ang)
    zeros = jnp.zeros((lp, LANE - ROPE), _f32)
    return (jnp.concatenate([cos, cos, zeros], axis=1),
            jnp.concatenate([-sin, sin, zeros], axis=1))


@jax.jit
def _forward(x, meta_tokens, g_norm, w_in, g_q, g_kv, w_uq, w_ukv, g_out_mla, g_out_sb,
             w_o, g_final):
    b, seq, d = x.shape
    depth = w_in.shape[0]
    ql, kvl = g_q.shape[1], g_kv.shape[1]
    heads = w_uq.shape[2] // (NOPE + ROPE)
    w_grp = heads * VDIM
    assert w_ukv.shape[2] == heads * (NOPE + VDIM)
    assert w_in.shape[2] == ql + kvl + ROPE + 5 * w_grp
    assert ql % kvl == 0 and (ql + kvl) % LANE == 0 and kvl % LANE == 0

    l_real = N_META + seq
    lp = -(-l_real // LANE) * LANE
    mp = b * lp

    n_small = ql + kvl + ROPE
    w_small = jnp.pad(w_in[:, :, :n_small], ((0, 0), (0, 0), (0, LANE - ROPE))).astype(_bf16)
    w_big = w_in[:, :, n_small:].astype(_bf16)
    zm_col, qs_col, ks_col, vs_col, zs_col = (i * w_grp for i in range(5))
    colscale = jnp.ones((1, 5 * w_grp), _f32).at[:, qs_col:qs_col + w_grp].set(SB_SCALE2)
    wq = w_uq.reshape(depth, ql, heads, NOPE + ROPE)
    wq = jnp.pad(wq, ((0, 0), (0, 0), (0, 0), (0, QK_CAT - NOPE - ROPE)))
    wq = wq.reshape(depth, ql, heads * QK_CAT).astype(_bf16)
    wkv = w_ukv.reshape(depth, kvl, heads, NOPE + VDIM)
    wk = wkv[..., :NOPE].reshape(depth, kvl, heads * NOPE).astype(_bf16)
    wv = wkv[..., NOPE:].reshape(depth, kvl, heads * VDIM).astype(_bf16)
    wo = w_o.astype(_bf16)
    cos, sin = _rope_tables(lp)

    meta = jnp.broadcast_to(meta_tokens[None].astype(x.dtype), (b, N_META, d))
    h = jnp.concatenate([meta, x, jnp.zeros((b, lp - l_real, d), x.dtype)], axis=1)
    h = h.reshape(mp, d)

    tm_norm = _pick_tile(lp, 384)
    tm_mm = _pick_tile(lp, 528)
    tm_proj = _pick_tile(lp, 1056)
    tn_big = 1024 if (5 * w_grp) % 1024 == 0 else 512
    tn_out = 1024 if d % 1024 == 0 else 512

    for i in range(depth):
        u = _rmsnorm(h, g_norm[i], _bf16, tm_norm)
        small = _matmul_small(u, w_small, i, tm_mm)
        big = _matmul_colscale(u, w_big, i, colscale, _bf16, tm_mm, tn_big)
        q_cat = _q_proj(small, g_q[i], wq, i, cos, sin, lp, tm_proj, ql)
        k_nope, v_mla, k_rope = _kv_proj(small, g_kv[i], wk, wv, i, cos, sin, lp, tm_proj, ql, kvl)
        y_mla = _mla_attention(q_cat.reshape(b, lp, -1), k_nope.reshape(b, lp, -1),
                               k_rope.reshape(b, lp, -1), v_mla.reshape(b, lp, -1), heads)
        y_sb = _sb_attention(big.reshape(b, lp, -1), heads, qs_col, ks_col, vs_col)
        y = _gate(y_mla.reshape(mp, w_grp), y_sb.reshape(mp, w_grp), big,
                  g_out_mla[i], g_out_sb[i], zm_col, zs_col, tm_norm)
        h = _matmul_residual(y, wo, i, h, tm_mm, tn_out)

    return _final_norm(h.reshape(b, lp, d), g_final, seq, x.dtype, _pick_tile(seq, 512))


def kernel(x, meta_tokens, g_norm, w_in, g_q, g_kv, w_uq, w_ukv, g_out_mla, g_out_sb, w_o, g_final):
    return _forward(x, meta_tokens, g_norm, w_in, g_q, g_kv, w_uq, w_ukv, g_out_mla, g_out_sb,
                    w_o, g_final)
```

```python
import functools
import math

import jax
import jax.numpy as jnp
from jax import lax
from jax.experimental import pallas as pl
from jax.experimental.pallas import tpu as pltpu

N_META = 16
NOPE = 128
ROPE = 64
VDIM = 128
QK_CAT = 256
ROPE_THETA = 10000.0
EPS = 1e-6
LOG2E = math.log2(math.e)
MLA_SCALE2 = LOG2E / math.sqrt(NOPE + ROPE)
SB_SCALE2 = LOG2E / math.sqrt(VDIM)
LANE = 128
BF16_SUBLANE = 16
ATT_BLK = 256
Q_BLK = 2 * ATT_BLK
MLA_HEADS_PER_STEP = 4
SB_HEADS_PER_STEP = 4
VMEM_LIMIT = 56 * 1024 * 1024
NEG_BIG = -1e30
SB_DEAD_LOG2 = -160.0

_f32 = jnp.float32
_bf16 = jnp.bfloat16


def _pick_tile(n, target):
    best = None
    for t in range(BF16_SUBLANE, min(n, target) + 1, BF16_SUBLANE):
        if n % t == 0:
            best = t
    assert best is not None, (n, target)
    return best


def _params(*sem):
    return pltpu.CompilerParams(dimension_semantics=sem, vmem_limit_bytes=VMEM_LIMIT)


def _rms(xf, g):
    return xf * lax.rsqrt(jnp.mean(xf * xf, axis=-1, keepdims=True) + EPS) * g


def _rmsnorm_kernel(x_ref, g_ref, o_ref):
    o_ref[...] = _rms(x_ref[...].astype(_f32), g_ref[...]).astype(o_ref.dtype)


def _rmsnorm(x, g, out_dtype, tm):
    m, d = x.shape
    return pl.pallas_call(
        _rmsnorm_kernel,
        grid=(m // tm,),
        in_specs=[pl.BlockSpec((tm, d), lambda i: (i, 0)),
                  pl.BlockSpec((1, d), lambda i: (0, 0))],
        out_specs=pl.BlockSpec((tm, d), lambda i: (i, 0)),
        out_shape=jax.ShapeDtypeStruct((m, d), out_dtype),
        compiler_params=_params("parallel"),
        name="rmsnorm",
    )(x, g.reshape(1, d))


def _final_norm(h, g, seq, out_dtype, tm):
    b, lp, d = h.shape
    sub = 8
    assert N_META % sub == 0 and lp % sub == 0 and tm % sub == 0

    return pl.pallas_call(
        _rmsnorm_kernel,
        grid=(b, seq // tm),
        in_specs=[pl.BlockSpec((pl.Element(tm), pl.Element(d)),
                               lambda bi, i: (pl.multiple_of(bi * lp + N_META + i * tm, sub), 0)),
                  pl.BlockSpec((1, d), lambda bi, i: (0, 0))],
        out_specs=pl.BlockSpec((None, tm, d), lambda bi, i: (bi, i, 0)),
        out_shape=jax.ShapeDtypeStruct((b, seq, d), out_dtype),
        compiler_params=_params("parallel", "parallel"),
        name="final_norm",
    )(h.reshape(b * lp, d), g.reshape(1, d))


def _mm_scale_kernel(x_ref, w_ref, s_ref, o_ref):
    acc = jnp.dot(x_ref[...], w_ref[...], preferred_element_type=_f32)
    o_ref[...] = (acc * s_ref[...]).astype(o_ref.dtype)


def _matmul_colscale(x, w, layer, colscale, out_dtype, tm, tn):
    m, k = x.shape
    n = w.shape[2]
    return pl.pallas_call(
        _mm_scale_kernel,
        grid=(n // tn, m // tm),
        in_specs=[pl.BlockSpec((tm, k), lambda j, i: (i, 0)),
                  pl.BlockSpec((None, k, tn), lambda j, i: (layer, 0, j)),
                  pl.BlockSpec((1, tn), lambda j, i: (0, j))],
        out_specs=pl.BlockSpec((tm, tn), lambda j, i: (i, j)),
        out_shape=jax.ShapeDtypeStruct((m, n), out_dtype),
        compiler_params=_params("parallel", "parallel"),
        name="matmul_colscale",
    )(x, w, colscale)


def _mm_kernel(x_ref, w_ref, o_ref):
    o_ref[...] = jnp.dot(x_ref[...], w_ref[...], preferred_element_type=_f32).astype(o_ref.dtype)


def _matmul_small(x, w, layer, tm):
    m, k = x.shape
    n = w.shape[2]
    return pl.pallas_call(
        _mm_kernel,
        grid=(m // tm,),
        in_specs=[pl.BlockSpec((tm, k), lambda i: (i, 0)),
                  pl.BlockSpec((None, k, n), lambda i: (layer, 0, 0))],
        out_specs=pl.BlockSpec((tm, n), lambda i: (i, 0)),
        out_shape=jax.ShapeDtypeStruct((m, n), _f32),
        compiler_params=_params("parallel"),
        name="matmul_small",
    )(x, w)


def _mm_res_kernel(x_ref, w_ref, r_ref, o_ref):
    o_ref[...] = r_ref[...] + jnp.dot(x_ref[...], w_ref[...], preferred_element_type=_f32)


def _matmul_residual(x, w, layer, res, tm, tn):
    m, k = x.shape
    n = w.shape[2]
    return pl.pallas_call(
        _mm_res_kernel,
        grid=(n // tn, m // tm),
        in_specs=[pl.BlockSpec((tm, k), lambda j, i: (i, 0)),
                  pl.BlockSpec((None, k, tn), lambda j, i: (layer, 0, j)),
                  pl.BlockSpec((tm, tn), lambda j, i: (i, j))],
        out_specs=pl.BlockSpec((tm, tn), lambda j, i: (i, j)),
        out_shape=jax.ShapeDtypeStruct((m, n), _f32),
        compiler_params=_params("parallel", "parallel"),
        name="matmul_residual",
    )(x, w, res)


def _rope128(x, cos, sin_signed):
    lane = lax.broadcasted_iota(jnp.int32, x.shape, 1)
    partner = jnp.where(lane < ROPE // 2,
                        pltpu.roll(x, LANE - ROPE // 2, 1),
                        pltpu.roll(x, ROPE // 2, 1))
    return x * cos + partner * sin_signed


def _qproj_kernel(x_ref, g_ref, w_ref, cos_ref, sin_ref, o_ref, xn_ref, *, heads_per_step):
    @pl.when(pl.program_id(1) == 0)
    def _():
        xn_ref[...] = _rms(x_ref[...], g_ref[...]).astype(_bf16)

    acc = jnp.dot(xn_ref[...], w_ref[...], preferred_element_type=_f32)
    cos = cos_ref[...]
    sin = sin_ref[...]
    for hh in range(heads_per_step):
        c0 = hh * QK_CAT
        o_ref[:, c0:c0 + NOPE] = (acc[:, c0:c0 + NOPE] * MLA_SCALE2).astype(o_ref.dtype)
        roped = _rope128(acc[:, c0 + NOPE:c0 + QK_CAT], cos, sin)
        o_ref[:, c0 + NOPE:c0 + QK_CAT] = (roped * MLA_SCALE2).astype(o_ref.dtype)


def _q_proj(small, g_q, w_q, layer, cos, sin, lp, tm, ql):
    m = small.shape[0]
    n = w_q.shape[2]
    hps = 2 if (n // QK_CAT) % 2 == 0 else 1
    tn = hps * QK_CAT
    nrow = lp // tm
    return pl.pallas_call(
        functools.partial(_qproj_kernel, heads_per_step=hps),
        grid=(m // tm, n // tn),
        in_specs=[pl.BlockSpec((tm, ql), lambda i, j: (i, 0)),
                  pl.BlockSpec((1, ql), lambda i, j: (0, 0)),
                  pl.BlockSpec((None, ql, tn), lambda i, j: (layer, 0, j)),
                  pl.BlockSpec((tm, LANE), lambda i, j: (i % nrow, 0)),
                  pl.BlockSpec((tm, LANE), lambda i, j: (i % nrow, 0))],
        out_specs=pl.BlockSpec((tm, tn), lambda i, j: (i, j)),
        out_shape=jax.ShapeDtypeStruct((m, n), _bf16),
        scratch_shapes=[pltpu.VMEM((tm, ql), _bf16)],
        compiler_params=_params("parallel", "arbitrary"),
        name="q_proj",
    )(small, g_q.reshape(1, ql), w_q, cos, sin)


def _kvproj_kernel(x_ref, kr_ref, g_ref, wk_ref, wv_ref, cos_ref, sin_ref,
                   k_ref, v_ref, krr_ref, xn_ref):
    @pl.when(pl.program_id(1) == 0)
    def _():
        xn_ref[...] = _rms(x_ref[...], g_ref[...]).astype(_bf16)
        krr_ref[...] = _rope128(kr_ref[...], cos_ref[...], sin_ref[...]).astype(krr_ref.dtype)

    xn = xn_ref[...]
    k_ref[...] = jnp.dot(xn, wk_ref[...], preferred_element_type=_f32).astype(k_ref.dtype)
    v_ref[...] = jnp.dot(xn, wv_ref[...], preferred_element_type=_f32).astype(v_ref.dtype)


def _kv_proj(small, g_kv, w_k, w_v, layer, cos, sin, lp, tm, ql, kvl):
    m = small.shape[0]
    heads = w_k.shape[2] // NOPE
    hps = 4 if heads % 4 == 0 else 1
    nrow = lp // tm
    return pl.pallas_call(
        _kvproj_kernel,
        grid=(m // tm, heads // hps),
        in_specs=[pl.BlockSpec((tm, kvl), lambda i, j: (i, ql // kvl)),
                  pl.BlockSpec((tm, LANE), lambda i, j: (i, (ql + kvl) // LANE)),
                  pl.BlockSpec((1, kvl), lambda i, j: (0, 0)),
                  pl.BlockSpec((None, kvl, hps * NOPE), lambda i, j: (layer, 0, j)),
                  pl.BlockSpec((None, kvl, hps * VDIM), lambda i, j: (layer, 0, j)),
                  pl.BlockSpec((tm, LANE), lambda i, j: (i % nrow, 0)),
                  pl.BlockSpec((tm, LANE), lambda i, j: (i % nrow, 0))],
        out_specs=[pl.BlockSpec((tm, hps * NOPE), lambda i, j: (i, j)),
                   pl.BlockSpec((tm, hps * VDIM), lambda i, j: (i, j)),
                   pl.BlockSpec((tm, LANE), lambda i, j: (i, 0))],
        out_shape=[jax.ShapeDtypeStruct((m, heads * NOPE), _bf16),
                   jax.ShapeDtypeStruct((m, heads * VDIM), _bf16),
                   jax.ShapeDtypeStruct((m, LANE), _bf16)],
        scratch_shapes=[pltpu.VMEM((tm, kvl), _bf16)],
        compiler_params=_params("parallel", "arbitrary"),
        name="kv_proj",
    )(small, small, g_kv.reshape(1, kvl), w_k, w_v, cos, sin)


def _dot_nt(a, b):
    return lax.dot_general(a, b, (((1,), (1,)), ((), ())), preferred_element_type=_f32)


def _dot_tn(a, b):
    return lax.dot_general(a, b, (((0,), (0,)), ((), ())), preferred_element_type=_f32)


def _neg_abs(x):
    bits = lax.bitcast_convert_type(x, jnp.uint32) | jnp.uint32(0x80000000)
    return lax.bitcast_convert_type(bits, _f32)


def _row_blocks(lp):
    n_main = lp // Q_BLK
    tail = lp - n_main * Q_BLK
    assert tail <= ATT_BLK, (lp, tail)
    return n_main, tail


def _key_query_iota(bk, bq):
    key = lax.broadcasted_iota(jnp.int32, (bk, bq), 0)
    qry = lax.broadcasted_iota(jnp.int32, (bk, bq), 1)
    return key, qry


def _mla_kernel(q_ref, k_ref, kr_ref, v_ref, o_ref, s_ref, p_ref, al_ref, m_ref, l_ref, acc_ref,
                *, lp, hps):
    n_main, tail = _row_blocks(lp)

    def q_block(q0, bq, n_full, is_tail):
        qs = [q_ref[pl.ds(q0, bq), a * QK_CAT:(a + 1) * QK_CAT] for a in range(hps)]

        def scores(a, k0, bk):
            k_cat = jnp.concatenate([k_ref[pl.ds(k0, bk), a * NOPE:(a + 1) * NOPE],
                                     kr_ref[pl.ds(k0, bk), :]], axis=1)
            return _dot_nt(k_cat, qs[a])

        def probs(a, s, slot, bk, mask_off):
            if mask_off is not None:
                key, qry = _key_query_iota(bk, bq)
                s = jnp.where(key + mask_off <= qry, s, NEG_BIG)
            m_prev = m_ref[a, :, 0:bq]
            m_new = jnp.maximum(m_prev, jnp.max(s, axis=0, keepdims=True))
            alpha = jnp.exp2(m_prev - m_new)
            p = jnp.exp2(s - m_new)
            l_ref[a, :, 0:bq] = alpha * l_ref[a, :, 0:bq] + jnp.sum(p, axis=0, keepdims=True)
            m_ref[a, :, 0:bq] = m_new
            p_ref[slot, a, 0:bk, 0:bq] = p.astype(_bf16)
            al_ref[slot, a, :, 0:bq] = alpha

        def values(a, slot, k0, bk):
            pv = _dot_tn(v_ref[pl.ds(k0, bk), a * VDIM:(a + 1) * VDIM], p_ref[slot, a, 0:bk, 0:bq])
            acc_ref[a, :, 0:bq] = al_ref[slot, a, :, 0:bq] * acc_ref[a, :, 0:bq] + pv

        first = 0 if is_tail else 1
        last = n_full - 1 if is_tail else n_full + 1
        for a in range(hps):
            m_ref[a, :, 0:bq] = jnp.full((1, bq), NEG_BIG, _f32)
            l_ref[a, :, 0:bq] = jnp.zeros((1, bq), _f32)
            acc_ref[a, :, 0:bq] = jnp.zeros((VDIM, bq), _f32)
            p_ref[1 - first, a, :, 0:bq] = jnp.zeros((ATT_BLK, bq), _bf16)
            al_ref[1 - first, a, :, 0:bq] = jnp.ones((1, bq), _f32)
            s_ref[first, a, :, 0:bq] = scores(a, 0, ATT_BLK)

        def step(c, slot, mask_off=None, prefetch=True):
            nxt = pl.multiple_of(jnp.minimum(c + 1, last) * ATT_BLK, ATT_BLK)
            prv = pl.multiple_of(jnp.maximum(c - 1, 0) * ATT_BLK, ATT_BLK)
            for a in range(hps):
                if prefetch:
                    s_ref[1 - slot, a, :, 0:bq] = scores(a, nxt, ATT_BLK)
                probs(a, s_ref[slot, a, :, 0:bq], slot, ATT_BLK, mask_off)
                values(a, 1 - slot, prv, ATT_BLK)

        def pair(t, carry):
            step(2 * t, first)
            step(2 * t + 1, 1 - first)
            return carry

        lax.fori_loop(0, n_full // 2, pair, 0)
        prv = pl.multiple_of(jnp.maximum(n_full - 1, 0) * ATT_BLK, ATT_BLK)
        if is_tail:
            for a in range(hps):
                probs(a, scores(a, q0, bq), 0, bq, 0)
                values(a, 1, prv, ATT_BLK)
                values(a, 0, q0, bq)
        else:
            step(n_full, 1, mask_off=0)
            step(n_full + 1, 0, mask_off=ATT_BLK, prefetch=False)
            k_last = pl.multiple_of((n_full + 1) * ATT_BLK, ATT_BLK)
            for a in range(hps):
                values(a, 0, k_last, ATT_BLK)
        for a in range(hps):
            out_t = acc_ref[a, :, 0:bq] * (1.0 / l_ref[a, :, 0:bq])
            o_ref[pl.ds(q0, bq), a * VDIM:(a + 1) * VDIM] = out_t.T.astype(o_ref.dtype)

    def main_block(i, carry):
        q_block(pl.multiple_of(i * Q_BLK, Q_BLK), Q_BLK, 2 * i, False)
        return carry

    lax.fori_loop(0, n_main, main_block, 0)
    if tail:
        q_block(n_main * Q_BLK, tail, n_main * (Q_BLK // ATT_BLK), True)


def _sb_kernel(q_ref, k_ref, v_ref, o_ref, s_ref, w_ref, r_ref, acc_ref, *, lp, hps):
    n_main, tail = _row_blocks(lp)
    key_i, key_j = _key_query_iota(ATT_BLK, ATT_BLK)
    later = (key_j > key_i).astype(_bf16)

    def q_block(q0, bq, n_full, is_tail):
        qs = [q_ref[pl.ds(q0, bq), a * VDIM:(a + 1) * VDIM] for a in range(hps)]

        def scores(a, k0, bk):
            return _dot_nt(k_ref[pl.ds(k0, bk), a * VDIM:(a + 1) * VDIM], qs[a])

        def weights(a, z_of, slot, bk, mask_off):
            for lo in range(0, bq, ATT_BLK):
                hi = min(lo + ATT_BLK, bq)
                z = z_of(lo, hi)
                tail_sp = jnp.log(1.0 + jnp.exp2(_neg_abs(z))) * LOG2E
                log_beta = jnp.minimum(z, 0.0) - tail_sp
                log_rest = log_beta - z
                masked = mask_off is not None
                if masked:
                    key, qry = _key_query_iota(bk, hi - lo)
                    strict = key + (mask_off - lo) < qry
                    log_rest = jnp.where(strict, log_rest, 0.0)
                within = jnp.dot(later[0:bk, 0:bk], log_rest.astype(_bf16),
                                 preferred_element_type=_f32)
                r = r_ref[a, :, lo:hi]
                w = jnp.exp2(log_beta + within + r)
                if masked:
                    w = jnp.where(strict, w, 0.0)
                w_ref[slot, a, 0:bk, lo:hi] = w.astype(_bf16)
                r_ref[a, :, lo:hi] = r + within[0:1, :] + log_rest[0:1, :]

        def from_value(z):
            return lambda lo, hi: z[:, lo:hi]

        def values(a, slot, k0, bk):
            acc_ref[a, :, 0:bq] = acc_ref[a, :, 0:bq] + _dot_tn(
                v_ref[pl.ds(k0, bk), a * VDIM:(a + 1) * VDIM], w_ref[slot, a, 0:bk, 0:bq])

        v_last = n_full - 1 if is_tail else n_full + 1

        def step(c, slot, mask_off=None):
            k_before = pl.multiple_of(jnp.maximum(c - 1, 0) * ATT_BLK, ATT_BLK)
            k_after = pl.multiple_of(jnp.minimum(c + 1, v_last) * ATT_BLK, ATT_BLK)
            for a in range(hps):
                s_ref[1 - slot, a, :, 0:bq] = scores(a, k_before, ATT_BLK)
                weights(a, lambda lo, hi: s_ref[slot, a, :, lo:hi], slot, ATT_BLK, mask_off)
                values(a, 1 - slot, k_after, ATT_BLK)

        k_top = pl.multiple_of(jnp.maximum(n_full - 1, 0) * ATT_BLK, ATT_BLK)
        for a in range(hps):
            r_ref[a, :, 0:bq] = jnp.zeros((1, bq), _f32)
            acc_ref[a, :, 0:bq] = jnp.zeros((VDIM, bq), _f32)
            if is_tail:
                s_ref[1, a, :, 0:bq] = scores(a, k_top, ATT_BLK)
                weights(a, from_value(scores(a, q0, bq)), 0, bq, 0)
                values(a, 0, q0, bq)
                w_ref[0, a, :, 0:bq] = jnp.zeros((ATT_BLK, bq), _bf16)
            else:
                k_hi = pl.multiple_of((n_full + 1) * ATT_BLK, ATT_BLK)
                s_ref[0, a, :, 0:bq] = scores(a, q0, ATT_BLK)
                weights(a, from_value(scores(a, k_hi, ATT_BLK)), 1, ATT_BLK, ATT_BLK)
        if not is_tail:
            step(n_full, 0, mask_off=0)

        n_pairs = n_full // 2

        def live(tt):
            r_max = r_ref[0, :, 0:bq]
            for a in range(1, hps):
                r_max = jnp.maximum(r_max, r_ref[a, :, 0:bq])
            return jnp.logical_and(tt < n_pairs, jnp.max(r_max) > SB_DEAD_LOG2)

        def pair(tt):
            c = 2 * (n_pairs - 1 - tt) + 1
            step(c, 1)
            step(c - 1, 0)
            return tt + 1

        done = lax.while_loop(live, pair, jnp.int32(0))
        k_fin = pl.multiple_of(jnp.minimum(2 * (n_pairs - done), v_last) * ATT_BLK, ATT_BLK)
        for a in range(hps):
            values(a, 0, k_fin, ATT_BLK)
            o_ref[pl.ds(q0, bq), a * VDIM:(a + 1) * VDIM] = acc_ref[a, :, 0:bq].T.astype(o_ref.dtype)

    def main_block(i, carry):
        q_block(pl.multiple_of(i * Q_BLK, Q_BLK), Q_BLK, 2 * i, False)
        return carry

    lax.fori_loop(0, n_main, main_block, 0)
    if tail:
        q_block(n_main * Q_BLK, tail, n_main * (Q_BLK // ATT_BLK), True)


def _att_scratch(hps, slot_stats, n_stat):
    return ([pltpu.VMEM((2, hps, ATT_BLK, Q_BLK), _f32),
             pltpu.VMEM((2, hps, ATT_BLK, Q_BLK), _bf16)]
            + [pltpu.VMEM((2, hps, 1, Q_BLK), _f32)] * slot_stats
            + [pltpu.VMEM((hps, 1, Q_BLK), _f32)] * n_stat
            + [pltpu.VMEM((hps, VDIM, Q_BLK), _f32)])


def _mla_attention(q_cat, k_nope, k_rope, v, heads):
    b, lp, _ = q_cat.shape
    hps = MLA_HEADS_PER_STEP if heads % MLA_HEADS_PER_STEP == 0 else 1
    q_spec = pl.BlockSpec((None, lp, hps * QK_CAT), lambda bi, h: (bi, 0, h))
    kr_spec = pl.BlockSpec((None, lp, LANE), lambda bi, h: (bi, 0, 0))
    v_spec = pl.BlockSpec((None, lp, hps * VDIM), lambda bi, h: (bi, 0, h))
    return pl.pallas_call(
        functools.partial(_mla_kernel, lp=lp, hps=hps),
        grid=(b, heads // hps),
        in_specs=[q_spec, v_spec, kr_spec, v_spec],
        out_specs=v_spec,
        out_shape=jax.ShapeDtypeStruct((b, lp, heads * VDIM), _bf16),
        scratch_shapes=_att_scratch(hps, 1, 2),
        compiler_params=_params("parallel", "parallel"),
        name="mla_attention",
    )(q_cat, k_nope, k_rope, v)


def _sb_attention(big, heads, q_col, k_col, v_col):
    b, lp, _ = big.shape
    hps = SB_HEADS_PER_STEP if heads % SB_HEADS_PER_STEP == 0 else 1
    w = hps * VDIM

    def spec(col0):
        return pl.BlockSpec((None, lp, w), lambda bi, h: (bi, 0, col0 // w + h))

    return pl.pallas_call(
        functools.partial(_sb_kernel, lp=lp, hps=hps),
        grid=(b, heads // hps),
        in_specs=[spec(q_col), spec(k_col), spec(v_col)],
        out_specs=pl.BlockSpec((None, lp, w), lambda bi, h: (bi, 0, h)),
        out_shape=jax.ShapeDtypeStruct((b, lp, heads * VDIM), _bf16),
        scratch_shapes=_att_scratch(hps, 0, 1),
        compiler_params=_params("parallel", "parallel"),
        name="sb_attention",
    )(big, big, big)


def _gate_kernel(ym_ref, ys_ref, zm_ref, zs_ref, gm_ref, gs_ref, o_ref, *, w):
    def one(y_ref, z_ref, g_ref):
        z = z_ref[...].astype(_f32)
        silu = z / (1.0 + jnp.exp(-z))
        return (_rms(y_ref[...].astype(_f32), g_ref[...]) * silu).astype(o_ref.dtype)

    o_ref[:, 0:w] = one(ym_ref, zm_ref, gm_ref)
    o_ref[:, w:2 * w] = one(ys_ref, zs_ref, gs_ref)


def _gate(y_mla, y_sb, big, g_mla, g_sb, zm_col, zs_col, tm):
    m, w = y_mla.shape
    yspec = pl.BlockSpec((tm, w), lambda i: (i, 0))
    gspec = pl.BlockSpec((1, w), lambda i: (0, 0))
    return pl.pallas_call(
        functools.partial(_gate_kernel, w=w),
        grid=(m // tm,),
        in_specs=[yspec, yspec,
                  pl.BlockSpec((tm, w), lambda i: (i, zm_col // w)),
                  pl.BlockSpec((tm, w), lambda i: (i, zs_col // w)),
                  gspec, gspec],
        out_specs=pl.BlockSpec((tm, 2 * w), lambda i: (i, 0)),
        out_shape=jax.ShapeDtypeStruct((m, 2 * w), _bf16),
        compiler_params=_params("parallel"),
        name="gate",
    )(y_mla, y_sb, big, big, g_mla.reshape(1, w), g_sb.reshape(1, w))


def _rope_tables(lp):
    inv_freq = ROPE_THETA ** (-jnp.arange(0, ROPE, 2, dtype=_f32) / ROPE)
    ang = jnp.arange(lp, dtype=jnp.int32).astype(_f32)[:, None] * inv_freq[None, :]
    cos, sin = jnp.cos(ang), jnp.sin(ang)
    zeros = jnp.zeros((lp, LANE - ROPE), _f32)
    return (jnp.concatenate([cos, cos, zeros], axis=1),
            jnp.concatenate([-sin, sin, zeros], axis=1))


@jax.jit
def _forward(x, meta_tokens, g_norm, w_in, g_q, g_kv, w_uq, w_ukv, g_out_mla, g_out_sb,
             w_o, g_final):
    b, seq, d = x.shape
    depth = w_in.shape[0]
    ql, kvl = g_q.shape[1], g_kv.shape[1]
    heads = w_uq.shape[2] // (NOPE + ROPE)
    w_grp = heads * VDIM
    assert w_ukv.shape[2] == heads * (NOPE + VDIM)
    assert w_in.shape[2] == ql + kvl + ROPE + 5 * w_grp
    assert ql % kvl == 0 and (ql + kvl) % LANE == 0 and kvl % LANE == 0

    l_real = N_META + seq
    lp = -(-l_real // LANE) * LANE
    mp = b * lp

    n_small = ql + kvl + ROPE
    w_small = jnp.pad(w_in[:, :, :n_small], ((0, 0), (0, 0), (0, LANE - ROPE))).astype(_bf16)
    w_big = w_in[:, :, n_small:].astype(_bf16)
    zm_col, qs_col, ks_col, vs_col, zs_col = (i * w_grp for i in range(5))
    colscale = jnp.ones((1, 5 * w_grp), _f32).at[:, qs_col:qs_col + w_grp].set(SB_SCALE2)
    wq = w_uq.reshape(depth, ql, heads, NOPE + ROPE)
    wq = jnp.pad(wq, ((0, 0), (0, 0), (0, 0), (0, QK_CAT - NOPE - ROPE)))
    wq = wq.reshape(depth, ql, heads * QK_CAT).astype(_bf16)
    wkv = w_ukv.reshape(depth, kvl, heads, NOPE + VDIM)
    wk = wkv[..., :NOPE].reshape(depth, kvl, heads * NOPE).astype(_bf16)
    wv = wkv[..., NOPE:].reshape(depth, kvl, heads * VDIM).astype(_bf16)
    wo = w_o.astype(_bf16)
    cos, sin = _rope_tables(lp)

    meta = jnp.broadcast_to(meta_tokens[None].astype(x.dtype), (b, N_META, d))
    h = jnp.concatenate([meta, x, jnp.zeros((b, lp - l_real, d), x.dtype)], axis=1)
    h = h.reshape(mp, d)

    tm_norm = _pick_tile(lp, 384)
    tm_mm = _pick_tile(lp, 528)
    tm_proj = _pick_tile(lp, 1056)
    tn_big = 1024 if (5 * w_grp) % 1024 == 0 else 512
    tn_out = 1024 if d % 1024 == 0 else 512

    for i in range(depth):
        u = _rmsnorm(h, g_norm[i], _bf16, tm_norm)
        small = _matmul_small(u, w_small, i, tm_mm)
        big = _matmul_colscale(u, w_big, i, colscale, _bf16, tm_mm, tn_big)
        q_cat = _q_proj(small, g_q[i], wq, i, cos, sin, lp, tm_proj, ql)
        k_nope, v_mla, k_rope = _kv_proj(small, g_kv[i], wk, wv, i, cos, sin, lp, tm_proj, ql, kvl)
        y_mla = _mla_attention(q_cat.reshape(b, lp, -1), k_nope.reshape(b, lp, -1),
                               k_rope.reshape(b, lp, -1), v_mla.reshape(b, lp, -1), heads)
        y_sb = _sb_attention(big.reshape(b, lp, -1), heads, qs_col, ks_col, vs_col)
        y = _gate(y_mla.reshape(mp, w_grp), y_sb.reshape(mp, w_grp), big,
                  g_out_mla[i], g_out_sb[i], zm_col, zs_col, tm_norm)
        h = _matmul_residual(y, wo, i, h, tm_mm, tn_out)

    return _final_norm(h.reshape(b, lp, d), g_final, seq, x.dtype, _pick_tile(seq, 512))


def kernel(x, meta_tokens, g_norm, w_in, g_q, g_kv, w_uq, w_ukv, g_out_mla, g_out_sb, w_o, g_final):
    return _forward(x, meta_tokens, g_norm, w_in, g_q, g_kv, w_uq, w_ukv, g_out_mla, g_out_sb,
                    w_o, g_final)
```

```python
import functools
import math

import jax
import jax.numpy as jnp
from jax import lax
from jax.experimental import pallas as pl
from jax.experimental.pallas import tpu as pltpu

N_META = 16
NOPE = 128
ROPE = 64
VDIM = 128
QK_CAT = 256
ROPE_THETA = 10000.0
EPS = 1e-6
LOG2E = math.log2(math.e)
MLA_SCALE2 = LOG2E / math.sqrt(NOPE + ROPE)
SB_SCALE2 = LOG2E / math.sqrt(VDIM)
LANE = 128
BF16_SUBLANE = 16
ATT_BLK = 256
Q_BLK = 2 * ATT_BLK
MLA_HEADS_PER_STEP = 4
SB_HEADS_PER_STEP = 4
VMEM_LIMIT = 56 * 1024 * 1024
NEG_BIG = -1e30
SB_DEAD_LOG2 = -160.0

_f32 = jnp.float32
_bf16 = jnp.bfloat16


def _pick_tile(n, target):
    best = None
    for t in range(BF16_SUBLANE, min(n, target) + 1, BF16_SUBLANE):
        if n % t == 0:
            best = t
    assert best is not None, (n, target)
    return best


def _params(*sem):
    return pltpu.CompilerParams(dimension_semantics=sem, vmem_limit_bytes=VMEM_LIMIT)


def _rms(xf, g):
    return xf * lax.rsqrt(jnp.mean(xf * xf, axis=-1, keepdims=True) + EPS) * g


def _rmsnorm_kernel(x_ref, g_ref, o_ref):
    o_ref[...] = _rms(x_ref[...].astype(_f32), g_ref[...]).astype(o_ref.dtype)


def _rmsnorm(x, g, out_dtype, tm):
    m, d = x.shape
    return pl.pallas_call(
        _rmsnorm_kernel,
        grid=(m // tm,),
        in_specs=[pl.BlockSpec((tm, d), lambda i: (i, 0)),
                  pl.BlockSpec((1, d), lambda i: (0, 0))],
        out_specs=pl.BlockSpec((tm, d), lambda i: (i, 0)),
        out_shape=jax.ShapeDtypeStruct((m, d), out_dtype),
        compiler_params=_params("parallel"),
        name="rmsnorm",
    )(x, g.reshape(1, d))


def _final_norm(h, g, seq, out_dtype, tm):
    b, lp, d = h.shape
    sub = 8
    assert N_META % sub == 0 and lp % sub == 0 and tm % sub == 0

    return pl.pallas_call(
        _rmsnorm_kernel,
        grid=(b, seq // tm),
        in_specs=[pl.BlockSpec((pl.Element(tm), pl.Element(d)),
                               lambda bi, i: (pl.multiple_of(bi * lp + N_META + i * tm, sub), 0)),
                  pl.BlockSpec((1, d), lambda bi, i: (0, 0))],
        out_specs=pl.BlockSpec((None, tm, d), lambda bi, i: (bi, i, 0)),
        out_shape=jax.ShapeDtypeStruct((b, seq, d), out_dtype),
        compiler_params=_params("parallel", "parallel"),
        name="final_norm",
    )(h.reshape(b * lp, d), g.reshape(1, d))


def _mm_scale_kernel(x_ref, w_ref, s_ref, o_ref):
    acc = lax.dot_general(x_ref[...], w_ref[...], (((1,), (1,)), ((), ())),
                          preferred_element_type=_f32)
    o_ref[...] = (acc * s_ref[...]).astype(o_ref.dtype)


def _matmul_colscale(x, w_t, layer, colscale, out_dtype, tm, tn):
    m, k = x.shape
    n = w_t.shape[1]
    w = w_t
    return pl.pallas_call(
        _mm_scale_kernel,
        grid=(n // tn, m // tm),
        in_specs=[pl.BlockSpec((tm, k), lambda j, i: (i, 0)),
                  pl.BlockSpec((None, tn, k), lambda j, i: (layer, j, 0)),
                  pl.BlockSpec((1, tn), lambda j, i: (0, j))],
        out_specs=pl.BlockSpec((tm, tn), lambda j, i: (i, j)),
        out_shape=jax.ShapeDtypeStruct((m, n), out_dtype),
        compiler_params=_params("parallel", "parallel"),
        name="matmul_colscale",
    )(x, w, colscale)


def _mm_kernel(x_ref, w_ref, o_ref):
    o_ref[...] = jnp.dot(x_ref[...], w_ref[...], preferred_element_type=_f32).astype(o_ref.dtype)


def _matmul_small(x, w, layer, tm):
    m, k = x.shape
    n = w.shape[2]
    return pl.pallas_call(
        _mm_kernel,
        grid=(m // tm,),
        in_specs=[pl.BlockSpec((tm, k), lambda i: (i, 0)),
                  pl.BlockSpec((None, k, n), lambda i: (layer, 0, 0))],
        out_specs=pl.BlockSpec((tm, n), lambda i: (i, 0)),
        out_shape=jax.ShapeDtypeStruct((m, n), _f32),
        compiler_params=_params("parallel"),
        name="matmul_small",
    )(x, w)


def _mm_res_kernel(x_ref, w_ref, r_ref, o_ref):
    o_ref[...] = r_ref[...] + jnp.dot(x_ref[...], w_ref[...], preferred_element_type=_f32)


def _matmul_residual(x, w, layer, res, tm, tn):
    m, k = x.shape
    n = w.shape[2]
    return pl.pallas_call(
        _mm_res_kernel,
        grid=(n // tn, m // tm),
        in_specs=[pl.BlockSpec((tm, k), lambda j, i: (i, 0)),
                  pl.BlockSpec((None, k, tn), lambda j, i: (layer, 0, j)),
                  pl.BlockSpec((tm, tn), lambda j, i: (i, j))],
        out_specs=pl.BlockSpec((tm, tn), lambda j, i: (i, j)),
        out_shape=jax.ShapeDtypeStruct((m, n), _f32),
        compiler_params=_params("parallel", "parallel"),
        name="matmul_residual",
    )(x, w, res)


def _rope128(x, cos, sin_signed):
    lane = lax.broadcasted_iota(jnp.int32, x.shape, 1)
    partner = jnp.where(lane < ROPE // 2,
                        pltpu.roll(x, LANE - ROPE // 2, 1),
                        pltpu.roll(x, ROPE // 2, 1))
    return x * cos + partner * sin_signed


def _qproj_kernel(x_ref, g_ref, w_ref, cos_ref, sin_ref, o_ref, xn_ref, *, heads_per_step):
    @pl.when(pl.program_id(1) == 0)
    def _():
        xn_ref[...] = _rms(x_ref[...], g_ref[...]).astype(_bf16)

    acc = jnp.dot(xn_ref[...], w_ref[...], preferred_element_type=_f32)
    cos = cos_ref[...]
    sin = sin_ref[...]
    for hh in range(heads_per_step):
        c0 = hh * QK_CAT
        o_ref[:, c0:c0 + NOPE] = (acc[:, c0:c0 + NOPE] * MLA_SCALE2).astype(o_ref.dtype)
        roped = _rope128(acc[:, c0 + NOPE:c0 + QK_CAT], cos, sin)
        o_ref[:, c0 + NOPE:c0 + QK_CAT] = (roped * MLA_SCALE2).astype(o_ref.dtype)


def _q_proj(small, g_q, w_q, layer, cos, sin, lp, tm, ql):
    m = small.shape[0]
    n = w_q.shape[2]
    hps = 2 if (n // QK_CAT) % 2 == 0 else 1
    tn = hps * QK_CAT
    nrow = lp // tm
    return pl.pallas_call(
        functools.partial(_qproj_kernel, heads_per_step=hps),
        grid=(m // tm, n // tn),
        in_specs=[pl.BlockSpec((tm, ql), lambda i, j: (i, 0)),
                  pl.BlockSpec((1, ql), lambda i, j: (0, 0)),
                  pl.BlockSpec((None, ql, tn), lambda i, j: (layer, 0, j)),
                  pl.BlockSpec((tm, LANE), lambda i, j: (i % nrow, 0)),
                  pl.BlockSpec((tm, LANE), lambda i, j: (i % nrow, 0))],
        out_specs=pl.BlockSpec((tm, tn), lambda i, j: (i, j)),
        out_shape=jax.ShapeDtypeStruct((m, n), _bf16),
        scratch_shapes=[pltpu.VMEM((tm, ql), _bf16)],
        compiler_params=_params("parallel", "arbitrary"),
        name="q_proj",
    )(small, g_q.reshape(1, ql), w_q, cos, sin)


def _kvproj_kernel(x_ref, kr_ref, g_ref, wk_ref, wv_ref, cos_ref, sin_ref,
                   k_ref, v_ref, krr_ref, xn_ref):
    @pl.when(pl.program_id(1) == 0)
    def _():
        xn_ref[...] = _rms(x_ref[...], g_ref[...]).astype(_bf16)
        krr_ref[...] = _rope128(kr_ref[...], cos_ref[...], sin_ref[...]).astype(krr_ref.dtype)

    xn = xn_ref[...]
    k_ref[...] = jnp.dot(xn, wk_ref[...], preferred_element_type=_f32).astype(k_ref.dtype)
    v_ref[...] = jnp.dot(xn, wv_ref[...], preferred_element_type=_f32).astype(v_ref.dtype)


def _kv_proj(small, g_kv, w_k, w_v, layer, cos, sin, lp, tm, ql, kvl):
    m = small.shape[0]
    heads = w_k.shape[2] // NOPE
    hps = 4 if heads % 4 == 0 else 1
    nrow = lp // tm
    return pl.pallas_call(
        _kvproj_kernel,
        grid=(m // tm, heads // hps),
        in_specs=[pl.BlockSpec((tm, kvl), lambda i, j: (i, ql // kvl)),
                  pl.BlockSpec((tm, LANE), lambda i, j: (i, (ql + kvl) // LANE)),
                  pl.BlockSpec((1, kvl), lambda i, j: (0, 0)),
                  pl.BlockSpec((None, kvl, hps * NOPE), lambda i, j: (layer, 0, j)),
                  pl.BlockSpec((None, kvl, hps * VDIM), lambda i, j: (layer, 0, j)),
                  pl.BlockSpec((tm, LANE), lambda i, j: (i % nrow, 0)),
                  pl.BlockSpec((tm, LANE), lambda i, j: (i % nrow, 0))],
        out_specs=[pl.BlockSpec((tm, hps * NOPE), lambda i, j: (i, j)),
                   pl.BlockSpec((tm, hps * VDIM), lambda i, j: (i, j)),
                   pl.BlockSpec((tm, LANE), lambda i, j: (i, 0))],
        out_shape=[jax.ShapeDtypeStruct((m, heads * NOPE), _bf16),
                   jax.ShapeDtypeStruct((m, heads * VDIM), _bf16),
                   jax.ShapeDtypeStruct((m, LANE), _bf16)],
        scratch_shapes=[pltpu.VMEM((tm, kvl), _bf16)],
        compiler_params=_params("parallel", "arbitrary"),
        name="kv_proj",
    )(small, small, g_kv.reshape(1, kvl), w_k, w_v, cos, sin)


def _dot_nt(a, b):
    return lax.dot_general(a, b, (((1,), (1,)), ((), ())), preferred_element_type=_f32)


def _dot_tn(a, b):
    return lax.dot_general(a, b, (((0,), (0,)), ((), ())), preferred_element_type=_f32)


def _neg_abs(x):
    bits = lax.bitcast_convert_type(x, jnp.uint32) | jnp.uint32(0x80000000)
    return lax.bitcast_convert_type(bits, _f32)


def _row_blocks(lp):
    n_main = lp // Q_BLK
    tail = lp - n_main * Q_BLK
    assert tail <= ATT_BLK, (lp, tail)
    return n_main, tail


def _key_query_iota(bk, bq):
    key = lax.broadcasted_iota(jnp.int32, (bk, bq), 0)
    qry = lax.broadcasted_iota(jnp.int32, (bk, bq), 1)
    return key, qry


def _mla_kernel(q_ref, k_ref, kr_ref, v_ref, o_ref, s_ref, p_ref, al_ref, m_ref, l_ref, acc_ref,
                *, lp, hps):
    n_main, tail = _row_blocks(lp)

    def q_block(q0, bq, n_full, is_tail):
        qs = [q_ref[pl.ds(q0, bq), a * QK_CAT:(a + 1) * QK_CAT] for a in range(hps)]

        def scores(a, k0, bk):
            k_cat = jnp.concatenate([k_ref[pl.ds(k0, bk), a * NOPE:(a + 1) * NOPE],
                                     kr_ref[pl.ds(k0, bk), :]], axis=1)
            return _dot_nt(k_cat, qs[a])

        def probs(a, s, slot, bk, mask_off):
            if mask_off is not None:
                key, qry = _key_query_iota(bk, bq)
                s = jnp.where(key + mask_off <= qry, s, NEG_BIG)
            m_prev = m_ref[a, :, 0:bq]
            m_new = jnp.maximum(m_prev, jnp.max(s, axis=0, keepdims=True))
            alpha = jnp.exp2(m_prev - m_new)
            p = jnp.exp2(s - m_new)
            l_ref[a, :, 0:bq] = alpha * l_ref[a, :, 0:bq] + jnp.sum(p, axis=0, keepdims=True)
            m_ref[a, :, 0:bq] = m_new
            p_ref[slot, a, 0:bk, 0:bq] = p.astype(_bf16)
            al_ref[slot, a, :, 0:bq] = alpha

        def values(a, slot, k0, bk):
            pv = _dot_tn(v_ref[pl.ds(k0, bk), a * VDIM:(a + 1) * VDIM], p_ref[slot, a, 0:bk, 0:bq])
            acc_ref[a, :, 0:bq] = al_ref[slot, a, :, 0:bq] * acc_ref[a, :, 0:bq] + pv

        first = 0 if is_tail else 1
        last = n_full - 1 if is_tail else n_full + 1
        for a in range(hps):
            m_ref[a, :, 0:bq] = jnp.full((1, bq), NEG_BIG, _f32)
            l_ref[a, :, 0:bq] = jnp.zeros((1, bq), _f32)
            acc_ref[a, :, 0:bq] = jnp.zeros((VDIM, bq), _f32)
            p_ref[1 - first, a, :, 0:bq] = jnp.zeros((ATT_BLK, bq), _bf16)
            al_ref[1 - first, a, :, 0:bq] = jnp.ones((1, bq), _f32)
            s_ref[first, a, :, 0:bq] = scores(a, 0, ATT_BLK)

        def step(c, slot, mask_off=None, prefetch=True):
            nxt = pl.multiple_of(jnp.minimum(c + 1, last) * ATT_BLK, ATT_BLK)
            prv = pl.multiple_of(jnp.maximum(c - 1, 0) * ATT_BLK, ATT_BLK)
            for a in range(hps):
                if prefetch:
                    s_ref[1 - slot, a, :, 0:bq] = scores(a, nxt, ATT_BLK)
                probs(a, s_ref[slot, a, :, 0:bq], slot, ATT_BLK, mask_off)
                values(a, 1 - slot, prv, ATT_BLK)

        def pair(t, carry):
            step(2 * t, first)
            step(2 * t + 1, 1 - first)
            return carry

        lax.fori_loop(0, n_full // 2, pair, 0)
        prv = pl.multiple_of(jnp.maximum(n_full - 1, 0) * ATT_BLK, ATT_BLK)
        if is_tail:
            for a in range(hps):
                probs(a, scores(a, q0, bq), 0, bq, 0)
                values(a, 1, prv, ATT_BLK)
                values(a, 0, q0, bq)
        else:
            step(n_full, 1, mask_off=0)
            step(n_full + 1, 0, mask_off=ATT_BLK, prefetch=False)
            k_last = pl.multiple_of((n_full + 1) * ATT_BLK, ATT_BLK)
            for a in range(hps):
                values(a, 0, k_last, ATT_BLK)
        for a in range(hps):
            out_t = acc_ref[a, :, 0:bq] * (1.0 / l_ref[a, :, 0:bq])
            o_ref[pl.ds(q0, bq), a * VDIM:(a + 1) * VDIM] = out_t.T.astype(o_ref.dtype)

    def main_block(i, carry):
        q_block(pl.multiple_of(i * Q_BLK, Q_BLK), Q_BLK, 2 * i, False)
        return carry

    lax.fori_loop(0, n_main, main_block, 0)
    if tail:
        q_block(n_main * Q_BLK, tail, n_main * (Q_BLK // ATT_BLK), True)


def _sb_kernel(q_ref, k_ref, v_ref, o_ref, s_ref, w_ref, r_ref, acc_ref, *, lp, hps):
    n_main, tail = _row_blocks(lp)
    key_i, key_j = _key_query_iota(ATT_BLK, ATT_BLK)
    later = (key_j > key_i).astype(_bf16)

    def q_block(q0, bq, n_full, is_tail):
        qs = [q_ref[pl.ds(q0, bq), a * VDIM:(a + 1) * VDIM] for a in range(hps)]

        def scores(a, k0, bk):
            return _dot_nt(k_ref[pl.ds(k0, bk), a * VDIM:(a + 1) * VDIM], qs[a])

        def weights(a, z_of, slot, bk, mask_off):
            for lo in range(0, bq, ATT_BLK):
                hi = min(lo + ATT_BLK, bq)
                z = z_of(lo, hi)
                tail_sp = jnp.log(1.0 + jnp.exp2(_neg_abs(z))) * LOG2E
                log_beta = jnp.minimum(z, 0.0) - tail_sp
                log_rest = log_beta - z
                masked = mask_off is not None
                if masked:
                    key, qry = _key_query_iota(bk, hi - lo)
                    strict = key + (mask_off - lo) < qry
                    log_rest = jnp.where(strict, log_rest, 0.0)
                within = jnp.dot(later[0:bk, 0:bk], log_rest.astype(_bf16),
                                 preferred_element_type=_f32)
                r = r_ref[a, :, lo:hi]
                w = jnp.exp2(log_beta + within + r)
                if masked:
                    w = jnp.where(strict, w, 0.0)
                w_ref[slot, a, 0:bk, lo:hi] = w.astype(_bf16)
                r_ref[a, :, lo:hi] = r + within[0:1, :] + log_rest[0:1, :]

        def from_value(z):
            return lambda lo, hi: z[:, lo:hi]

        def values(a, slot, k0, bk):
            acc_ref[a, :, 0:bq] = acc_ref[a, :, 0:bq] + _dot_tn(
                v_ref[pl.ds(k0, bk), a * VDIM:(a + 1) * VDIM], w_ref[slot, a, 0:bk, 0:bq])

        v_last = n_full - 1 if is_tail else n_full + 1

        def step(c, slot, mask_off=None):
            k_before = pl.multiple_of(jnp.maximum(c - 1, 0) * ATT_BLK, ATT_BLK)
            k_after = pl.multiple_of(jnp.minimum(c + 1, v_last) * ATT_BLK, ATT_BLK)
            for a in range(hps):
                s_ref[1 - slot, a, :, 0:bq] = scores(a, k_before, ATT_BLK)
                weights(a, lambda lo, hi: s_ref[slot, a, :, lo:hi], slot, ATT_BLK, mask_off)
                values(a, 1 - slot, k_after, ATT_BLK)

        k_top = pl.multiple_of(jnp.maximum(n_full - 1, 0) * ATT_BLK, ATT_BLK)
        for a in range(hps):
            r_ref[a, :, 0:bq] = jnp.zeros((1, bq), _f32)
            acc_ref[a, :, 0:bq] = jnp.zeros((VDIM, bq), _f32)
            if is_tail:
                s_ref[1, a, :, 0:bq] = scores(a, k_top, ATT_BLK)
                weights(a, from_value(scores(a, q0, bq)), 0, bq, 0)
                values(a, 0, q0, bq)
                w_ref[0, a, :, 0:bq] = jnp.zeros((ATT_BLK, bq), _bf16)
            else:
                k_hi = pl.multiple_of((n_full + 1) * ATT_BLK, ATT_BLK)
                s_ref[0, a, :, 0:bq] = scores(a, q0, ATT_BLK)
                weights(a, from_value(scores(a, k_hi, ATT_BLK)), 1, ATT_BLK, ATT_BLK)
        if not is_tail:
            step(n_full, 0, mask_off=0)

        n_pairs = n_full // 2

        def alive():
            r_max = r_ref[0, :, 0:bq]
            for a in range(1, hps):
                r_max = jnp.maximum(r_max, r_ref[a, :, 0:bq])
            return (jnp.max(r_max) > SB_DEAD_LOG2).astype(jnp.int32)

        def pair(carry):
            tt, _, _ = carry
            c = 2 * (n_pairs - 1 - tt) + 1
            step(c, 1)
            go_on = alive()

            @pl.when(go_on == 1)
            def _():
                step(c - 1, 0)

            return tt + 1, c - go_on, jnp.where(go_on == 1, alive(), 0)

        _, c_fin, _ = lax.while_loop(
            lambda carry: jnp.logical_and(carry[0] < n_pairs, carry[2] == 1), pair,
            (jnp.int32(0), jnp.int32(0) + n_full, alive()))
        k_fin = pl.multiple_of(jnp.minimum(c_fin, v_last) * ATT_BLK, ATT_BLK)
        for slot in range(2):
            @pl.when(c_fin % 2 == slot)
            def _():
                for a in range(hps):
                    values(a, slot, k_fin, ATT_BLK)

        for a in range(hps):
            o_ref[pl.ds(q0, bq), a * VDIM:(a + 1) * VDIM] = acc_ref[a, :, 0:bq].T.astype(o_ref.dtype)

    def main_block(i, carry):
        q_block(pl.multiple_of(i * Q_BLK, Q_BLK), Q_BLK, 2 * i, False)
        return carry

    lax.fori_loop(0, n_main, main_block, 0)
    if tail:
        q_block(n_main * Q_BLK, tail, n_main * (Q_BLK // ATT_BLK), True)


def _att_scratch(hps, slot_stats, n_stat):
    return ([pltpu.VMEM((2, hps, ATT_BLK, Q_BLK), _f32),
             pltpu.VMEM((2, hps, ATT_BLK, Q_BLK), _bf16)]
            + [pltpu.VMEM((2, hps, 1, Q_BLK), _f32)] * slot_stats
            + [pltpu.VMEM((hps, 1, Q_BLK), _f32)] * n_stat
            + [pltpu.VMEM((hps, VDIM, Q_BLK), _f32)])


def _mla_attention(q_cat, k_nope, k_rope, v, heads):
    b, lp, _ = q_cat.shape
    hps = MLA_HEADS_PER_STEP if heads % MLA_HEADS_PER_STEP == 0 else 1
    q_spec = pl.BlockSpec((None, lp, hps * QK_CAT), lambda bi, h: (bi, 0, h))
    kr_spec = pl.BlockSpec((None, lp, LANE), lambda bi, h: (bi, 0, 0))
    v_spec = pl.BlockSpec((None, lp, hps * VDIM), lambda bi, h: (bi, 0, h))
    return pl.pallas_call(
        functools.partial(_mla_kernel, lp=lp, hps=hps),
        grid=(b, heads // hps),
        in_specs=[q_spec, v_spec, kr_spec, v_spec],
        out_specs=v_spec,
        out_shape=jax.ShapeDtypeStruct((b, lp, heads * VDIM), _bf16),
        scratch_shapes=_att_scratch(hps, 1, 2),
        compiler_params=_params("parallel", "parallel"),
        name="mla_attention",
    )(q_cat, k_nope, k_rope, v)


def _sb_attention(big, heads, q_col, k_col, v_col):
    b, lp, _ = big.shape
    hps = SB_HEADS_PER_STEP if heads % SB_HEADS_PER_STEP == 0 else 1
    w = hps * VDIM

    def spec(col0):
        return pl.BlockSpec((None, lp, w), lambda bi, h: (bi, 0, col0 // w + h))

    return pl.pallas_call(
        functools.partial(_sb_kernel, lp=lp, hps=hps),
        grid=(b, heads // hps),
        in_specs=[spec(q_col), spec(k_col), spec(v_col)],
        out_specs=pl.BlockSpec((None, lp, w), lambda bi, h: (bi, 0, h)),
        out_shape=jax.ShapeDtypeStruct((b, lp, heads * VDIM), _bf16),
        scratch_shapes=_att_scratch(hps, 0, 1),
        compiler_params=_params("parallel", "parallel"),
        name="sb_attention",
    )(big, big, big)


def _gate_kernel(ym_ref, ys_ref, zm_ref, zs_ref, gm_ref, gs_ref, o_ref, *, w):
    def one(y_ref, z_ref, g_ref):
        z = z_ref[...].astype(_f32)
        silu = z / (1.0 + jnp.exp(-z))
        return (_rms(y_ref[...].astype(_f32), g_ref[...]) * silu).astype(o_ref.dtype)

    o_ref[:, 0:w] = one(ym_ref, zm_ref, gm_ref)
    o_ref[:, w:2 * w] = one(ys_ref, zs_ref, gs_ref)


def _gate(y_mla, y_sb, big, g_mla, g_sb, zm_col, zs_col, tm):
    m, w = y_mla.shape
    yspec = pl.BlockSpec((tm, w), lambda i: (i, 0))
    gspec = pl.BlockSpec((1, w), lambda i: (0, 0))
    return pl.pallas_call(
        functools.partial(_gate_kernel, w=w),
        grid=(m // tm,),
        in_specs=[yspec, yspec,
                  pl.BlockSpec((tm, w), lambda i: (i, zm_col // w)),
                  pl.BlockSpec((tm, w), lambda i: (i, zs_col // w)),
                  gspec, gspec],
        out_specs=pl.BlockSpec((tm, 2 * w), lambda i: (i, 0)),
        out_shape=jax.ShapeDtypeStruct((m, 2 * w), _bf16),
        compiler_params=_params("parallel"),
        name="gate",
    )(y_mla, y_sb, big, big, g_mla.reshape(1, w), g_sb.reshape(1, w))


def _rope_tables(lp):
    inv_freq = ROPE_THETA ** (-jnp.arange(0, ROPE, 2, dtype=_f32) / ROPE)
    ang = jnp.arange(lp, dtype=jnp.int32).astype(_f32)[:, None] * inv_freq[None, :]
    cos, sin = jnp.cos(ang), jnp.sin(ang)
    zeros = jnp.zeros((lp, LANE - ROPE), _f32)
    return (jnp.concatenate([cos, cos, zeros], axis=1),
            jnp.concatenate([-sin, sin, zeros], axis=1))


@jax.jit
def _forward(x, meta_tokens, g_norm, w_in, g_q, g_kv, w_uq, w_ukv, g_out_mla, g_out_sb,
             w_o, g_final):
    b, seq, d = x.shape
    depth = w_in.shape[0]
    ql, kvl = g_q.shape[1], g_kv.shape[1]
    heads = w_uq.shape[2] // (NOPE + ROPE)
    w_grp = heads * VDIM
    assert w_ukv.shape[2] == heads * (NOPE + VDIM)
    assert w_in.shape[2] == ql + kvl + ROPE + 5 * w_grp
    assert ql % kvl == 0 and (ql + kvl) % LANE == 0 and kvl % LANE == 0

    l_real = N_META + seq
    lp = -(-l_real // LANE) * LANE
    mp = b * lp

    n_small = ql + kvl + ROPE
    w_small = jnp.pad(w_in[:, :, :n_small], ((0, 0), (0, 0), (0, LANE - ROPE))).astype(_bf16)
    w_big = jnp.swapaxes(w_in, 1, 2)[:, n_small:, :].astype(_bf16)
    zm_col, qs_col, ks_col, vs_col, zs_col = (i * w_grp for i in range(5))
    colscale = jnp.ones((1, 5 * w_grp), _f32).at[:, qs_col:qs_col + w_grp].set(SB_SCALE2)
    wq = w_uq.reshape(depth, ql, heads, NOPE + ROPE)
    wq = jnp.pad(wq, ((0, 0), (0, 0), (0, 0), (0, QK_CAT - NOPE - ROPE)))
    wq = wq.reshape(depth, ql, heads * QK_CAT).astype(_bf16)
    wkv = w_ukv.reshape(depth, kvl, heads, NOPE + VDIM)
    wk = wkv[..., :NOPE].reshape(depth, kvl, heads * NOPE).astype(_bf16)
    wv = wkv[..., NOPE:].reshape(depth, kvl, heads * VDIM).astype(_bf16)
    wo = w_o.astype(_bf16)
    cos, sin = _rope_tables(lp)

    meta = jnp.broadcast_to(meta_tokens[None].astype(x.dtype), (b, N_META, d))
    h = jnp.concatenate([meta, x, jnp.zeros((b, lp - l_real, d), x.dtype)], axis=1)
    h = h.reshape(mp, d)

    tm_norm = _pick_tile(lp, 384)
    tm_mm = _pick_tile(lp, 528)
    tm_proj = _pick_tile(lp, 1056)
    tn_big = 1024 if (5 * w_grp) % 1024 == 0 else 512
    tn_out = 1024 if d % 1024 == 0 else 512

    for i in range(depth):
        u = _rmsnorm(h, g_norm[i], _bf16, tm_norm)
        small = _matmul_small(u, w_small, i, tm_mm)
        big = _matmul_colscale(u, w_big, i, colscale, _bf16, tm_proj, tn_big)
        q_cat = _q_proj(small, g_q[i], wq, i, cos, sin, lp, tm_proj, ql)
        k_nope, v_mla, k_rope = _kv_proj(small, g_kv[i], wk, wv, i, cos, sin, lp, tm_proj, ql, kvl)
        y_mla = _mla_attention(q_cat.reshape(b, lp, -1), k_nope.reshape(b, lp, -1),
                               k_rope.reshape(b, lp, -1), v_mla.reshape(b, lp, -1), heads)
        y_sb = _sb_attention(big.reshape(b, lp, -1), heads, qs_col, ks_col, vs_col)
        y = _gate(y_mla.reshape(mp, w_grp), y_sb.reshape(mp, w_grp), big,
                  g_out_mla[i], g_out_sb[i], zm_col, zs_col, tm_norm)
        h = _matmul_residual(y, wo, i, h, tm_mm, tn_out)

    return _final_norm(h.reshape(b, lp, d), g_final, seq, x.dtype, _pick_tile(seq, 512))


def kernel(x, meta_tokens, g_norm, w_in, g_q, g_kv, w_uq, w_ukv, g_out_mla, g_out_sb, w_o, g_final):
    return _forward(x, meta_tokens, g_norm, w_in, g_q, g_kv, w_uq, w_ukv, g_out_mla, g_out_sb,
                    w_o, g_final)
```

```python
import functools
import math

import jax
import jax.numpy as jnp
from jax import lax
from jax.experimental import pallas as pl
from jax.experimental.pallas import tpu as pltpu

N_META = 16
NOPE = 128
ROPE = 64
VDIM = 128
QK_CAT = 256
ROPE_THETA = 10000.0
EPS = 1e-6
LOG2E = math.log2(math.e)
MLA_SCALE2 = LOG2E / math.sqrt(NOPE + ROPE)
SB_SCALE2 = LOG2E / math.sqrt(VDIM)
LANE = 128
BF16_SUBLANE = 16
ATT_BLK = 256
Q_BLK = 2 * ATT_BLK
MLA_HEADS_PER_STEP = 4
SB_HEADS_PER_STEP = 4
VMEM_LIMIT = 56 * 1024 * 1024
NEG_BIG = -1e30
SB_DEAD_LOG2 = -160.0

_f32 = jnp.float32
_bf16 = jnp.bfloat16


def _pick_tile(n, target):
    best = None
    for t in range(BF16_SUBLANE, min(n, target) + 1, BF16_SUBLANE):
        if n % t == 0:
            best = t
    assert best is not None, (n, target)
    return best


def _params(*sem):
    return pltpu.CompilerParams(dimension_semantics=sem, vmem_limit_bytes=VMEM_LIMIT)


def _rms(xf, g):
    return xf * lax.rsqrt(jnp.mean(xf * xf, axis=-1, keepdims=True) + EPS) * g


def _rmsnorm_kernel(x_ref, g_ref, o_ref):
    o_ref[...] = _rms(x_ref[...].astype(_f32), g_ref[...]).astype(o_ref.dtype)


def _rmsnorm(x, g, out_dtype, tm):
    m, d = x.shape
    return pl.pallas_call(
        _rmsnorm_kernel,
        grid=(m // tm,),
        in_specs=[pl.BlockSpec((tm, d), lambda i: (i, 0)),
                  pl.BlockSpec((1, d), lambda i: (0, 0))],
        out_specs=pl.BlockSpec((tm, d), lambda i: (i, 0)),
        out_shape=jax.ShapeDtypeStruct((m, d), out_dtype),
        compiler_params=_params("parallel"),
        name="rmsnorm",
    )(x, g.reshape(1, d))


def _final_norm(h, g, seq, out_dtype, tm):
    b, lp, d = h.shape
    sub = 8
    assert N_META % sub == 0 and lp % sub == 0 and tm % sub == 0

    return pl.pallas_call(
        _rmsnorm_kernel,
        grid=(b, seq // tm),
        in_specs=[pl.BlockSpec((pl.Element(tm), pl.Element(d)),
                               lambda bi, i: (pl.multiple_of(bi * lp + N_META + i * tm, sub), 0)),
                  pl.BlockSpec((1, d), lambda bi, i: (0, 0))],
        out_specs=pl.BlockSpec((None, tm, d), lambda bi, i: (bi, i, 0)),
        out_shape=jax.ShapeDtypeStruct((b, seq, d), out_dtype),
        compiler_params=_params("parallel", "parallel"),
        name="final_norm",
    )(h.reshape(b * lp, d), g.reshape(1, d))


def _mm_scale_kernel(x_ref, w_ref, s_ref, o_ref):
    acc = lax.dot_general(x_ref[...], w_ref[...], (((1,), (1,)), ((), ())),
                          preferred_element_type=_f32)
    o_ref[...] = (acc * s_ref[...]).astype(o_ref.dtype)


def _matmul_colscale(x, w_rows, row0, n, colscale, out_dtype, tm, tn):
    m, k = x.shape
    assert row0 % BF16_SUBLANE == 0 and tn % BF16_SUBLANE == 0
    return pl.pallas_call(
        _mm_scale_kernel,
        grid=(n // tn, m // tm),
        in_specs=[pl.BlockSpec((tm, k), lambda j, i: (i, 0)),
                  pl.BlockSpec((pl.Element(tn), pl.Element(k)),
                               lambda j, i: (pl.multiple_of(row0 + j * tn, BF16_SUBLANE), 0)),
                  pl.BlockSpec((1, tn), lambda j, i: (0, j))],
        out_specs=pl.BlockSpec((tm, tn), lambda j, i: (i, j)),
        out_shape=jax.ShapeDtypeStruct((m, n), out_dtype),
        compiler_params=_params("parallel", "parallel"),
        name="matmul_colscale",
    )(x, w_rows, colscale)


def _mm_kernel(x_ref, w_ref, o_ref):
    o_ref[...] = lax.dot_general(x_ref[...], w_ref[...], (((1,), (1,)), ((), ())),
                                 preferred_element_type=_f32).astype(o_ref.dtype)


def _matmul_small(x, w_rows, row0, n, tm):
    m, k = x.shape
    assert row0 % BF16_SUBLANE == 0
    return pl.pallas_call(
        _mm_kernel,
        grid=(m // tm,),
        in_specs=[pl.BlockSpec((tm, k), lambda i: (i, 0)),
                  pl.BlockSpec((pl.Element(n), pl.Element(k)), lambda i: (row0, 0))],
        out_specs=pl.BlockSpec((tm, n), lambda i: (i, 0)),
        out_shape=jax.ShapeDtypeStruct((m, n), _f32),
        compiler_params=_params("parallel"),
        name="matmul_small",
    )(x, w_rows)


def _mm_res_kernel(x_ref, w_ref, r_ref, o_ref):
    o_ref[...] = r_ref[...] + jnp.dot(x_ref[...], w_ref[...], preferred_element_type=_f32)


def _matmul_residual(x, w, layer, res, tm, tn):
    m, k = x.shape
    n = w.shape[2]
    return pl.pallas_call(
        _mm_res_kernel,
        grid=(n // tn, m // tm),
        in_specs=[pl.BlockSpec((tm, k), lambda j, i: (i, 0)),
                  pl.BlockSpec((None, k, tn), lambda j, i: (layer, 0, j)),
                  pl.BlockSpec((tm, tn), lambda j, i: (i, j))],
        out_specs=pl.BlockSpec((tm, tn), lambda j, i: (i, j)),
        out_shape=jax.ShapeDtypeStruct((m, n), _f32),
        compiler_params=_params("parallel", "parallel"),
        name="matmul_residual",
    )(x, w, res)


def _rope128(x, cos, sin_signed):
    lane = lax.broadcasted_iota(jnp.int32, x.shape, 1)
    partner = jnp.where(lane < ROPE // 2,
                        pltpu.roll(x, LANE - ROPE // 2, 1),
                        pltpu.roll(x, ROPE // 2, 1))
    return x * cos + partner * sin_signed


def _qproj_kernel(x_ref, g_ref, w_ref, cos_ref, sin_ref, o_ref, xn_ref, *, heads_per_step):
    @pl.when(pl.program_id(1) == 0)
    def _():
        xn_ref[...] = (_rms(x_ref[...], g_ref[...]) * MLA_SCALE2).astype(_bf16)

    acc = jnp.dot(xn_ref[...], w_ref[...], preferred_element_type=_f32)
    cos = cos_ref[...]
    sin = sin_ref[...]
    for hh in range(heads_per_step):
        c0 = hh * QK_CAT
        o_ref[:, c0:c0 + NOPE] = acc[:, c0:c0 + NOPE].astype(o_ref.dtype)
        roped = _rope128(acc[:, c0 + NOPE:c0 + QK_CAT], cos, sin)
        o_ref[:, c0 + NOPE:c0 + QK_CAT] = roped.astype(o_ref.dtype)


def _q_proj(small, g_q, w_q, layer, cos, sin, lp, tm, ql):
    m = small.shape[0]
    n = w_q.shape[2]
    hps = 4 if (n // QK_CAT) % 4 == 0 else 1
    tn = hps * QK_CAT
    nrow = lp // tm
    return pl.pallas_call(
        functools.partial(_qproj_kernel, heads_per_step=hps),
        grid=(m // tm, n // tn),
        in_specs=[pl.BlockSpec((tm, ql), lambda i, j: (i, 0)),
                  pl.BlockSpec((1, ql), lambda i, j: (0, 0)),
                  pl.BlockSpec((None, ql, tn), lambda i, j: (layer, 0, j)),
                  pl.BlockSpec((tm, LANE), lambda i, j: (i % nrow, 0)),
                  pl.BlockSpec((tm, LANE), lambda i, j: (i % nrow, 0))],
        out_specs=pl.BlockSpec((tm, tn), lambda i, j: (i, j)),
        out_shape=jax.ShapeDtypeStruct((m, n), _bf16),
        scratch_shapes=[pltpu.VMEM((tm, ql), _bf16)],
        compiler_params=_params("parallel", "arbitrary"),
        name="q_proj",
    )(small, g_q.reshape(1, ql), w_q, cos, sin)


def _kvproj_kernel(x_ref, kr_ref, g_ref, wk_ref, wv_ref, cos_ref, sin_ref,
                   k_ref, v_ref, krr_ref, xn_ref):
    @pl.when(pl.program_id(1) == 0)
    def _():
        xn_ref[...] = _rms(x_ref[...], g_ref[...]).astype(_bf16)
        krr_ref[...] = _rope128(kr_ref[...], cos_ref[...], sin_ref[...]).astype(krr_ref.dtype)

    xn = xn_ref[...]
    k_ref[...] = jnp.dot(xn, wk_ref[...], preferred_element_type=_f32).astype(k_ref.dtype)
    v_ref[...] = jnp.dot(xn, wv_ref[...], preferred_element_type=_f32).astype(v_ref.dtype)


def _kv_proj(small, g_kv, w_k, w_v, layer, cos, sin, lp, tm, ql, kvl):
    m = small.shape[0]
    heads = w_k.shape[2] // NOPE
    hps = 4 if heads % 4 == 0 else 1
    nrow = lp // tm
    return pl.pallas_call(
        _kvproj_kernel,
        grid=(m // tm, heads // hps),
        in_specs=[pl.BlockSpec((tm, kvl), lambda i, j: (i, ql // kvl)),
                  pl.BlockSpec((tm, LANE), lambda i, j: (i, (ql + kvl) // LANE)),
                  pl.BlockSpec((1, kvl), lambda i, j: (0, 0)),
                  pl.BlockSpec((None, kvl, hps * NOPE), lambda i, j: (layer, 0, j)),
                  pl.BlockSpec((None, kvl, hps * VDIM), lambda i, j: (layer, 0, j)),
                  pl.BlockSpec((tm, LANE), lambda i, j: (i % nrow, 0)),
                  pl.BlockSpec((tm, LANE), lambda i, j: (i % nrow, 0))],
        out_specs=[pl.BlockSpec((tm, hps * NOPE), lambda i, j: (i, j)),
                   pl.BlockSpec((tm, hps * VDIM), lambda i, j: (i, j)),
                   pl.BlockSpec((tm, LANE), lambda i, j: (i, 0))],
        out_shape=[jax.ShapeDtypeStruct((m, heads * NOPE), _bf16),
                   jax.ShapeDtypeStruct((m, heads * VDIM), _bf16),
                   jax.ShapeDtypeStruct((m, LANE), _bf16)],
        scratch_shapes=[pltpu.VMEM((tm, kvl), _bf16)],
        compiler_params=_params("parallel", "arbitrary"),
        name="kv_proj",
    )(small, small, g_kv.reshape(1, kvl), w_k, w_v, cos, sin)


def _dot_nt(a, b):
    return lax.dot_general(a, b, (((1,), (1,)), ((), ())), preferred_element_type=_f32)


def _dot_tn(a, b):
    return lax.dot_general(a, b, (((0,), (0,)), ((), ())), preferred_element_type=_f32)


def _neg_abs(x):
    bits = lax.bitcast_convert_type(x, jnp.uint32) | jnp.uint32(0x80000000)
    return lax.bitcast_convert_type(bits, _f32)


def _row_blocks(lp):
    n_main = lp // Q_BLK
    tail = lp - n_main * Q_BLK
    assert tail <= ATT_BLK, (lp, tail)
    return n_main, tail


def _key_query_iota(bk, bq):
    key = lax.broadcasted_iota(jnp.int32, (bk, bq), 0)
    qry = lax.broadcasted_iota(jnp.int32, (bk, bq), 1)
    return key, qry


def _mla_kernel(q_ref, k_ref, kr_ref, v_ref, o_ref, s_ref, p_ref, al_ref, m_ref, l_ref, acc_ref,
                *, lp, hps):
    n_main, tail = _row_blocks(lp)

    def q_block(q0, bq, n_full, is_tail):
        qs = [q_ref[pl.ds(q0, bq), a * QK_CAT:(a + 1) * QK_CAT] for a in range(hps)]

        def scores(a, k0, bk):
            k_cat = jnp.concatenate([k_ref[pl.ds(k0, bk), a * NOPE:(a + 1) * NOPE],
                                     kr_ref[pl.ds(k0, bk), :]], axis=1)
            return _dot_nt(k_cat, qs[a])

        def probs(a, s, slot, bk, mask_off):
            if mask_off is not None:
                key, qry = _key_query_iota(bk, bq)
                s = jnp.where(key + mask_off <= qry, s, NEG_BIG)
            m_prev = m_ref[a, :, 0:bq]
            m_new = jnp.maximum(m_prev, jnp.max(s, axis=0, keepdims=True))
            alpha = jnp.exp2(m_prev - m_new)
            p = jnp.exp2(s - m_new)
            l_ref[a, :, 0:bq] = alpha * l_ref[a, :, 0:bq] + jnp.sum(p, axis=0, keepdims=True)
            m_ref[a, :, 0:bq] = m_new
            p_ref[slot, a, 0:bk, 0:bq] = p.astype(_bf16)
            al_ref[slot, a, :, 0:bq] = alpha

        def values(a, slot, k0, bk):
            pv = _dot_tn(v_ref[pl.ds(k0, bk), a * VDIM:(a + 1) * VDIM], p_ref[slot, a, 0:bk, 0:bq])
            acc_ref[a, :, 0:bq] = al_ref[slot, a, :, 0:bq] * acc_ref[a, :, 0:bq] + pv

        first = 0 if is_tail else 1
        last = n_full - 1 if is_tail else n_full + 1
        for a in range(hps):
            m_ref[a, :, 0:bq] = jnp.full((1, bq), NEG_BIG, _f32)
            l_ref[a, :, 0:bq] = jnp.zeros((1, bq), _f32)
            acc_ref[a, :, 0:bq] = jnp.zeros((VDIM, bq), _f32)
            p_ref[1 - first, a, :, 0:bq] = jnp.zeros((ATT_BLK, bq), _bf16)
            al_ref[1 - first, a, :, 0:bq] = jnp.ones((1, bq), _f32)
            s_ref[first, a, :, 0:bq] = scores(a, 0, ATT_BLK)

        def step(c, slot, mask_off=None, prefetch=True):
            nxt = pl.multiple_of(jnp.minimum(c + 1, last) * ATT_BLK, ATT_BLK)
            prv = pl.multiple_of(jnp.maximum(c - 1, 0) * ATT_BLK, ATT_BLK)
            for a in range(hps):
                if prefetch:
                    s_ref[1 - slot, a, :, 0:bq] = scores(a, nxt, ATT_BLK)
                probs(a, s_ref[slot, a, :, 0:bq], slot, ATT_BLK, mask_off)
                values(a, 1 - slot, prv, ATT_BLK)

        def pair(t, carry):
            step(2 * t, first)
            step(2 * t + 1, 1 - first)
            return carry

        lax.fori_loop(0, n_full // 2, pair, 0)
        prv = pl.multiple_of(jnp.maximum(n_full - 1, 0) * ATT_BLK, ATT_BLK)
        if is_tail:
            for a in range(hps):
                probs(a, scores(a, q0, bq), 0, bq, 0)
                values(a, 1, prv, ATT_BLK)
                values(a, 0, q0, bq)
        else:
            step(n_full, 1, mask_off=0)
            step(n_full + 1, 0, mask_off=ATT_BLK, prefetch=False)
            k_last = pl.multiple_of((n_full + 1) * ATT_BLK, ATT_BLK)
            for a in range(hps):
                values(a, 0, k_last, ATT_BLK)
        for a in range(hps):
            out_t = acc_ref[a, :, 0:bq] * (1.0 / l_ref[a, :, 0:bq])
            o_ref[pl.ds(q0, bq), a * VDIM:(a + 1) * VDIM] = out_t.T.astype(o_ref.dtype)

    def main_block(i, carry):
        q_block(pl.multiple_of(i * Q_BLK, Q_BLK), Q_BLK, 2 * i, False)
        return carry

    lax.fori_loop(0, n_main, main_block, 0)
    if tail:
        q_block(n_main * Q_BLK, tail, n_main * (Q_BLK // ATT_BLK), True)


def _sb_kernel(q_ref, k_ref, v_ref, o_ref, s_ref, w_ref, r_ref, acc_ref, *, lp, hps):
    n_main, tail = _row_blocks(lp)
    key_i, key_j = _key_query_iota(ATT_BLK, ATT_BLK)
    later = (key_j > key_i).astype(_bf16)

    def q_block(q0, bq, n_full, is_tail):
        qs = [q_ref[pl.ds(q0, bq), a * VDIM:(a + 1) * VDIM] for a in range(hps)]

        def scores(a, k0, bk):
            return _dot_nt(k_ref[pl.ds(k0, bk), a * VDIM:(a + 1) * VDIM], qs[a])

        def weights(a, z_of, slot, bk, mask_off):
            for lo in range(0, bq, ATT_BLK):
                hi = min(lo + ATT_BLK, bq)
                if mask_off is not None and mask_off - lo >= hi - lo - 1:
                    w_ref[slot, a, 0:bk, lo:hi] = jnp.zeros((bk, hi - lo), _bf16)
                    continue
                z = z_of(lo, hi)
                tail_sp = jnp.log(1.0 + jnp.exp2(_neg_abs(z))) * LOG2E
                log_beta = jnp.minimum(z, 0.0) - tail_sp
                log_rest = log_beta - z
                masked = mask_off is not None
                if masked:
                    key, qry = _key_query_iota(bk, hi - lo)
                    strict = key + (mask_off - lo) < qry
                    log_rest = jnp.where(strict, log_rest, 0.0)
                within = jnp.dot(later[0:bk, 0:bk], log_rest.astype(_bf16),
                                 preferred_element_type=_f32)
                r = r_ref[a, :, lo:hi]
                w = jnp.exp2(log_beta + within + r)
                if masked:
                    w = jnp.where(strict, w, 0.0)
                w_ref[slot, a, 0:bk, lo:hi] = w.astype(_bf16)
                r_ref[a, :, lo:hi] = r + within[0:1, :] + log_rest[0:1, :]

        def from_value(z):
            return lambda lo, hi: z[:, lo:hi]

        def values(a, slot, k0, bk):
            acc_ref[a, :, 0:bq] = acc_ref[a, :, 0:bq] + _dot_tn(
                v_ref[pl.ds(k0, bk), a * VDIM:(a + 1) * VDIM], w_ref[slot, a, 0:bk, 0:bq])

        v_last = n_full - 1 if is_tail else n_full + 1

        def step(c, slot, mask_off=None):
            k_before = pl.multiple_of(jnp.maximum(c - 1, 0) * ATT_BLK, ATT_BLK)
            k_after = pl.multiple_of(jnp.minimum(c + 1, v_last) * ATT_BLK, ATT_BLK)
            for a in range(hps):
                s_ref[1 - slot, a, :, 0:bq] = scores(a, k_before, ATT_BLK)
                weights(a, lambda lo, hi: s_ref[slot, a, :, lo:hi], slot, ATT_BLK, mask_off)
                values(a, 1 - slot, k_after, ATT_BLK)

        k_top = pl.multiple_of(jnp.maximum(n_full - 1, 0) * ATT_BLK, ATT_BLK)
        for a in range(hps):
            r_ref[a, :, 0:bq] = jnp.zeros((1, bq), _f32)
            acc_ref[a, :, 0:bq] = jnp.zeros((VDIM, bq), _f32)
            if is_tail:
                s_ref[1, a, :, 0:bq] = scores(a, k_top, ATT_BLK)
                weights(a, from_value(scores(a, q0, bq)), 0, bq, 0)
                values(a, 0, q0, bq)
                w_ref[0, a, :, 0:bq] = jnp.zeros((ATT_BLK, bq), _bf16)
            else:
                k_hi = pl.multiple_of((n_full + 1) * ATT_BLK, ATT_BLK)
                s_ref[0, a, :, 0:bq] = scores(a, q0, ATT_BLK)
                weights(a, from_value(scores(a, k_hi, ATT_BLK)), 1, ATT_BLK, ATT_BLK)
        if not is_tail:
            step(n_full, 0, mask_off=0)

        n_pairs = n_full // 2

        def alive():
            r_max = r_ref[0, :, 0:bq]
            for a in range(1, hps):
                r_max = jnp.maximum(r_max, r_ref[a, :, 0:bq])
            return (jnp.max(r_max) > SB_DEAD_LOG2).astype(jnp.int32)

        def pair(carry):
            tt, _, _ = carry
            c = 2 * (n_pairs - 1 - tt) + 1
            step(c, 1)
            go_on = alive()

            @pl.when(go_on == 1)
            def _():
                step(c - 1, 0)

            return tt + 1, c - go_on, jnp.where(go_on == 1, alive(), 0)

        _, c_fin, _ = lax.while_loop(
            lambda carry: jnp.logical_and(carry[0] < n_pairs, carry[2] == 1), pair,
            (jnp.int32(0), jnp.int32(0) + n_full, jnp.int32(1)))
        k_fin = pl.multiple_of(jnp.minimum(c_fin, v_last) * ATT_BLK, ATT_BLK)
        for slot in range(2):
            @pl.when(c_fin % 2 == slot)
            def _():
                for a in range(hps):
                    values(a, slot, k_fin, ATT_BLK)

        for a in range(hps):
            o_ref[pl.ds(q0, bq), a * VDIM:(a + 1) * VDIM] = acc_ref[a, :, 0:bq].T.astype(o_ref.dtype)

    def main_block(i, carry):
        q_block(pl.multiple_of(i * Q_BLK, Q_BLK), Q_BLK, 2 * i, False)
        return carry

    lax.fori_loop(0, n_main, main_block, 0)
    if tail:
        q_block(n_main * Q_BLK, tail, n_main * (Q_BLK // ATT_BLK), True)


def _att_scratch(hps, slot_stats, n_stat):
    return ([pltpu.VMEM((2, hps, ATT_BLK, Q_BLK), _f32),
             pltpu.VMEM((2, hps, ATT_BLK, Q_BLK), _bf16)]
            + [pltpu.VMEM((2, hps, 1, Q_BLK), _f32)] * slot_stats
            + [pltpu.VMEM((hps, 1, Q_BLK), _f32)] * n_stat
            + [pltpu.VMEM((hps, VDIM, Q_BLK), _f32)])


def _mla_attention(q_cat, k_nope, k_rope, v, heads):
    b, lp, _ = q_cat.shape
    hps = MLA_HEADS_PER_STEP if heads % MLA_HEADS_PER_STEP == 0 else 1
    q_spec = pl.BlockSpec((None, lp, hps * QK_CAT), lambda bi, h: (bi, 0, h))
    kr_spec = pl.BlockSpec((None, lp, LANE), lambda bi, h: (bi, 0, 0))
    v_spec = pl.BlockSpec((None, lp, hps * VDIM), lambda bi, h: (bi, 0, h))
    return pl.pallas_call(
        functools.partial(_mla_kernel, lp=lp, hps=hps),
        grid=(b, heads // hps),
        in_specs=[q_spec, v_spec, kr_spec, v_spec],
        out_specs=v_spec,
        out_shape=jax.ShapeDtypeStruct((b, lp, heads * VDIM), _bf16),
        scratch_shapes=_att_scratch(hps, 1, 2),
        compiler_params=_params("parallel", "parallel"),
        name="mla_attention",
    )(q_cat, k_nope, k_rope, v)


def _sb_attention(big, heads, q_col, k_col, v_col):
    b, lp, _ = big.shape
    hps = SB_HEADS_PER_STEP if heads % SB_HEADS_PER_STEP == 0 else 1
    w = hps * VDIM

    def spec(col0):
        return pl.BlockSpec((None, lp, w), lambda bi, h: (bi, 0, col0 // w + h))

    return pl.pallas_call(
        functools.partial(_sb_kernel, lp=lp, hps=hps),
        grid=(b, heads // hps),
        in_specs=[spec(q_col), spec(k_col), spec(v_col)],
        out_specs=pl.BlockSpec((None, lp, w), lambda bi, h: (bi, 0, h)),
        out_shape=jax.ShapeDtypeStruct((b, lp, heads * VDIM), _bf16),
        scratch_shapes=_att_scratch(hps, 0, 1),
        compiler_params=_params("parallel", "parallel"),
        name="sb_attention",
    )(big, big, big)


def _gate_kernel(ym_ref, ys_ref, zm_ref, zs_ref, gm_ref, gs_ref, o_ref, *, w):
    def one(y_ref, z_ref, g_ref):
        z = z_ref[...].astype(_f32)
        silu = z / (1.0 + jnp.exp(-z))
        return (_rms(y_ref[...].astype(_f32), g_ref[...]) * silu).astype(o_ref.dtype)

    o_ref[:, 0:w] = one(ym_ref, zm_ref, gm_ref)
    o_ref[:, w:2 * w] = one(ys_ref, zs_ref, gs_ref)


def _gate(y_mla, y_sb, big, g_mla, g_sb, zm_col, zs_col, tm):
    m, w = y_mla.shape
    yspec = pl.BlockSpec((tm, w), lambda i: (i, 0))
    gspec = pl.BlockSpec((1, w), lambda i: (0, 0))
    return pl.pallas_call(
        functools.partial(_gate_kernel, w=w),
        grid=(m // tm,),
        in_specs=[yspec, yspec,
                  pl.BlockSpec((tm, w), lambda i: (i, zm_col // w)),
                  pl.BlockSpec((tm, w), lambda i: (i, zs_col // w)),
                  gspec, gspec],
        out_specs=pl.BlockSpec((tm, 2 * w), lambda i: (i, 0)),
        out_shape=jax.ShapeDtypeStruct((m, 2 * w), _bf16),
        compiler_params=_params("parallel"),
        name="gate",
    )(y_mla, y_sb, big, big, g_mla.reshape(1, w), g_sb.reshape(1, w))


def _rope_tables(lp):
    inv_freq = ROPE_THETA ** (-jnp.arange(0, ROPE, 2, dtype=_f32) / ROPE)
    ang = jnp.arange(lp, dtype=jnp.int32).astype(_f32)[:, None] * inv_freq[None, :]
    cos, sin = jnp.cos(ang), jnp.sin(ang)
    zeros = jnp.zeros((lp, LANE - ROPE), _f32)
    return (jnp.concatenate([cos, cos, zeros], axis=1),
            jnp.concatenate([-sin, sin, zeros], axis=1))


@jax.jit
def _forward(x, meta_tokens, g_norm, w_in, g_q, g_kv, w_uq, w_ukv, g_out_mla, g_out_sb,
             w_o, g_final):
    b, seq, d = x.shape
    depth = w_in.shape[0]
    ql, kvl = g_q.shape[1], g_kv.shape[1]
    heads = w_uq.shape[2] // (NOPE + ROPE)
    w_grp = heads * VDIM
    assert w_ukv.shape[2] == heads * (NOPE + VDIM)
    assert w_in.shape[2] == ql + kvl + ROPE + 5 * w_grp
    assert ql % kvl == 0 and (ql + kvl) % LANE == 0 and kvl % LANE == 0

    l_real = N_META + seq
    lp = -(-l_real // LANE) * LANE
    mp = b * lp

    n_small = ql + kvl + ROPE
    n_in = w_in.shape[2]
    w_rows = jnp.swapaxes(w_in, 1, 2).astype(_bf16).reshape(depth * n_in, d)
    n_small_win = ql + kvl + LANE
    zm_col, qs_col, ks_col, vs_col, zs_col = (i * w_grp for i in range(5))
    colscale = jnp.ones((1, 5 * w_grp), _f32).at[:, qs_col:qs_col + w_grp].set(SB_SCALE2)
    wq = w_uq.reshape(depth, ql, heads, NOPE + ROPE)
    wq = jnp.pad(wq, ((0, 0), (0, 0), (0, 0), (0, QK_CAT - NOPE - ROPE)))
    wq = wq.reshape(depth, ql, heads * QK_CAT).astype(_bf16)
    wkv = w_ukv.reshape(depth, kvl, heads, NOPE + VDIM)
    wk = wkv[..., :NOPE].reshape(depth, kvl, heads * NOPE).astype(_bf16)
    wv = wkv[..., NOPE:].reshape(depth, kvl, heads * VDIM).astype(_bf16)
    wo = w_o.astype(_bf16)
    cos, sin = _rope_tables(lp)

    meta = jnp.broadcast_to(meta_tokens[None].astype(x.dtype), (b, N_META, d))
    h = jnp.concatenate([meta, x, jnp.zeros((b, lp - l_real, d), x.dtype)], axis=1)
    h = h.reshape(mp, d)

    tm_norm = _pick_tile(lp, 384)
    tm_mm = _pick_tile(lp, 528)
    tm_proj = _pick_tile(lp, 1056)
    tn_big = 1024 if (5 * w_grp) % 1024 == 0 else 512
    tn_out = 1024 if d % 1024 == 0 else 512

    for i in range(depth):
        u = _rmsnorm(h, g_norm[i], _bf16, tm_norm)
        small = _matmul_small(u, w_rows, i * n_in, n_small_win, tm_mm)
        big = _matmul_colscale(u, w_rows, i * n_in + n_small, 5 * w_grp, colscale, _bf16,
                               tm_proj, tn_big)
        q_cat = _q_proj(small, g_q[i], wq, i, cos, sin, lp, tm_proj, ql)
        k_nope, v_mla, k_rope = _kv_proj(small, g_kv[i], wk, wv, i, cos, sin, lp, tm_proj, ql, kvl)
        y_mla = _mla_attention(q_cat.reshape(b, lp, -1), k_nope.reshape(b, lp, -1),
                               k_rope.reshape(b, lp, -1), v_mla.reshape(b, lp, -1), heads)
        y_sb = _sb_attention(big.reshape(b, lp, -1), heads, qs_col, ks_col, vs_col)
        y = _gate(y_mla.reshape(mp, w_grp), y_sb.reshape(mp, w_grp), big,
                  g_out_mla[i], g_out_sb[i], zm_col, zs_col, tm_norm)
        h = _matmul_residual(y, wo, i, h, tm_mm, tn_out)

    return _final_norm(h.reshape(b, lp, d), g_final, seq, x.dtype, _pick_tile(seq, 512))


def kernel(x, meta_tokens, g_norm, w_in, g_q, g_kv, w_uq, w_ukv, g_out_mla, g_out_sb, w_o, g_final):
    return _forward(x, meta_tokens, g_norm, w_in, g_q, g_kv, w_uq, w_ukv, g_out_mla, g_out_sb,
                    w_o, g_final)
```

```python
import functools
import math

import jax
import jax.numpy as jnp
from jax import lax
from jax.experimental import pallas as pl
from jax.experimental.pallas import tpu as pltpu

N_META = 16
NOPE = 128
ROPE = 64
VDIM = 128
QK_CAT = 256
ROPE_THETA = 10000.0
EPS = 1e-6
LOG2E = math.log2(math.e)
MLA_SCALE2 = LOG2E / math.sqrt(NOPE + ROPE)
SB_SCALE2 = LOG2E / math.sqrt(VDIM)
LANE = 128
BF16_SUBLANE = 16
ATT_BLK = 256
Q_BLK = 2 * ATT_BLK
MLA_HEADS_PER_STEP = 4
SB_HEADS_PER_STEP = 4
VMEM_LIMIT = 56 * 1024 * 1024
NEG_BIG = -1e30
SB_DEAD_LOG2 = -160.0

_f32 = jnp.float32
_bf16 = jnp.bfloat16


def _pick_tile(n, target):
    best = None
    for t in range(BF16_SUBLANE, min(n, target) + 1, BF16_SUBLANE):
        if n % t == 0:
            best = t
    assert best is not None, (n, target)
    return best


def _params(*sem):
    return pltpu.CompilerParams(dimension_semantics=sem, vmem_limit_bytes=VMEM_LIMIT)


def _rms(xf, g):
    return xf * lax.rsqrt(jnp.mean(xf * xf, axis=-1, keepdims=True) + EPS) * g


def _rmsnorm_kernel(x_ref, g_ref, o_ref):
    o_ref[...] = _rms(x_ref[...].astype(_f32), g_ref[...]).astype(o_ref.dtype)


def _rmsnorm(x, g, out_dtype, tm):
    m, d = x.shape
    return pl.pallas_call(
        _rmsnorm_kernel,
        grid=(m // tm,),
        in_specs=[pl.BlockSpec((tm, d), lambda i: (i, 0)),
                  pl.BlockSpec((1, d), lambda i: (0, 0))],
        out_specs=pl.BlockSpec((tm, d), lambda i: (i, 0)),
        out_shape=jax.ShapeDtypeStruct((m, d), out_dtype),
        compiler_params=_params("parallel"),
        name="rmsnorm",
    )(x, g.reshape(1, d))


def _final_norm(h, g, seq, out_dtype, tm):
    b, lp, d = h.shape
    sub = 8
    assert N_META % sub == 0 and lp % sub == 0 and tm % sub == 0

    return pl.pallas_call(
        _rmsnorm_kernel,
        grid=(b, seq // tm),
        in_specs=[pl.BlockSpec((pl.Element(tm), pl.Element(d)),
                               lambda bi, i: (pl.multiple_of(bi * lp + N_META + i * tm, sub), 0)),
                  pl.BlockSpec((1, d), lambda bi, i: (0, 0))],
        out_specs=pl.BlockSpec((None, tm, d), lambda bi, i: (bi, i, 0)),
        out_shape=jax.ShapeDtypeStruct((b, seq, d), out_dtype),
        compiler_params=_params("parallel", "parallel"),
        name="final_norm",
    )(h.reshape(b * lp, d), g.reshape(1, d))


def _mm_scale_kernel(x_ref, w_ref, s_ref, o_ref):
    acc = lax.dot_general(x_ref[...], w_ref[...], (((1,), (1,)), ((), ())),
                          preferred_element_type=_f32)
    o_ref[...] = (acc * s_ref[...]).astype(o_ref.dtype)


def _matmul_colscale(x, w_rows, row0, n, colscale, out_dtype, tm, tn):
    m, k = x.shape
    assert row0 % BF16_SUBLANE == 0 and tn % BF16_SUBLANE == 0
    return pl.pallas_call(
        _mm_scale_kernel,
        grid=(n // tn, m // tm),
        in_specs=[pl.BlockSpec((tm, k), lambda j, i: (i, 0)),
                  pl.BlockSpec((pl.Element(tn), pl.Element(k)),
                               lambda j, i: (pl.multiple_of(row0 + j * tn, BF16_SUBLANE), 0)),
                  pl.BlockSpec((1, tn), lambda j, i: (0, j))],
        out_specs=pl.BlockSpec((tm, tn), lambda j, i: (i, j)),
        out_shape=jax.ShapeDtypeStruct((m, n), out_dtype),
        compiler_params=_params("parallel", "parallel"),
        name="matmul_colscale",
    )(x, w_rows, colscale)


def _mm_kernel(x_ref, w_ref, o_ref):
    o_ref[...] = lax.dot_general(x_ref[...], w_ref[...], (((1,), (1,)), ((), ())),
                                 preferred_element_type=_f32).astype(o_ref.dtype)


def _matmul_small(x, w_rows, row0, n, tm):
    m, k = x.shape
    assert row0 % BF16_SUBLANE == 0
    return pl.pallas_call(
        _mm_kernel,
        grid=(m // tm,),
        in_specs=[pl.BlockSpec((tm, k), lambda i: (i, 0)),
                  pl.BlockSpec((pl.Element(n), pl.Element(k)), lambda i: (row0, 0))],
        out_specs=pl.BlockSpec((tm, n), lambda i: (i, 0)),
        out_shape=jax.ShapeDtypeStruct((m, n), _f32),
        compiler_params=_params("parallel"),
        name="matmul_small",
    )(x, w_rows)


def _mm_res_kernel(x_ref, w_ref, r_ref, o_ref):
    o_ref[...] = r_ref[...] + jnp.dot(x_ref[...], w_ref[...], preferred_element_type=_f32)


def _matmul_residual(x, w, layer, res, tm, tn):
    m, k = x.shape
    n = w.shape[2]
    return pl.pallas_call(
        _mm_res_kernel,
        grid=(n // tn, m // tm),
        in_specs=[pl.BlockSpec((tm, k), lambda j, i: (i, 0)),
                  pl.BlockSpec((None, k, tn), lambda j, i: (layer, 0, j)),
                  pl.BlockSpec((tm, tn), lambda j, i: (i, j))],
        out_specs=pl.BlockSpec((tm, tn), lambda j, i: (i, j)),
        out_shape=jax.ShapeDtypeStruct((m, n), _f32),
        compiler_params=_params("parallel", "parallel"),
        name="matmul_residual",
    )(x, w, res)


def _rope128(x, cos, sin_signed):
    lane = lax.broadcasted_iota(jnp.int32, x.shape, 1)
    partner = jnp.where(lane < ROPE // 2,
                        pltpu.roll(x, LANE - ROPE // 2, 1),
                        pltpu.roll(x, ROPE // 2, 1))
    return x * cos + partner * sin_signed


def _qproj_kernel(x_ref, g_ref, w_ref, cos_ref, sin_ref, o_ref, xn_ref, *, heads_per_step):
    @pl.when(pl.program_id(1) == 0)
    def _():
        xn_ref[...] = (_rms(x_ref[...], g_ref[...]) * MLA_SCALE2).astype(_bf16)

    acc = jnp.dot(xn_ref[...], w_ref[...], preferred_element_type=_f32)
    cos = cos_ref[...]
    sin = sin_ref[...]
    for hh in range(heads_per_step):
        c0 = hh * QK_CAT
        o_ref[:, c0:c0 + NOPE] = acc[:, c0:c0 + NOPE].astype(o_ref.dtype)
        roped = _rope128(acc[:, c0 + NOPE:c0 + QK_CAT], cos, sin)
        o_ref[:, c0 + NOPE:c0 + QK_CAT] = roped.astype(o_ref.dtype)


def _q_proj(small, g_q, w_q, layer, cos, sin, lp, tm, ql):
    m = small.shape[0]
    n = w_q.shape[2]
    hps = 4 if (n // QK_CAT) % 4 == 0 else 1
    tn = hps * QK_CAT
    nrow = lp // tm
    return pl.pallas_call(
        functools.partial(_qproj_kernel, heads_per_step=hps),
        grid=(m // tm, n // tn),
        in_specs=[pl.BlockSpec((tm, ql), lambda i, j: (i, 0)),
                  pl.BlockSpec((1, ql), lambda i, j: (0, 0)),
                  pl.BlockSpec((None, ql, tn), lambda i, j: (layer, 0, j)),
                  pl.BlockSpec((tm, LANE), lambda i, j: (i % nrow, 0)),
                  pl.BlockSpec((tm, LANE), lambda i, j: (i % nrow, 0))],
        out_specs=pl.BlockSpec((tm, tn), lambda i, j: (i, j)),
        out_shape=jax.ShapeDtypeStruct((m, n), _bf16),
        scratch_shapes=[pltpu.VMEM((tm, ql), _bf16)],
        compiler_params=_params("parallel", "arbitrary"),
        name="q_proj",
    )(small, g_q.reshape(1, ql), w_q, cos, sin)


def _kvproj_kernel(x_ref, kr_ref, g_ref, wk_ref, wv_ref, cos_ref, sin_ref,
                   k_ref, v_ref, krr_ref, xn_ref):
    @pl.when(pl.program_id(1) == 0)
    def _():
        xn_ref[...] = _rms(x_ref[...], g_ref[...]).astype(_bf16)
        krr_ref[...] = _rope128(kr_ref[...], cos_ref[...], sin_ref[...]).astype(krr_ref.dtype)

    xn = xn_ref[...]
    k_ref[...] = jnp.dot(xn, wk_ref[...], preferred_element_type=_f32).astype(k_ref.dtype)
    v_ref[...] = jnp.dot(xn, wv_ref[...], preferred_element_type=_f32).astype(v_ref.dtype)


def _kv_proj(small, g_kv, w_k, w_v, layer, cos, sin, lp, tm, ql, kvl):
    m = small.shape[0]
    heads = w_k.shape[2] // NOPE
    hps = 4 if heads % 4 == 0 else 1
    nrow = lp // tm
    return pl.pallas_call(
        _kvproj_kernel,
        grid=(m // tm, heads // hps),
        in_specs=[pl.BlockSpec((tm, kvl), lambda i, j: (i, ql // kvl)),
                  pl.BlockSpec((tm, LANE), lambda i, j: (i, (ql + kvl) // LANE)),
                  pl.BlockSpec((1, kvl), lambda i, j: (0, 0)),
                  pl.BlockSpec((None, kvl, hps * NOPE), lambda i, j: (layer, 0, j)),
                  pl.BlockSpec((None, kvl, hps * VDIM), lambda i, j: (layer, 0, j)),
                  pl.BlockSpec((tm, LANE), lambda i, j: (i % nrow, 0)),
                  pl.BlockSpec((tm, LANE), lambda i, j: (i % nrow, 0))],
        out_specs=[pl.BlockSpec((tm, hps * NOPE), lambda i, j: (i, j)),
                   pl.BlockSpec((tm, hps * VDIM), lambda i, j: (i, j)),
                   pl.BlockSpec((tm, LANE), lambda i, j: (i, 0))],
        out_shape=[jax.ShapeDtypeStruct((m, heads * NOPE), _bf16),
                   jax.ShapeDtypeStruct((m, heads * VDIM), _bf16),
                   jax.ShapeDtypeStruct((m, LANE), _bf16)],
        scratch_shapes=[pltpu.VMEM((tm, kvl), _bf16)],
        compiler_params=_params("parallel", "arbitrary"),
        name="kv_proj",
    )(small, small, g_kv.reshape(1, kvl), w_k, w_v, cos, sin)


def _dot_nt(a, b):
    return lax.dot_general(a, b, (((1,), (1,)), ((), ())), preferred_element_type=_f32)


def _dot_tn(a, b):
    return lax.dot_general(a, b, (((0,), (0,)), ((), ())), preferred_element_type=_f32)


def _neg_abs(x):
    bits = lax.bitcast_convert_type(x, jnp.uint32) | jnp.uint32(0x80000000)
    return lax.bitcast_convert_type(bits, _f32)


def _row_blocks(lp):
    n_main = lp // Q_BLK
    tail = lp - n_main * Q_BLK
    assert tail <= ATT_BLK, (lp, tail)
    return n_main, tail


def _key_query_iota(bk, bq):
    key = lax.broadcasted_iota(jnp.int32, (bk, bq), 0)
    qry = lax.broadcasted_iota(jnp.int32, (bk, bq), 1)
    return key, qry


def _mla_kernel(q_ref, k_ref, kr_ref, v_ref, o_ref, s_ref, p_ref, al_ref, m_ref, l_ref, acc_ref,
                *, lp, hps):
    n_main, tail = _row_blocks(lp)

    def q_block(q0, bq, n_full, is_tail):
        qs = [q_ref[pl.ds(q0, bq), a * QK_CAT:(a + 1) * QK_CAT] for a in range(hps)]

        def scores(a, k0, bk):
            k_cat = jnp.concatenate([k_ref[pl.ds(k0, bk), a * NOPE:(a + 1) * NOPE],
                                     kr_ref[pl.ds(k0, bk), :]], axis=1)
            return _dot_nt(k_cat, qs[a])

        def probs(a, s, slot, bk, mask_off):
            if mask_off is not None:
                key, qry = _key_query_iota(bk, bq)
                s = jnp.where(key + mask_off <= qry, s, NEG_BIG)
            m_prev = m_ref[a, :, 0:bq]
            m_new = jnp.maximum(m_prev, jnp.max(s, axis=0, keepdims=True))
            alpha = jnp.exp2(m_prev - m_new)
            p = jnp.exp2(s - m_new)
            l_ref[a, :, 0:bq] = alpha * l_ref[a, :, 0:bq] + jnp.sum(p, axis=0, keepdims=True)
            m_ref[a, :, 0:bq] = m_new
            p_ref[slot, a, 0:bk, 0:bq] = p.astype(_bf16)
            al_ref[slot, a, :, 0:bq] = alpha

        def values(a, slot, k0, bk):
            pv = _dot_tn(v_ref[pl.ds(k0, bk), a * VDIM:(a + 1) * VDIM], p_ref[slot, a, 0:bk, 0:bq])
            acc_ref[a, :, 0:bq] = al_ref[slot, a, :, 0:bq] * acc_ref[a, :, 0:bq] + pv

        first = 0 if is_tail else 1
        last = n_full - 1 if is_tail else n_full + 1
        for a in range(hps):
            m_ref[a, :, 0:bq] = jnp.full((1, bq), NEG_BIG, _f32)
            l_ref[a, :, 0:bq] = jnp.zeros((1, bq), _f32)
            acc_ref[a, :, 0:bq] = jnp.zeros((VDIM, bq), _f32)
            p_ref[1 - first, a, :, 0:bq] = jnp.zeros((ATT_BLK, bq), _bf16)
            al_ref[1 - first, a, :, 0:bq] = jnp.ones((1, bq), _f32)
            s_ref[first, a, :, 0:bq] = scores(a, 0, ATT_BLK)

        def step(c, slot, mask_off=None, prefetch=True):
            nxt = pl.multiple_of(jnp.minimum(c + 1, last) * ATT_BLK, ATT_BLK)
            prv = pl.multiple_of(jnp.maximum(c - 1, 0) * ATT_BLK, ATT_BLK)
            for a in range(hps):
                if prefetch:
                    s_ref[1 - slot, a, :, 0:bq] = scores(a, nxt, ATT_BLK)
                probs(a, s_ref[slot, a, :, 0:bq], slot, ATT_BLK, mask_off)
                values(a, 1 - slot, prv, ATT_BLK)

        def pair(t, carry):
            step(2 * t, first)
            step(2 * t + 1, 1 - first)
            return carry

        lax.fori_loop(0, n_full // 2, pair, 0)
        prv = pl.multiple_of(jnp.maximum(n_full - 1, 0) * ATT_BLK, ATT_BLK)
        if is_tail:
            for a in range(hps):
                probs(a, scores(a, q0, bq), 0, bq, 0)
                values(a, 1, prv, ATT_BLK)
                values(a, 0, q0, bq)
        else:
            step(n_full, 1, mask_off=0)
            step(n_full + 1, 0, mask_off=ATT_BLK, prefetch=False)
            k_last = pl.multiple_of((n_full + 1) * ATT_BLK, ATT_BLK)
            for a in range(hps):
                values(a, 0, k_last, ATT_BLK)
        for a in range(hps):
            out_t = acc_ref[a, :, 0:bq] * (1.0 / l_ref[a, :, 0:bq])
            o_ref[pl.ds(q0, bq), a * VDIM:(a + 1) * VDIM] = out_t.T.astype(o_ref.dtype)

    def main_block(i, carry):
        q_block(pl.multiple_of(i * Q_BLK, Q_BLK), Q_BLK, 2 * i, False)
        return carry

    lax.fori_loop(0, n_main, main_block, 0)
    if tail:
        q_block(n_main * Q_BLK, tail, n_main * (Q_BLK // ATT_BLK), True)


def _sb_kernel(q_ref, k_ref, v_ref, o_ref, s_ref, w_ref, r_ref, acc_ref, *, lp, hps):
    n_main, tail = _row_blocks(lp)
    key_i, key_j = _key_query_iota(ATT_BLK, ATT_BLK)
    later = (key_j > key_i).astype(_bf16)

    def q_block(q0, bq, n_full, is_tail):
        qs = [q_ref[pl.ds(q0, bq), a * VDIM:(a + 1) * VDIM] for a in range(hps)]

        def scores(a, k0, bk):
            return _dot_nt(k_ref[pl.ds(k0, bk), a * VDIM:(a + 1) * VDIM], qs[a])

        def weights(a, z_of, slot, bk, mask_off):
            for lo in range(0, bq, ATT_BLK):
                hi = min(lo + ATT_BLK, bq)
                if mask_off is not None and mask_off - lo >= hi - lo - 1:
                    w_ref[slot, a, 0:bk, lo:hi] = jnp.zeros((bk, hi - lo), _bf16)
                    continue
                z = z_of(lo, hi)
                tail_sp = jnp.log(1.0 + jnp.exp2(_neg_abs(z))) * LOG2E
                log_beta = jnp.minimum(z, 0.0) - tail_sp
                log_rest = log_beta - z
                masked = mask_off is not None
                if masked:
                    key, qry = _key_query_iota(bk, hi - lo)
                    strict = key + (mask_off - lo) < qry
                    log_rest = jnp.where(strict, log_rest, 0.0)
                within = jnp.dot(later[0:bk, 0:bk], log_rest.astype(_bf16),
                                 preferred_element_type=_f32)
                r = r_ref[a, :, lo:hi]
                w = jnp.exp2(log_beta + within + r)
                if masked:
                    w = jnp.where(strict, w, 0.0)
                w_ref[slot, a, 0:bk, lo:hi] = w.astype(_bf16)
                r_ref[a, :, lo:hi] = r + within[0:1, :] + log_rest[0:1, :]

        def from_value(z):
            return lambda lo, hi: z[:, lo:hi]

        def values(a, slot, k0, bk):
            acc_ref[a, :, 0:bq] = acc_ref[a, :, 0:bq] + _dot_tn(
                v_ref[pl.ds(k0, bk), a * VDIM:(a + 1) * VDIM], w_ref[slot, a, 0:bk, 0:bq])

        v_last = n_full - 1 if is_tail else n_full + 1

        def step(c, slot, mask_off=None):
            k_before = pl.multiple_of(jnp.maximum(c - 1, 0) * ATT_BLK, ATT_BLK)
            k_after = pl.multiple_of(jnp.minimum(c + 1, v_last) * ATT_BLK, ATT_BLK)
            for a in range(hps):
                s_ref[1 - slot, a, :, 0:bq] = scores(a, k_before, ATT_BLK)
                weights(a, lambda lo, hi: s_ref[slot, a, :, lo:hi], slot, ATT_BLK, mask_off)
                values(a, 1 - slot, k_after, ATT_BLK)

        k_top = pl.multiple_of(jnp.maximum(n_full - 1, 0) * ATT_BLK, ATT_BLK)
        for a in range(hps):
            r_ref[a, :, 0:bq] = jnp.zeros((1, bq), _f32)
            acc_ref[a, :, 0:bq] = jnp.zeros((VDIM, bq), _f32)
            if is_tail:
                s_ref[1, a, :, 0:bq] = scores(a, k_top, ATT_BLK)
                weights(a, from_value(scores(a, q0, bq)), 0, bq, 0)
                values(a, 0, q0, bq)
                w_ref[0, a, :, 0:bq] = jnp.zeros((ATT_BLK, bq), _bf16)
            else:
                k_hi = pl.multiple_of((n_full + 1) * ATT_BLK, ATT_BLK)
                s_ref[0, a, :, 0:bq] = scores(a, q0, ATT_BLK)
                weights(a, from_value(scores(a, k_hi, ATT_BLK)), 1, ATT_BLK, ATT_BLK)
        if not is_tail:
            step(n_full, 0, mask_off=0)

        n_pairs = n_full // 2

        def alive():
            r_max = r_ref[0, :, 0:bq]
            for a in range(1, hps):
                r_max = jnp.maximum(r_max, r_ref[a, :, 0:bq])
            return (jnp.max(r_max) > SB_DEAD_LOG2).astype(jnp.int32)

        def pair(carry):
            tt, _, _ = carry
            c = 2 * (n_pairs - 1 - tt) + 1
            step(c, 1)
            go_on = alive()

            @pl.when(go_on == 1)
            def _():
                step(c - 1, 0)

            return tt + 1, c - go_on, jnp.where(go_on == 1, alive(), 0)

        _, c_fin, _ = lax.while_loop(
            lambda carry: jnp.logical_and(carry[0] < n_pairs, carry[2] == 1), pair,
            (jnp.int32(0), jnp.int32(0) + n_full, jnp.int32(1)))
        k_fin = pl.multiple_of(jnp.minimum(c_fin, v_last) * ATT_BLK, ATT_BLK)
        for slot in range(2):
            @pl.when(c_fin % 2 == slot)
            def _():
                for a in range(hps):
                    values(a, slot, k_fin, ATT_BLK)

        for a in range(hps):
            o_ref[pl.ds(q0, bq), a * VDIM:(a + 1) * VDIM] = acc_ref[a, :, 0:bq].T.astype(o_ref.dtype)

    def main_block(i, carry):
        q_block(pl.multiple_of(i * Q_BLK, Q_BLK), Q_BLK, 2 * i, False)
        return carry

    lax.fori_loop(0, n_main, main_block, 0)
    if tail:
        q_block(n_main * Q_BLK, tail, n_main * (Q_BLK // ATT_BLK), True)


def _att_scratch(hps, slot_stats, n_stat):
    return ([pltpu.VMEM((2, hps, ATT_BLK, Q_BLK), _f32),
             pltpu.VMEM((2, hps, ATT_BLK, Q_BLK), _bf16)]
            + [pltpu.VMEM((2, hps, 1, Q_BLK), _f32)] * slot_stats
            + [pltpu.VMEM((hps, 1, Q_BLK), _f32)] * n_stat
            + [pltpu.VMEM((hps, VDIM, Q_BLK), _f32)])


def _mla_attention(q_cat, k_nope, k_rope, v, heads):
    b, lp, _ = q_cat.shape
    hps = MLA_HEADS_PER_STEP if heads % MLA_HEADS_PER_STEP == 0 else 1
    q_spec = pl.BlockSpec((None, lp, hps * QK_CAT), lambda bi, h: (bi, 0, h))
    kr_spec = pl.BlockSpec((None, lp, LANE), lambda bi, h: (bi, 0, 0))
    v_spec = pl.BlockSpec((None, lp, hps * VDIM), lambda bi, h: (bi, 0, h))
    return pl.pallas_call(
        functools.partial(_mla_kernel, lp=lp, hps=hps),
        grid=(b, heads // hps),
        in_specs=[q_spec, v_spec, kr_spec, v_spec],
        out_specs=v_spec,
        out_shape=jax.ShapeDtypeStruct((b, lp, heads * VDIM), _bf16),
        scratch_shapes=_att_scratch(hps, 1, 2),
        compiler_params=_params("parallel", "parallel"),
        name="mla_attention",
    )(q_cat, k_nope, k_rope, v)


def _sb_attention(big, heads, q_col, k_col, v_col):
    b, lp, _ = big.shape
    hps = SB_HEADS_PER_STEP if heads % SB_HEADS_PER_STEP == 0 else 1
    w = hps * VDIM

    def spec(col0):
        return pl.BlockSpec((None, lp, w), lambda bi, h: (bi, 0, col0 // w + h))

    return pl.pallas_call(
        functools.partial(_sb_kernel, lp=lp, hps=hps),
        grid=(b, heads // hps),
        in_specs=[spec(q_col), spec(k_col), spec(v_col)],
        out_specs=pl.BlockSpec((None, lp, w), lambda bi, h: (bi, 0, h)),
        out_shape=jax.ShapeDtypeStruct((b, lp, heads * VDIM), _bf16),
        scratch_shapes=_att_scratch(hps, 0, 1),
        compiler_params=_params("parallel", "parallel"),
        name="sb_attention",
    )(big, big, big)


def _gate_kernel(ym_ref, ys_ref, zm_ref, zs_ref, gm_ref, gs_ref, o_ref, *, w):
    rows = BF16_SUBLANE

    def one(y_ref, z_ref, g_ref, r0):
        z = z_ref[pl.ds(r0, rows), :].astype(_f32)
        silu = z / (1.0 + jnp.exp(-z))
        return (_rms(y_ref[pl.ds(r0, rows), :].astype(_f32), g_ref[...]) * silu).astype(o_ref.dtype)

    def group(i, carry):
        r0 = pl.multiple_of(i * rows, rows)
        o_ref[pl.ds(r0, rows), 0:w] = one(ym_ref, zm_ref, gm_ref, r0)
        o_ref[pl.ds(r0, rows), w:2 * w] = one(ys_ref, zs_ref, gs_ref, r0)
        return carry

    lax.fori_loop(0, o_ref.shape[0] // rows, group, 0, unroll=2)


def _gate(y_mla, y_sb, big, g_mla, g_sb, zm_col, zs_col, tm):
    m, w = y_mla.shape
    yspec = pl.BlockSpec((tm, w), lambda i: (i, 0))
    gspec = pl.BlockSpec((1, w), lambda i: (0, 0))
    return pl.pallas_call(
        functools.partial(_gate_kernel, w=w),
        grid=(m // tm,),
        in_specs=[yspec, yspec,
                  pl.BlockSpec((tm, w), lambda i: (i, zm_col // w)),
                  pl.BlockSpec((tm, w), lambda i: (i, zs_col // w)),
                  gspec, gspec],
        out_specs=pl.BlockSpec((tm, 2 * w), lambda i: (i, 0)),
        out_shape=jax.ShapeDtypeStruct((m, 2 * w), _bf16),
        compiler_params=_params("parallel"),
        name="gate",
    )(y_mla, y_sb, big, big, g_mla.reshape(1, w), g_sb.reshape(1, w))


def _rope_tables(lp):
    inv_freq = ROPE_THETA ** (-jnp.arange(0, ROPE, 2, dtype=_f32) / ROPE)
    ang = jnp.arange(lp, dtype=jnp.int32).astype(_f32)[:, None] * inv_freq[None, :]
    cos, sin = jnp.cos(ang), jnp.sin(ang)
    zeros = jnp.zeros((lp, LANE - ROPE), _f32)
    return (jnp.concatenate([cos, cos, zeros], axis=1),
            jnp.concatenate([-sin, sin, zeros], axis=1))


@jax.jit
def _forward(x, meta_tokens, g_norm, w_in, g_q, g_kv, w_uq, w_ukv, g_out_mla, g_out_sb,
             w_o, g_final):
    b, seq, d = x.shape
    depth = w_in.shape[0]
    ql, kvl = g_q.shape[1], g_kv.shape[1]
    heads = w_uq.shape[2] // (NOPE + ROPE)
    w_grp = heads * VDIM
    assert w_ukv.shape[2] == heads * (NOPE + VDIM)
    assert w_in.shape[2] == ql + kvl + ROPE + 5 * w_grp
    assert ql % kvl == 0 and (ql + kvl) % LANE == 0 and kvl % LANE == 0

    l_real = N_META + seq
    lp = -(-l_real // LANE) * LANE
    mp = b * lp

    n_small = ql + kvl + ROPE
    n_in = w_in.shape[2]
    w_rows = jnp.swapaxes(w_in, 1, 2).astype(_bf16).reshape(depth * n_in, d)
    n_small_win = ql + kvl + LANE
    zm_col, qs_col, ks_col, vs_col, zs_col = (i * w_grp for i in range(5))
    colscale = jnp.ones((1, 5 * w_grp), _f32).at[:, qs_col:qs_col + w_grp].set(SB_SCALE2)
    wq = w_uq.reshape(depth, ql, heads, NOPE + ROPE)
    wq = jnp.pad(wq, ((0, 0), (0, 0), (0, 0), (0, QK_CAT - NOPE - ROPE)))
    wq = wq.reshape(depth, ql, heads * QK_CAT).astype(_bf16)
    wkv = w_ukv.reshape(depth, kvl, heads, NOPE + VDIM)
    wk = wkv[..., :NOPE].reshape(depth, kvl, heads * NOPE).astype(_bf16)
    wv = wkv[..., NOPE:].reshape(depth, kvl, heads * VDIM).astype(_bf16)
    wo = w_o.astype(_bf16)
    cos, sin = _rope_tables(lp)

    meta = jnp.broadcast_to(meta_tokens[None].astype(x.dtype), (b, N_META, d))
    h = jnp.concatenate([meta, x, jnp.zeros((b, lp - l_real, d), x.dtype)], axis=1)
    h = h.reshape(mp, d)

    tm_norm = _pick_tile(lp, 384)
    tm_mm = _pick_tile(lp, 528)
    tm_proj = _pick_tile(lp, 1056)
    tn_big = 1024 if (5 * w_grp) % 1024 == 0 else 512
    tn_out = 1024 if d % 1024 == 0 else 512

    for i in range(depth):
        u = _rmsnorm(h, g_norm[i], _bf16, tm_norm)
        small = _matmul_small(u, w_rows, i * n_in, n_small_win, tm_mm)
        big = _matmul_colscale(u, w_rows, i * n_in + n_small, 5 * w_grp, colscale, _bf16,
                               tm_proj, tn_big)
        q_cat = _q_proj(small, g_q[i], wq, i, cos, sin, lp, tm_proj, ql)
        k_nope, v_mla, k_rope = _kv_proj(small, g_kv[i], wk, wv, i, cos, sin, lp, tm_proj, ql, kvl)
        y_mla = _mla_attention(q_cat.reshape(b, lp, -1), k_nope.reshape(b, lp, -1),
                               k_rope.reshape(b, lp, -1), v_mla.reshape(b, lp, -1), heads)
        y_sb = _sb_attention(big.reshape(b, lp, -1), heads, qs_col, ks_col, vs_col)
        y = _gate(y_mla.reshape(mp, w_grp), y_sb.reshape(mp, w_grp), big,
                  g_out_mla[i], g_out_sb[i], zm_col, zs_col, tm_norm)
        h = _matmul_residual(y, wo, i, h, tm_mm, tn_out)

    return _final_norm(h.reshape(b, lp, d), g_final, seq, x.dtype, _pick_tile(seq, 512))


def kernel(x, meta_tokens, g_norm, w_in, g_q, g_kv, w_uq, w_ukv, g_out_mla, g_out_sb, w_o, g_final):
    return _forward(x, meta_tokens, g_norm, w_in, g_q, g_kv, w_uq, w_ukv, g_out_mla, g_out_sb,
                    w_o, g_final)
```

```python
import functools
import math

import jax
import jax.numpy as jnp
from jax import lax
from jax.experimental import pallas as pl
from jax.experimental.pallas import tpu as pltpu

N_META = 16
NOPE = 128
ROPE = 64
VDIM = 128
QK_CAT = 256
ROPE_THETA = 10000.0
EPS = 1e-6
LOG2E = math.log2(math.e)
MLA_SCALE2 = LOG2E / math.sqrt(NOPE + ROPE)
SB_SCALE2 = LOG2E / math.sqrt(VDIM)
LANE = 128
BF16_SUBLANE = 16
ATT_BLK = 256
Q_BLK = 2 * ATT_BLK
MLA_HEADS_PER_STEP = 4
SB_HEADS_PER_STEP = 4
VMEM_LIMIT = 56 * 1024 * 1024
NEG_BIG = -1e30
SB_DEAD_LOG2 = -160.0

_f32 = jnp.float32
_bf16 = jnp.bfloat16


def _pick_tile(n, target):
    best = None
    for t in range(BF16_SUBLANE, min(n, target) + 1, BF16_SUBLANE):
        if n % t == 0:
            best = t
    assert best is not None, (n, target)
    return best


def _params(*sem):
    return pltpu.CompilerParams(dimension_semantics=sem, vmem_limit_bytes=VMEM_LIMIT)


def _rms(xf, g):
    return xf * lax.rsqrt(jnp.mean(xf * xf, axis=-1, keepdims=True) + EPS) * g


def _rmsnorm_kernel(x_ref, g_ref, o_ref):
    o_ref[...] = _rms(x_ref[...].astype(_f32), g_ref[...]).astype(o_ref.dtype)


def _rmsnorm(x, g, out_dtype, tm):
    m, d = x.shape
    return pl.pallas_call(
        _rmsnorm_kernel,
        grid=(m // tm,),
        in_specs=[pl.BlockSpec((tm, d), lambda i: (i, 0)),
                  pl.BlockSpec((1, d), lambda i: (0, 0))],
        out_specs=pl.BlockSpec((tm, d), lambda i: (i, 0)),
        out_shape=jax.ShapeDtypeStruct((m, d), out_dtype),
        compiler_params=_params("parallel"),
        name="rmsnorm",
    )(x, g.reshape(1, d))


def _final_norm(h, g, seq, out_dtype, tm):
    b, lp, d = h.shape
    sub = 8
    assert N_META % sub == 0 and lp % sub == 0 and tm % sub == 0

    return pl.pallas_call(
        _rmsnorm_kernel,
        grid=(b, seq // tm),
        in_specs=[pl.BlockSpec((pl.Element(tm), pl.Element(d)),
                               lambda bi, i: (pl.multiple_of(bi * lp + N_META + i * tm, sub), 0)),
                  pl.BlockSpec((1, d), lambda bi, i: (0, 0))],
        out_specs=pl.BlockSpec((None, tm, d), lambda bi, i: (bi, i, 0)),
        out_shape=jax.ShapeDtypeStruct((b, seq, d), out_dtype),
        compiler_params=_params("parallel", "parallel"),
        name="final_norm",
    )(h.reshape(b * lp, d), g.reshape(1, d))


def _mm_scale_kernel(x_ref, w_ref, s_ref, o_ref):
    acc = lax.dot_general(x_ref[...], w_ref[...], (((1,), (1,)), ((), ())),
                          preferred_element_type=_f32)
    o_ref[...] = (acc * s_ref[...]).astype(o_ref.dtype)


def _matmul_colscale(x, w_rows, row0, n, colscale, out_dtype, tm, tn):
    m, k = x.shape
    assert row0 % BF16_SUBLANE == 0 and tn % BF16_SUBLANE == 0
    return pl.pallas_call(
        _mm_scale_kernel,
        grid=(n // tn, m // tm),
        in_specs=[pl.BlockSpec((tm, k), lambda j, i: (i, 0)),
                  pl.BlockSpec((pl.Element(tn), pl.Element(k)),
                               lambda j, i: (pl.multiple_of(row0 + j * tn, BF16_SUBLANE), 0)),
                  pl.BlockSpec((1, tn), lambda j, i: (0, j))],
        out_specs=pl.BlockSpec((tm, tn), lambda j, i: (i, j)),
        out_shape=jax.ShapeDtypeStruct((m, n), out_dtype),
        compiler_params=_params("parallel", "parallel"),
        name="matmul_colscale",
    )(x, w_rows, colscale)


def _mm_kernel(x_ref, w_ref, o_ref):
    o_ref[...] = lax.dot_general(x_ref[...], w_ref[...], (((1,), (1,)), ((), ())),
                                 preferred_element_type=_f32).astype(o_ref.dtype)


def _matmul_small(x, w_rows, row0, n, tm):
    m, k = x.shape
    assert row0 % BF16_SUBLANE == 0
    return pl.pallas_call(
        _mm_kernel,
        grid=(m // tm,),
        in_specs=[pl.BlockSpec((tm, k), lambda i: (i, 0)),
                  pl.BlockSpec((pl.Element(n), pl.Element(k)), lambda i: (row0, 0))],
        out_specs=pl.BlockSpec((tm, n), lambda i: (i, 0)),
        out_shape=jax.ShapeDtypeStruct((m, n), _f32),
        compiler_params=_params("parallel"),
        name="matmul_small",
    )(x, w_rows)


def _mm_res_kernel(x_ref, w_ref, r_ref, o_ref):
    o_ref[...] = r_ref[...] + jnp.dot(x_ref[...], w_ref[...], preferred_element_type=_f32)


def _matmul_residual(x, w, layer, res, tm, tn):
    m, k = x.shape
    n = w.shape[2]
    return pl.pallas_call(
        _mm_res_kernel,
        grid=(n // tn, m // tm),
        in_specs=[pl.BlockSpec((tm, k), lambda j, i: (i, 0)),
                  pl.BlockSpec((None, k, tn), lambda j, i: (layer, 0, j)),
                  pl.BlockSpec((tm, tn), lambda j, i: (i, j))],
        out_specs=pl.BlockSpec((tm, tn), lambda j, i: (i, j)),
        out_shape=jax.ShapeDtypeStruct((m, n), _f32),
        compiler_params=_params("parallel", "parallel"),
        name="matmul_residual",
    )(x, w, res)


def _rope128(x, cos, sin_signed):
    lane = lax.broadcasted_iota(jnp.int32, x.shape, 1)
    partner = jnp.where(lane < ROPE // 2,
                        pltpu.roll(x, LANE - ROPE // 2, 1),
                        pltpu.roll(x, ROPE // 2, 1))
    return x * cos + partner * sin_signed


def _qproj_kernel(x_ref, g_ref, w_ref, cos_ref, sin_ref, o_ref, xn_ref, *, heads_per_step):
    @pl.when(pl.program_id(1) == 0)
    def _():
        xn_ref[...] = (_rms(x_ref[...], g_ref[...]) * MLA_SCALE2).astype(_bf16)

    acc = jnp.dot(xn_ref[...], w_ref[...], preferred_element_type=_f32)
    cos = cos_ref[...]
    sin = sin_ref[...]
    for hh in range(heads_per_step):
        c0 = hh * QK_CAT
        o_ref[:, c0:c0 + NOPE] = acc[:, c0:c0 + NOPE].astype(o_ref.dtype)
        roped = _rope128(acc[:, c0 + NOPE:c0 + QK_CAT], cos, sin)
        o_ref[:, c0 + NOPE:c0 + QK_CAT] = roped.astype(o_ref.dtype)


def _q_proj(small, g_q, w_q, layer, cos, sin, lp, tm, ql):
    m = small.shape[0]
    n = w_q.shape[2]
    hps = 4 if (n // QK_CAT) % 4 == 0 else 1
    tn = hps * QK_CAT
    nrow = lp // tm
    return pl.pallas_call(
        functools.partial(_qproj_kernel, heads_per_step=hps),
        grid=(m // tm, n // tn),
        in_specs=[pl.BlockSpec((tm, ql), lambda i, j: (i, 0)),
                  pl.BlockSpec((1, ql), lambda i, j: (0, 0)),
                  pl.BlockSpec((None, ql, tn), lambda i, j: (layer, 0, j)),
                  pl.BlockSpec((tm, LANE), lambda i, j: (i % nrow, 0)),
                  pl.BlockSpec((tm, LANE), lambda i, j: (i % nrow, 0))],
        out_specs=pl.BlockSpec((tm, tn), lambda i, j: (i, j)),
        out_shape=jax.ShapeDtypeStruct((m, n), _bf16),
        scratch_shapes=[pltpu.VMEM((tm, ql), _bf16)],
        compiler_params=_params("parallel", "arbitrary"),
        name="q_proj",
    )(small, g_q.reshape(1, ql), w_q, cos, sin)


def _kvproj_kernel(x_ref, kr_ref, g_ref, wk_ref, wv_ref, cos_ref, sin_ref,
                   k_ref, v_ref, krr_ref, xn_ref):
    @pl.when(pl.program_id(1) == 0)
    def _():
        xn_ref[...] = _rms(x_ref[...], g_ref[...]).astype(_bf16)
        krr_ref[...] = _rope128(kr_ref[...], cos_ref[...], sin_ref[...]).astype(krr_ref.dtype)

    xn = xn_ref[...]
    k_ref[...] = jnp.dot(xn, wk_ref[...], preferred_element_type=_f32).astype(k_ref.dtype)
    v_ref[...] = jnp.dot(xn, wv_ref[...], preferred_element_type=_f32).astype(v_ref.dtype)


def _kv_proj(small, g_kv, w_k, w_v, layer, cos, sin, lp, tm, ql, kvl):
    m = small.shape[0]
    heads = w_k.shape[2] // NOPE
    hps = 4 if heads % 4 == 0 else 1
    nrow = lp // tm
    return pl.pallas_call(
        _kvproj_kernel,
        grid=(m // tm, heads // hps),
        in_specs=[pl.BlockSpec((tm, kvl), lambda i, j: (i, ql // kvl)),
                  pl.BlockSpec((tm, LANE), lambda i, j: (i, (ql + kvl) // LANE)),
                  pl.BlockSpec((1, kvl), lambda i, j: (0, 0)),
                  pl.BlockSpec((None, kvl, hps * NOPE), lambda i, j: (layer, 0, j)),
                  pl.BlockSpec((None, kvl, hps * VDIM), lambda i, j: (layer, 0, j)),
                  pl.BlockSpec((tm, LANE), lambda i, j: (i % nrow, 0)),
                  pl.BlockSpec((tm, LANE), lambda i, j: (i % nrow, 0))],
        out_specs=[pl.BlockSpec((tm, hps * NOPE), lambda i, j: (i, j)),
                   pl.BlockSpec((tm, hps * VDIM), lambda i, j: (i, j)),
                   pl.BlockSpec((tm, LANE), lambda i, j: (i, 0))],
        out_shape=[jax.ShapeDtypeStruct((m, heads * NOPE), _bf16),
                   jax.ShapeDtypeStruct((m, heads * VDIM), _bf16),
                   jax.ShapeDtypeStruct((m, LANE), _bf16)],
        scratch_shapes=[pltpu.VMEM((tm, kvl), _bf16)],
        compiler_params=_params("parallel", "arbitrary"),
        name="kv_proj",
    )(small, small, g_kv.reshape(1, kvl), w_k, w_v, cos, sin)


def _dot_nt(a, b):
    return lax.dot_general(a, b, (((1,), (1,)), ((), ())), preferred_element_type=_f32)


def _dot_tn(a, b):
    return lax.dot_general(a, b, (((0,), (0,)), ((), ())), preferred_element_type=_f32)


def _neg_abs(x):
    bits = lax.bitcast_convert_type(x, jnp.uint32) | jnp.uint32(0x80000000)
    return lax.bitcast_convert_type(bits, _f32)


def _row_blocks(lp):
    n_main = lp // Q_BLK
    tail = lp - n_main * Q_BLK
    assert tail <= ATT_BLK, (lp, tail)
    return n_main, tail


def _key_query_iota(bk, bq):
    key = lax.broadcasted_iota(jnp.int32, (bk, bq), 0)
    qry = lax.broadcasted_iota(jnp.int32, (bk, bq), 1)
    return key, qry


def _mla_kernel(q_ref, k_ref, kr_ref, v_ref, o_ref, s_ref, p_ref, al_ref, m_ref, l_ref, acc_ref,
                *, lp, hps):
    n_main, tail = _row_blocks(lp)

    def q_block(q0, bq, n_full, is_tail):
        qs = [q_ref[pl.ds(q0, bq), a * QK_CAT:(a + 1) * QK_CAT] for a in range(hps)]

        full = (0, bq)
        upper = (ATT_BLK, bq)

        def scores(a, k0, bk, lanes=full):
            k_cat = jnp.concatenate([k_ref[pl.ds(k0, bk), a * NOPE:(a + 1) * NOPE],
                                     kr_ref[pl.ds(k0, bk), :]], axis=1)
            return _dot_nt(k_cat, qs[a][lanes[0]:lanes[1]])

        def probs(a, s, slot, bk, mask_off, lanes=full):
            lo, hi = lanes
            if mask_off is not None:
                key, qry = _key_query_iota(bk, hi - lo)
                s = jnp.where(key + mask_off <= qry, s, NEG_BIG)
            m_prev = m_ref[a, :, lo:hi]
            m_new = jnp.maximum(m_prev, jnp.max(s, axis=0, keepdims=True))
            alpha = jnp.exp2(m_prev - m_new)
            p = jnp.exp2(s - m_new)
            l_ref[a, :, lo:hi] = alpha * l_ref[a, :, lo:hi] + jnp.sum(p, axis=0, keepdims=True)
            m_ref[a, :, lo:hi] = m_new
            p_ref[slot, a, 0:bk, lo:hi] = p.astype(_bf16)
            al_ref[slot, a, :, lo:hi] = alpha

        def values(a, slot, k0, bk, lanes=full):
            lo, hi = lanes
            pv = _dot_tn(v_ref[pl.ds(k0, bk), a * VDIM:(a + 1) * VDIM], p_ref[slot, a, 0:bk, lo:hi])
            acc_ref[a, :, lo:hi] = al_ref[slot, a, :, lo:hi] * acc_ref[a, :, lo:hi] + pv

        first = 0 if is_tail else 1
        last = n_full - 1 if is_tail else n_full + 1
        for a in range(hps):
            m_ref[a, :, 0:bq] = jnp.full((1, bq), NEG_BIG, _f32)
            l_ref[a, :, 0:bq] = jnp.zeros((1, bq), _f32)
            acc_ref[a, :, 0:bq] = jnp.zeros((VDIM, bq), _f32)
            p_ref[1 - first, a, :, 0:bq] = jnp.zeros((ATT_BLK, bq), _bf16)
            al_ref[1 - first, a, :, 0:bq] = jnp.ones((1, bq), _f32)
            s_ref[first, a, :, 0:bq] = scores(a, 0, ATT_BLK)

        def step(c, slot, mask_off=None, prefetch=full, lanes=full):
            nxt = pl.multiple_of(jnp.minimum(c + 1, last) * ATT_BLK, ATT_BLK)
            prv = pl.multiple_of(jnp.maximum(c - 1, 0) * ATT_BLK, ATT_BLK)
            for a in range(hps):
                if prefetch is not None:
                    s_ref[1 - slot, a, :, prefetch[0]:prefetch[1]] = scores(a, nxt, ATT_BLK, prefetch)
                probs(a, s_ref[slot, a, :, lanes[0]:lanes[1]], slot, ATT_BLK, mask_off, lanes)
                values(a, 1 - slot, prv, ATT_BLK)

        def pair(t, carry):
            step(2 * t, first)
            step(2 * t + 1, 1 - first)
            return carry

        lax.fori_loop(0, n_full // 2, pair, 0)
        prv = pl.multiple_of(jnp.maximum(n_full - 1, 0) * ATT_BLK, ATT_BLK)
        if is_tail:
            for a in range(hps):
                probs(a, scores(a, q0, bq), 0, bq, 0)
                values(a, 1, prv, ATT_BLK)
                values(a, 0, q0, bq)
        else:
            step(n_full, 1, mask_off=0, prefetch=upper)
            step(n_full + 1, 0, mask_off=0, prefetch=None, lanes=upper)
            k_last = pl.multiple_of((n_full + 1) * ATT_BLK, ATT_BLK)
            for a in range(hps):
                values(a, 0, k_last, ATT_BLK, upper)
        for a in range(hps):
            out_t = acc_ref[a, :, 0:bq] * (1.0 / l_ref[a, :, 0:bq])
            o_ref[pl.ds(q0, bq), a * VDIM:(a + 1) * VDIM] = out_t.T.astype(o_ref.dtype)

    def main_block(i, carry):
        q_block(pl.multiple_of(i * Q_BLK, Q_BLK), Q_BLK, 2 * i, False)
        return carry

    lax.fori_loop(0, n_main, main_block, 0)
    if tail:
        q_block(n_main * Q_BLK, tail, n_main * (Q_BLK // ATT_BLK), True)


def _sb_kernel(q_ref, k_ref, v_ref, o_ref, s_ref, w_ref, r_ref, acc_ref, *, lp, hps):
    n_main, tail = _row_blocks(lp)
    key_i, key_j = _key_query_iota(ATT_BLK, ATT_BLK)
    later = (key_j > key_i).astype(_bf16)

    def q_block(q0, bq, n_full, is_tail):
        qs = [q_ref[pl.ds(q0, bq), a * VDIM:(a + 1) * VDIM] for a in range(hps)]

        def scores(a, k0, bk, lanes=None):
            q = qs[a] if lanes is None else qs[a][lanes[0]:lanes[1]]
            return _dot_nt(k_ref[pl.ds(k0, bk), a * VDIM:(a + 1) * VDIM], q)

        full = (0, bq)
        upper = (ATT_BLK, bq)

        def weights(a, z_of, slot, bk, mask_off, lanes=full):
            for lo in range(lanes[0], lanes[1], ATT_BLK):
                hi = min(lo + ATT_BLK, lanes[1])
                z = z_of(lo, hi)
                tail_sp = jnp.log(1.0 + jnp.exp2(_neg_abs(z))) * LOG2E
                log_beta = jnp.minimum(z, 0.0) - tail_sp
                log_rest = log_beta - z
                masked = mask_off is not None
                if masked:
                    key, qry = _key_query_iota(bk, hi - lo)
                    strict = key + (mask_off - lo) < qry
                    log_rest = jnp.where(strict, log_rest, 0.0)
                within = jnp.dot(later[0:bk, 0:bk], log_rest.astype(_bf16),
                                 preferred_element_type=_f32)
                r = r_ref[a, :, lo:hi]
                w = jnp.exp2(log_beta + within + r)
                if masked:
                    w = jnp.where(strict, w, 0.0)
                w_ref[slot, a, 0:bk, lo:hi] = w.astype(_bf16)
                r_ref[a, :, lo:hi] = r + within[0:1, :] + log_rest[0:1, :]

        def from_value(z, lane0=0):
            return lambda lo, hi: z[:, lo - lane0:hi - lane0]

        def values(a, slot, k0, bk, lanes=full):
            lo, hi = lanes
            acc_ref[a, :, lo:hi] = acc_ref[a, :, lo:hi] + _dot_tn(
                v_ref[pl.ds(k0, bk), a * VDIM:(a + 1) * VDIM], w_ref[slot, a, 0:bk, lo:hi])

        v_last = n_full - 1 if is_tail else n_full + 1

        def step(c, slot, mask_off=None, v_lanes=full):
            k_before = pl.multiple_of(jnp.maximum(c - 1, 0) * ATT_BLK, ATT_BLK)
            k_after = pl.multiple_of(jnp.minimum(c + 1, v_last) * ATT_BLK, ATT_BLK)
            for a in range(hps):
                s_ref[1 - slot, a, :, 0:bq] = scores(a, k_before, ATT_BLK)
                weights(a, lambda lo, hi: s_ref[slot, a, :, lo:hi], slot, ATT_BLK, mask_off)
                values(a, 1 - slot, k_after, ATT_BLK, v_lanes)

        k_top = pl.multiple_of(jnp.maximum(n_full - 1, 0) * ATT_BLK, ATT_BLK)
        for a in range(hps):
            r_ref[a, :, 0:bq] = jnp.zeros((1, bq), _f32)
            acc_ref[a, :, 0:bq] = jnp.zeros((VDIM, bq), _f32)
            if is_tail:
                s_ref[1, a, :, 0:bq] = scores(a, k_top, ATT_BLK)
                weights(a, from_value(scores(a, q0, bq)), 0, bq, 0)
                values(a, 0, q0, bq)
                w_ref[0, a, :, 0:bq] = jnp.zeros((ATT_BLK, bq), _bf16)
            else:
                k_hi = pl.multiple_of((n_full + 1) * ATT_BLK, ATT_BLK)
                s_ref[0, a, :, 0:bq] = scores(a, q0, ATT_BLK)
                weights(a, from_value(scores(a, k_hi, ATT_BLK, upper), ATT_BLK), 1, ATT_BLK,
                        ATT_BLK, upper)
        if not is_tail:
            step(n_full, 0, mask_off=0, v_lanes=upper)

        n_pairs = n_full // 2

        def alive():
            r_max = r_ref[0, :, 0:bq]
            for a in range(1, hps):
                r_max = jnp.maximum(r_max, r_ref[a, :, 0:bq])
            return (jnp.max(r_max) > SB_DEAD_LOG2).astype(jnp.int32)

        def pair(carry):
            tt, _, _ = carry
            c = 2 * (n_pairs - 1 - tt) + 1
            step(c, 1)
            go_on = alive()

            @pl.when(go_on == 1)
            def _():
                step(c - 1, 0)

            return tt + 1, c - go_on, jnp.where(go_on == 1, alive(), 0)

        _, c_fin, _ = lax.while_loop(
            lambda carry: jnp.logical_and(carry[0] < n_pairs, carry[2] == 1), pair,
            (jnp.int32(0), jnp.int32(0) + n_full, jnp.int32(1)))
        k_fin = pl.multiple_of(jnp.minimum(c_fin, v_last) * ATT_BLK, ATT_BLK)
        for a in range(hps):
            values(a, c_fin % 2, k_fin, ATT_BLK)
            o_ref[pl.ds(q0, bq), a * VDIM:(a + 1) * VDIM] = acc_ref[a, :, 0:bq].T.astype(o_ref.dtype)

    def main_block(i, carry):
        q_block(pl.multiple_of(i * Q_BLK, Q_BLK), Q_BLK, 2 * i, False)
        return carry

    lax.fori_loop(0, n_main, main_block, 0)
    if tail:
        q_block(n_main * Q_BLK, tail, n_main * (Q_BLK // ATT_BLK), True)


def _att_scratch(hps, slot_stats, n_stat):
    return ([pltpu.VMEM((2, hps, ATT_BLK, Q_BLK), _f32),
             pltpu.VMEM((2, hps, ATT_BLK, Q_BLK), _bf16)]
            + [pltpu.VMEM((2, hps, 1, Q_BLK), _f32)] * slot_stats
            + [pltpu.VMEM((hps, 1, Q_BLK), _f32)] * n_stat
            + [pltpu.VMEM((hps, VDIM, Q_BLK), _f32)])


def _mla_attention(q_cat, k_nope, k_rope, v, heads):
    b, lp, _ = q_cat.shape
    hps = MLA_HEADS_PER_STEP if heads % MLA_HEADS_PER_STEP == 0 else 1
    q_spec = pl.BlockSpec((None, lp, hps * QK_CAT), lambda bi, h: (bi, 0, h))
    kr_spec = pl.BlockSpec((None, lp, LANE), lambda bi, h: (bi, 0, 0))
    v_spec = pl.BlockSpec((None, lp, hps * VDIM), lambda bi, h: (bi, 0, h))
    return pl.pallas_call(
        functools.partial(_mla_kernel, lp=lp, hps=hps),
        grid=(b, heads // hps),
        in_specs=[q_spec, v_spec, kr_spec, v_spec],
        out_specs=v_spec,
        out_shape=jax.ShapeDtypeStruct((b, lp, heads * VDIM), _bf16),
        scratch_shapes=_att_scratch(hps, 1, 2),
        compiler_params=_params("parallel", "parallel"),
        name="mla_attention",
    )(q_cat, k_nope, k_rope, v)


def _sb_attention(big, heads, q_col, k_col, v_col):
    b, lp, _ = big.shape
    hps = SB_HEADS_PER_STEP if heads % SB_HEADS_PER_STEP == 0 else 1
    w = hps * VDIM

    def spec(col0):
        return pl.BlockSpec((None, lp, w), lambda bi, h: (bi, 0, col0 // w + h))

    return pl.pallas_call(
        functools.partial(_sb_kernel, lp=lp, hps=hps),
        grid=(b, heads // hps),
        in_specs=[spec(q_col), spec(k_col), spec(v_col)],
        out_specs=pl.BlockSpec((None, lp, w), lambda bi, h: (bi, 0, h)),
        out_shape=jax.ShapeDtypeStruct((b, lp, heads * VDIM), _bf16),
        scratch_shapes=_att_scratch(hps, 0, 1),
        compiler_params=_params("parallel", "parallel"),
        name="sb_attention",
    )(big, big, big)


def _gate_kernel(ym_ref, ys_ref, zm_ref, zs_ref, gm_ref, gs_ref, o_ref, *, w):
    rows = BF16_SUBLANE

    def one(y_ref, z_ref, g_ref, r0):
        z = z_ref[pl.ds(r0, rows), :].astype(_f32)
        silu = z / (1.0 + jnp.exp(-z))
        return (_rms(y_ref[pl.ds(r0, rows), :].astype(_f32), g_ref[...]) * silu).astype(o_ref.dtype)

    def group(i, carry):
        r0 = pl.multiple_of(i * rows, rows)
        o_ref[pl.ds(r0, rows), 0:w] = one(ym_ref, zm_ref, gm_ref, r0)
        o_ref[pl.ds(r0, rows), w:2 * w] = one(ys_ref, zs_ref, gs_ref, r0)
        return carry

    lax.fori_loop(0, o_ref.shape[0] // rows, group, 0, unroll=2)


def _gate(y_mla, y_sb, big, g_mla, g_sb, zm_col, zs_col, tm):
    m, w = y_mla.shape
    yspec = pl.BlockSpec((tm, w), lambda i: (i, 0))
    gspec = pl.BlockSpec((1, w), lambda i: (0, 0))
    return pl.pallas_call(
        functools.partial(_gate_kernel, w=w),
        grid=(m // tm,),
        in_specs=[yspec, yspec,
                  pl.BlockSpec((tm, w), lambda i: (i, zm_col // w)),
                  pl.BlockSpec((tm, w), lambda i: (i, zs_col // w)),
                  gspec, gspec],
        out_specs=pl.BlockSpec((tm, 2 * w), lambda i: (i, 0)),
        out_shape=jax.ShapeDtypeStruct((m, 2 * w), _bf16),
        compiler_params=_params("parallel"),
        name="gate",
    )(y_mla, y_sb, big, big, g_mla.reshape(1, w), g_sb.reshape(1, w))


def _rope_tables(lp):
    inv_freq = ROPE_THETA ** (-jnp.arange(0, ROPE, 2, dtype=_f32) / ROPE)
    ang = jnp.arange(lp, dtype=jnp.int32).astype(_f32)[:, None] * inv_freq[None, :]
    cos, sin = jnp.cos(ang), jnp.sin(ang)
    zeros = jnp.zeros((lp, LANE - ROPE), _f32)
    return (jnp.concatenate([cos, cos, zeros], axis=1),
            jnp.concatenate([-sin, sin, zeros], axis=1))


@jax.jit
def _forward(x, meta_tokens, g_norm, w_in, g_q, g_kv, w_uq, w_ukv, g_out_mla, g_out_sb,
             w_o, g_final):
    b, seq, d = x.shape
    depth = w_in.shape[0]
    ql, kvl = g_q.shape[1], g_kv.shape[1]
    heads = w_uq.shape[2] // (NOPE + ROPE)
    w_grp = heads * VDIM
    assert w_ukv.shape[2] == heads * (NOPE + VDIM)
    assert w_in.shape[2] == ql + kvl + ROPE + 5 * w_grp
    assert ql % kvl == 0 and (ql + kvl) % LANE == 0 and kvl % LANE == 0

    l_real = N_META + seq
    lp = -(-l_real // LANE) * LANE
    mp = b * lp

    n_small = ql + kvl + ROPE
    n_in = w_in.shape[2]
    w_rows = jnp.swapaxes(w_in, 1, 2).astype(_bf16).reshape(depth * n_in, d)
    n_small_win = ql + kvl + LANE
    zm_col, qs_col, ks_col, vs_col, zs_col = (i * w_grp for i in range(5))
    colscale = jnp.ones((1, 5 * w_grp), _f32).at[:, qs_col:qs_col + w_grp].set(SB_SCALE2)
    wq = w_uq.reshape(depth, ql, heads, NOPE + ROPE)
    wq = jnp.pad(wq, ((0, 0), (0, 0), (0, 0), (0, QK_CAT - NOPE - ROPE)))
    wq = wq.reshape(depth, ql, heads * QK_CAT).astype(_bf16)
    wkv = w_ukv.reshape(depth, kvl, heads, NOPE + VDIM)
    wk = wkv[..., :NOPE].reshape(depth, kvl, heads * NOPE).astype(_bf16)
    wv = wkv[..., NOPE:].reshape(depth, kvl, heads * VDIM).astype(_bf16)
    wo = w_o.astype(_bf16)
    cos, sin = _rope_tables(lp)

    meta = jnp.broadcast_to(meta_tokens[None].astype(x.dtype), (b, N_META, d))
    h = jnp.concatenate([meta, x, jnp.zeros((b, lp - l_real, d), x.dtype)], axis=1)
    h = h.reshape(mp, d)

    tm_norm = _pick_tile(lp, 384)
    tm_mm = _pick_tile(lp, 528)
    tm_proj = _pick_tile(lp, 1056)
    tn_big = 1024 if (5 * w_grp) % 1024 == 0 else 512
    tn_out = 1024 if d % 1024 == 0 else 512

    for i in range(depth):
        u = _rmsnorm(h, g_norm[i], _bf16, tm_norm)
        small = _matmul_small(u, w_rows, i * n_in, n_small_win, tm_mm)
        big = _matmul_colscale(u, w_rows, i * n_in + n_small, 5 * w_grp, colscale, _bf16,
                               tm_proj, tn_big)
        q_cat = _q_proj(small, g_q[i], wq, i, cos, sin, lp, tm_proj, ql)
        k_nope, v_mla, k_rope = _kv_proj(small, g_kv[i], wk, wv, i, cos, sin, lp, tm_proj, ql, kvl)
        y_mla = _mla_attention(q_cat.reshape(b, lp, -1), k_nope.reshape(b, lp, -1),
                               k_rope.reshape(b, lp, -1), v_mla.reshape(b, lp, -1), heads)
        y_sb = _sb_attention(big.reshape(b, lp, -1), heads, qs_col, ks_col, vs_col)
        y = _gate(y_mla.reshape(mp, w_grp), y_sb.reshape(mp, w_grp), big,
                  g_out_mla[i], g_out_sb[i], zm_col, zs_col, tm_norm)
        h = _matmul_residual(y, wo, i, h, tm_mm, tn_out)

    return _final_norm(h.reshape(b, lp, d), g_final, seq, x.dtype, _pick_tile(seq, 512))


def kernel(x, meta_tokens, g_norm, w_in, g_q, g_kv, w_uq, w_ukv, g_out_mla, g_out_sb, w_o, g_final):
    return _forward(x, meta_tokens, g_norm, w_in, g_q, g_kv, w_uq, w_ukv, g_out_mla, g_out_sb,
                    w_o, g_final)
```

```python
import functools
import math

import jax
import jax.numpy as jnp
from jax import lax
from jax.experimental import pallas as pl
from jax.experimental.pallas import tpu as pltpu

N_META = 16
NOPE = 128
ROPE = 64
VDIM = 128
QK_CAT = 256
ROPE_THETA = 10000.0
EPS = 1e-6
LOG2E = math.log2(math.e)
MLA_SCALE2 = LOG2E / math.sqrt(NOPE + ROPE)
SB_SCALE2 = LOG2E / math.sqrt(VDIM)
LANE = 128
F32_SUBLANE = 8
BF16_SUBLANE = 16
ATT_BLK = 256
Q_BLK = 2 * ATT_BLK
MLA_HEADS_PER_STEP = 4
SB_HEADS_PER_STEP = 4
VMEM_LIMIT = 56 * 1024 * 1024
NEG_BIG = -1e30
SB_DEAD_LOG2 = -160.0

_f32 = jnp.float32
_bf16 = jnp.bfloat16


def _pick_tile(n, target):
    best = None
    for t in range(BF16_SUBLANE, min(n, target) + 1, BF16_SUBLANE):
        if n % t == 0:
            best = t
    assert best is not None, (n, target)
    return best


def _params(*sem):
    return pltpu.CompilerParams(dimension_semantics=sem, vmem_limit_bytes=VMEM_LIMIT)


def _rms(xf, g):
    return xf * lax.rsqrt(jnp.mean(xf * xf, axis=-1, keepdims=True) + EPS) * g


def _rmsnorm_kernel(x_ref, g_ref, o_ref):
    o_ref[...] = _rms(x_ref[...].astype(_f32), g_ref[...]).astype(o_ref.dtype)


def _rmsnorm(x, g, out_dtype, tm):
    m, d = x.shape
    return pl.pallas_call(
        _rmsnorm_kernel,
        grid=(m // tm,),
        in_specs=[pl.BlockSpec((tm, d), lambda i: (i, 0)),
                  pl.BlockSpec((1, d), lambda i: (0, 0))],
        out_specs=pl.BlockSpec((tm, d), lambda i: (i, 0)),
        out_shape=jax.ShapeDtypeStruct((m, d), out_dtype),
        compiler_params=_params("parallel"),
        name="rmsnorm",
    )(x, g.reshape(1, d))


def _final_norm(h, g, seq, out_dtype, tm):
    b, lp, d = h.shape
    sub = F32_SUBLANE
    assert N_META % sub == 0 and lp % sub == 0 and tm % sub == 0

    return pl.pallas_call(
        _rmsnorm_kernel,
        grid=(b, seq // tm),
        in_specs=[pl.BlockSpec((pl.Element(tm), pl.Element(d)),
                               lambda bi, i: (pl.multiple_of(bi * lp + N_META + i * tm, sub), 0)),
                  pl.BlockSpec((1, d), lambda bi, i: (0, 0))],
        out_specs=pl.BlockSpec((None, tm, d), lambda bi, i: (bi, i, 0)),
        out_shape=jax.ShapeDtypeStruct((b, seq, d), out_dtype),
        compiler_params=_params("parallel", "parallel"),
        name="final_norm",
    )(h.reshape(b * lp, d), g.reshape(1, d))


def _mm_scale_kernel(x_ref, w_ref, s_ref, o_ref):
    acc = lax.dot_general(x_ref[...], w_ref[...], (((1,), (1,)), ((), ())),
                          preferred_element_type=_f32)
    o_ref[...] = (acc * s_ref[...]).astype(o_ref.dtype)


def _matmul_colscale(x, w_rows, row0, n, colscale, out_dtype, tm, tn):
    m, k = x.shape
    assert row0 % BF16_SUBLANE == 0 and tn % BF16_SUBLANE == 0
    return pl.pallas_call(
        _mm_scale_kernel,
        grid=(n // tn, m // tm),
        in_specs=[pl.BlockSpec((tm, k), lambda j, i: (i, 0)),
                  pl.BlockSpec((pl.Element(tn), pl.Element(k)),
                               lambda j, i: (pl.multiple_of(row0 + j * tn, BF16_SUBLANE), 0)),
                  pl.BlockSpec((1, tn), lambda j, i: (0, j))],
        out_specs=pl.BlockSpec((tm, tn), lambda j, i: (i, j)),
        out_shape=jax.ShapeDtypeStruct((m, n), out_dtype),
        compiler_params=_params("parallel", "parallel"),
        name="matmul_colscale",
    )(x, w_rows, colscale)


def _mm_kernel(x_ref, w_ref, o_ref):
    o_ref[...] = lax.dot_general(x_ref[...], w_ref[...], (((1,), (1,)), ((), ())),
                                 preferred_element_type=_f32).astype(o_ref.dtype)


def _matmul_small(x, w_rows, row0, n, tm):
    m, k = x.shape
    assert row0 % BF16_SUBLANE == 0
    return pl.pallas_call(
        _mm_kernel,
        grid=(m // tm,),
        in_specs=[pl.BlockSpec((tm, k), lambda i: (i, 0)),
                  pl.BlockSpec((pl.Element(n), pl.Element(k)), lambda i: (row0, 0))],
        out_specs=pl.BlockSpec((tm, n), lambda i: (i, 0)),
        out_shape=jax.ShapeDtypeStruct((m, n), _f32),
        compiler_params=_params("parallel"),
        name="matmul_small",
    )(x, w_rows)


def _mm_res_kernel(x_ref, w_ref, r_ref, o_ref):
    o_ref[...] = r_ref[...] + jnp.dot(x_ref[...], w_ref[...], preferred_element_type=_f32)


def _matmul_residual(x, w, layer, res, tm, tn):
    m, k = x.shape
    n = w.shape[2]
    return pl.pallas_call(
        _mm_res_kernel,
        grid=(n // tn, m // tm),
        in_specs=[pl.BlockSpec((tm, k), lambda j, i: (i, 0)),
                  pl.BlockSpec((None, k, tn), lambda j, i: (layer, 0, j)),
                  pl.BlockSpec((tm, tn), lambda j, i: (i, j))],
        out_specs=pl.BlockSpec((tm, tn), lambda j, i: (i, j)),
        out_shape=jax.ShapeDtypeStruct((m, n), _f32),
        compiler_params=_params("parallel", "parallel"),
        name="matmul_residual",
    )(x, w, res)


def _rope128(x, cos, sin_signed):
    lane = lax.broadcasted_iota(jnp.int32, x.shape, 1)
    partner = jnp.where(lane < ROPE // 2,
                        pltpu.roll(x, LANE - ROPE // 2, 1),
                        pltpu.roll(x, ROPE // 2, 1))
    return x * cos + partner * sin_signed


def _qproj_kernel(x_ref, g_ref, w_ref, cos_ref, sin_ref, o_ref, xn_ref, *, heads_per_step):
    @pl.when(pl.program_id(1) == 0)
    def _():
        xn_ref[...] = (_rms(x_ref[...], g_ref[...]) * MLA_SCALE2).astype(_bf16)

    acc = jnp.dot(xn_ref[...], w_ref[...], preferred_element_type=_f32)
    cos = cos_ref[...]
    sin = sin_ref[...]
    for hh in range(heads_per_step):
        c0 = hh * QK_CAT
        o_ref[:, c0:c0 + NOPE] = acc[:, c0:c0 + NOPE].astype(o_ref.dtype)
        roped = _rope128(acc[:, c0 + NOPE:c0 + QK_CAT], cos, sin)
        o_ref[:, c0 + NOPE:c0 + QK_CAT] = roped.astype(o_ref.dtype)


def _q_proj(small, g_q, w_q, layer, cos, sin, lp, tm, ql):
    m = small.shape[0]
    n = w_q.shape[2]
    hps = 4 if (n // QK_CAT) % 4 == 0 else 1
    tn = hps * QK_CAT
    nrow = lp // tm
    return pl.pallas_call(
        functools.partial(_qproj_kernel, heads_per_step=hps),
        grid=(m // tm, n // tn),
        in_specs=[pl.BlockSpec((tm, ql), lambda i, j: (i, 0)),
                  pl.BlockSpec((1, ql), lambda i, j: (0, 0)),
                  pl.BlockSpec((None, ql, tn), lambda i, j: (layer, 0, j)),
                  pl.BlockSpec((tm, LANE), lambda i, j: (i % nrow, 0)),
                  pl.BlockSpec((tm, LANE), lambda i, j: (i % nrow, 0))],
        out_specs=pl.BlockSpec((tm, tn), lambda i, j: (i, j)),
        out_shape=jax.ShapeDtypeStruct((m, n), _bf16),
        scratch_shapes=[pltpu.VMEM((tm, ql), _bf16)],
        compiler_params=_params("parallel", "arbitrary"),
        name="q_proj",
    )(small, g_q.reshape(1, ql), w_q, cos, sin)


def _kvproj_kernel(x_ref, kr_ref, g_ref, wk_ref, wv_ref, cos_ref, sin_ref,
                   k_ref, v_ref, krr_ref, xn_ref):
    @pl.when(pl.program_id(1) == 0)
    def _():
        xn_ref[...] = _rms(x_ref[...], g_ref[...]).astype(_bf16)
        krr_ref[...] = _rope128(kr_ref[...], cos_ref[...], sin_ref[...]).astype(krr_ref.dtype)

    xn = xn_ref[...]
    k_ref[...] = jnp.dot(xn, wk_ref[...], preferred_element_type=_f32).astype(k_ref.dtype)
    v_ref[...] = jnp.dot(xn, wv_ref[...], preferred_element_type=_f32).astype(v_ref.dtype)


def _kv_proj(small, g_kv, w_k, w_v, layer, cos, sin, lp, tm, ql, kvl):
    m = small.shape[0]
    heads = w_k.shape[2] // NOPE
    hps = 4 if heads % 4 == 0 else 1
    nrow = lp // tm
    return pl.pallas_call(
        _kvproj_kernel,
        grid=(m // tm, heads // hps),
        in_specs=[pl.BlockSpec((tm, kvl), lambda i, j: (i, ql // kvl)),
                  pl.BlockSpec((tm, LANE), lambda i, j: (i, (ql + kvl) // LANE)),
                  pl.BlockSpec((1, kvl), lambda i, j: (0, 0)),
                  pl.BlockSpec((None, kvl, hps * NOPE), lambda i, j: (layer, 0, j)),
                  pl.BlockSpec((None, kvl, hps * VDIM), lambda i, j: (layer, 0, j)),
                  pl.BlockSpec((tm, LANE), lambda i, j: (i % nrow, 0)),
                  pl.BlockSpec((tm, LANE), lambda i, j: (i % nrow, 0))],
        out_specs=[pl.BlockSpec((tm, hps * NOPE), lambda i, j: (i, j)),
                   pl.BlockSpec((tm, hps * VDIM), lambda i, j: (i, j)),
                   pl.BlockSpec((tm, LANE), lambda i, j: (i, 0))],
        out_shape=[jax.ShapeDtypeStruct((m, heads * NOPE), _bf16),
                   jax.ShapeDtypeStruct((m, heads * VDIM), _bf16),
                   jax.ShapeDtypeStruct((m, LANE), _bf16)],
        scratch_shapes=[pltpu.VMEM((tm, kvl), _bf16)],
        compiler_params=_params("parallel", "arbitrary"),
        name="kv_proj",
    )(small, small, g_kv.reshape(1, kvl), w_k, w_v, cos, sin)


def _dot_nt(a, b):
    return lax.dot_general(a, b, (((1,), (1,)), ((), ())), preferred_element_type=_f32)


def _dot_tn(a, b):
    return lax.dot_general(a, b, (((0,), (0,)), ((), ())), preferred_element_type=_f32)


def _neg_abs(x):
    bits = lax.bitcast_convert_type(x, jnp.uint32) | jnp.uint32(0x80000000)
    return lax.bitcast_convert_type(bits, _f32)


def _row_blocks(lp):
    n_main = lp // Q_BLK
    tail = lp - n_main * Q_BLK
    assert tail <= ATT_BLK, (lp, tail)
    return n_main, tail


def _key_query_iota(bk, bq):
    key = lax.broadcasted_iota(jnp.int32, (bk, bq), 0)
    qry = lax.broadcasted_iota(jnp.int32, (bk, bq), 1)
    return key, qry


def _mla_kernel(q_ref, k_ref, kr_ref, v_ref, o_ref, s_ref, p_ref, al_ref, m_ref, l_ref, acc_ref,
                *, lp, hps):
    n_main, tail = _row_blocks(lp)

    def q_block(q0, bq, n_full, is_tail):
        qs = [q_ref[pl.ds(q0, bq), a * QK_CAT:(a + 1) * QK_CAT] for a in range(hps)]

        full = (0, bq)
        upper = (ATT_BLK, bq)

        def scores(a, k0, bk, lanes=full):
            k_cat = jnp.concatenate([k_ref[pl.ds(k0, bk), a * NOPE:(a + 1) * NOPE],
                                     kr_ref[pl.ds(k0, bk), :]], axis=1)
            return _dot_nt(k_cat, qs[a][lanes[0]:lanes[1]])

        def probs(a, s, slot, bk, mask_off, lanes=full):
            lo, hi = lanes
            if mask_off is not None:
                key, qry = _key_query_iota(bk, hi - lo)
                s = jnp.where(key + mask_off <= qry, s, NEG_BIG)
            m_prev = m_ref[a, :, lo:hi]
            m_new = jnp.maximum(m_prev, jnp.max(s, axis=0, keepdims=True))
            alpha = jnp.exp2(m_prev - m_new)
            p = jnp.exp2(s - m_new)
            l_ref[a, :, lo:hi] = alpha * l_ref[a, :, lo:hi] + jnp.sum(p, axis=0, keepdims=True)
            m_ref[a, :, lo:hi] = m_new
            p_ref[slot, a, 0:bk, lo:hi] = p.astype(_bf16)
            al_ref[slot, a, :, lo:hi] = alpha

        def values(a, slot, k0, bk, lanes=full):
            lo, hi = lanes
            pv = _dot_tn(v_ref[pl.ds(k0, bk), a * VDIM:(a + 1) * VDIM], p_ref[slot, a, 0:bk, lo:hi])
            acc_ref[a, :, lo:hi] = al_ref[slot, a, :, lo:hi] * acc_ref[a, :, lo:hi] + pv

        first = 0 if is_tail else 1
        last = n_full - 1 if is_tail else n_full + 1
        for a in range(hps):
            m_ref[a, :, 0:bq] = jnp.full((1, bq), NEG_BIG, _f32)
            l_ref[a, :, 0:bq] = jnp.zeros((1, bq), _f32)
            acc_ref[a, :, 0:bq] = jnp.zeros((VDIM, bq), _f32)
            p_ref[1 - first, a, :, 0:bq] = jnp.zeros((ATT_BLK, bq), _bf16)
            al_ref[1 - first, a, :, 0:bq] = jnp.ones((1, bq), _f32)
            s_ref[first, a, :, 0:bq] = scores(a, 0, ATT_BLK)

        def step(c, slot, mask_off=None, prefetch=full, lanes=full):
            nxt = pl.multiple_of(jnp.minimum(c + 1, last) * ATT_BLK, ATT_BLK)
            prv = pl.multiple_of(jnp.maximum(c - 1, 0) * ATT_BLK, ATT_BLK)
            for a in range(hps):
                if prefetch is not None:
                    s_ref[1 - slot, a, :, prefetch[0]:prefetch[1]] = scores(a, nxt, ATT_BLK, prefetch)
                probs(a, s_ref[slot, a, :, lanes[0]:lanes[1]], slot, ATT_BLK, mask_off, lanes)
                values(a, 1 - slot, prv, ATT_BLK)

        def pair(t, carry):
            step(2 * t, first)
            step(2 * t + 1, 1 - first)
            return carry

        lax.fori_loop(0, n_full // 2, pair, 0)
        prv = pl.multiple_of(jnp.maximum(n_full - 1, 0) * ATT_BLK, ATT_BLK)
        if is_tail:
            for a in range(hps):
                probs(a, scores(a, q0, bq), 0, bq, 0)
                values(a, 1, prv, ATT_BLK)
                values(a, 0, q0, bq)
        else:
            step(n_full, 1, mask_off=0, prefetch=upper)
            step(n_full + 1, 0, mask_off=0, prefetch=None, lanes=upper)
            k_last = pl.multiple_of((n_full + 1) * ATT_BLK, ATT_BLK)
            for a in range(hps):
                values(a, 0, k_last, ATT_BLK, upper)
        for a in range(hps):
            out_t = acc_ref[a, :, 0:bq] * (1.0 / l_ref[a, :, 0:bq])
            o_ref[pl.ds(q0, bq), a * VDIM:(a + 1) * VDIM] = out_t.T.astype(o_ref.dtype)

    def main_block(i, carry):
        q_block(pl.multiple_of(i * Q_BLK, Q_BLK), Q_BLK, 2 * i, False)
        return carry

    lax.fori_loop(0, n_main, main_block, 0)
    if tail:
        q_block(n_main * Q_BLK, tail, n_main * (Q_BLK // ATT_BLK), True)


def _sb_kernel(q_ref, k_ref, v_ref, o_ref, s_ref, w_ref, r_ref, acc_ref, *, lp, hps):
    n_main, tail = _row_blocks(lp)
    key_i, key_j = _key_query_iota(ATT_BLK, ATT_BLK)
    later = (key_j > key_i).astype(_bf16)

    def q_block(q0, bq, n_full, is_tail):
        qs = [q_ref[pl.ds(q0, bq), a * VDIM:(a + 1) * VDIM] for a in range(hps)]

        def scores(a, k0, bk, lanes=None):
            q = qs[a] if lanes is None else qs[a][lanes[0]:lanes[1]]
            return _dot_nt(k_ref[pl.ds(k0, bk), a * VDIM:(a + 1) * VDIM], q)

        full = (0, bq)
        upper = (ATT_BLK, bq)

        def weights(a, z_of, slot, bk, mask_off, lanes=full):
            for lo in range(lanes[0], lanes[1], ATT_BLK):
                hi = min(lo + ATT_BLK, lanes[1])
                z = z_of(lo, hi)
                tail_sp = jnp.log(1.0 + jnp.exp2(_neg_abs(z))) * LOG2E
                log_beta = jnp.minimum(z, 0.0) - tail_sp
                log_rest = log_beta - z
                masked = mask_off is not None
                if masked:
                    key, qry = _key_query_iota(bk, hi - lo)
                    strict = key + (mask_off - lo) < qry
                    log_rest = jnp.where(strict, log_rest, 0.0)
                within = jnp.dot(later[0:bk, 0:bk], log_rest.astype(_bf16),
                                 preferred_element_type=_f32)
                r = r_ref[a, :, lo:hi]
                w = jnp.exp2(log_beta + within + r)
                if masked:
                    w = jnp.where(strict, w, 0.0)
                w_ref[slot, a, 0:bk, lo:hi] = w.astype(_bf16)
                r_ref[a, :, lo:hi] = r + within[0:1, :] + log_rest[0:1, :]

        def from_value(z, lane0=0):
            return lambda lo, hi: z[:, lo - lane0:hi - lane0]

        def values(a, slot, k0, bk, lanes=full):
            lo, hi = lanes
            acc_ref[a, :, lo:hi] = acc_ref[a, :, lo:hi] + _dot_tn(
                v_ref[pl.ds(k0, bk), a * VDIM:(a + 1) * VDIM], w_ref[slot, a, 0:bk, lo:hi])

        v_last = n_full - 1 if is_tail else n_full + 1

        def step(c, slot, mask_off=None, v_lanes=full):
            k_before = pl.multiple_of(jnp.maximum(c - 1, 0) * ATT_BLK, ATT_BLK)
            k_after = pl.multiple_of(jnp.minimum(c + 1, v_last) * ATT_BLK, ATT_BLK)
            for a in range(hps):
                s_ref[1 - slot, a, :, 0:bq] = scores(a, k_before, ATT_BLK)
                weights(a, lambda lo, hi: s_ref[slot, a, :, lo:hi], slot, ATT_BLK, mask_off)
                values(a, 1 - slot, k_after, ATT_BLK, v_lanes)

        k_top = pl.multiple_of(jnp.maximum(n_full - 1, 0) * ATT_BLK, ATT_BLK)
        for a in range(hps):
            r_ref[a, :, 0:bq] = jnp.zeros((1, bq), _f32)
            acc_ref[a, :, 0:bq] = jnp.zeros((VDIM, bq), _f32)
            if is_tail:
                s_ref[1, a, :, 0:bq] = scores(a, k_top, ATT_BLK)
                weights(a, from_value(scores(a, q0, bq)), 0, bq, 0)
                values(a, 0, q0, bq)
                w_ref[0, a, :, 0:bq] = jnp.zeros((ATT_BLK, bq), _bf16)
            else:
                k_hi = pl.multiple_of((n_full + 1) * ATT_BLK, ATT_BLK)
                s_ref[0, a, :, 0:bq] = scores(a, q0, ATT_BLK)
                weights(a, from_value(scores(a, k_hi, ATT_BLK, upper), ATT_BLK), 1, ATT_BLK,
                        ATT_BLK, upper)
        if not is_tail:
            step(n_full, 0, mask_off=0, v_lanes=upper)

        n_pairs = n_full // 2

        def alive():
            r_max = r_ref[0, :, 0:bq]
            for a in range(1, hps):
                r_max = jnp.maximum(r_max, r_ref[a, :, 0:bq])
            return (jnp.max(r_max) > SB_DEAD_LOG2).astype(jnp.int32)

        def pair(carry):
            tt, _, _ = carry
            c = 2 * (n_pairs - 1 - tt) + 1
            step(c, 1)
            go_on = alive()

            @pl.when(go_on == 1)
            def _():
                step(c - 1, 0)

            return tt + 1, c - go_on, jnp.where(go_on == 1, alive(), 0)

        _, c_fin, _ = lax.while_loop(
            lambda carry: jnp.logical_and(carry[0] < n_pairs, carry[2] == 1), pair,
            (jnp.int32(0), jnp.int32(0) + n_full, jnp.int32(1)))
        k_fin = pl.multiple_of(jnp.minimum(c_fin, v_last) * ATT_BLK, ATT_BLK)
        for a in range(hps):
            values(a, c_fin % 2, k_fin, ATT_BLK)
            o_ref[pl.ds(q0, bq), a * VDIM:(a + 1) * VDIM] = acc_ref[a, :, 0:bq].T.astype(o_ref.dtype)

    def main_block(i, carry):
        q_block(pl.multiple_of(i * Q_BLK, Q_BLK), Q_BLK, 2 * i, False)
        return carry

    lax.fori_loop(0, n_main, main_block, 0)
    if tail:
        q_block(n_main * Q_BLK, tail, n_main * (Q_BLK // ATT_BLK), True)


def _att_scratch(hps, slot_stats, n_stat):
    return ([pltpu.VMEM((2, hps, ATT_BLK, Q_BLK), _f32),
             pltpu.VMEM((2, hps, ATT_BLK, Q_BLK), _bf16)]
            + [pltpu.VMEM((2, hps, 1, Q_BLK), _f32)] * slot_stats
            + [pltpu.VMEM((hps, 1, Q_BLK), _f32)] * n_stat
            + [pltpu.VMEM((hps, VDIM, Q_BLK), _f32)])


def _mla_attention(q_cat, k_nope, k_rope, v, heads):
    b, lp, _ = q_cat.shape
    hps = MLA_HEADS_PER_STEP if heads % MLA_HEADS_PER_STEP == 0 else 1
    q_spec = pl.BlockSpec((None, lp, hps * QK_CAT), lambda bi, h: (bi, 0, h))
    kr_spec = pl.BlockSpec((None, lp, LANE), lambda bi, h: (bi, 0, 0))
    v_spec = pl.BlockSpec((None, lp, hps * VDIM), lambda bi, h: (bi, 0, h))
    return pl.pallas_call(
        functools.partial(_mla_kernel, lp=lp, hps=hps),
        grid=(b, heads // hps),
        in_specs=[q_spec, v_spec, kr_spec, v_spec],
        out_specs=v_spec,
        out_shape=jax.ShapeDtypeStruct((b, lp, heads * VDIM), _bf16),
        scratch_shapes=_att_scratch(hps, 1, 2),
        compiler_params=_params("parallel", "parallel"),
        name="mla_attention",
    )(q_cat, k_nope, k_rope, v)


def _sb_attention(big, heads, q_col, k_col, v_col):
    b, lp, _ = big.shape
    hps = SB_HEADS_PER_STEP if heads % SB_HEADS_PER_STEP == 0 else 1
    w = hps * VDIM

    def spec(col0):
        return pl.BlockSpec((None, lp, w), lambda bi, h: (bi, 0, col0 // w + h))

    return pl.pallas_call(
        functools.partial(_sb_kernel, lp=lp, hps=hps),
        grid=(b, heads // hps),
        in_specs=[spec(q_col), spec(k_col), spec(v_col)],
        out_specs=pl.BlockSpec((None, lp, w), lambda bi, h: (bi, 0, h)),
        out_shape=jax.ShapeDtypeStruct((b, lp, heads * VDIM), _bf16),
        scratch_shapes=_att_scratch(hps, 0, 1),
        compiler_params=_params("parallel", "parallel"),
        name="sb_attention",
    )(big, big, big)


def _gate_kernel(ym_ref, ys_ref, zm_ref, zs_ref, gm_ref, gs_ref, o_ref, *, w):
    rows = BF16_SUBLANE

    def one(y_ref, z_ref, g_ref, r0):
        z = z_ref[pl.ds(r0, rows), :].astype(_f32)
        silu = z / (1.0 + jnp.exp(-z))
        return (_rms(y_ref[pl.ds(r0, rows), :].astype(_f32), g_ref[...]) * silu).astype(o_ref.dtype)

    def group(i, carry):
        r0 = pl.multiple_of(i * rows, rows)
        o_ref[pl.ds(r0, rows), 0:w] = one(ym_ref, zm_ref, gm_ref, r0)
        o_ref[pl.ds(r0, rows), w:2 * w] = one(ys_ref, zs_ref, gs_ref, r0)
        return carry

    lax.fori_loop(0, o_ref.shape[0] // rows, group, 0, unroll=2)


def _gate(y_mla, y_sb, big, g_mla, g_sb, zm_col, zs_col, tm):
    m, w = y_mla.shape
    yspec = pl.BlockSpec((tm, w), lambda i: (i, 0))
    gspec = pl.BlockSpec((1, w), lambda i: (0, 0))
    return pl.pallas_call(
        functools.partial(_gate_kernel, w=w),
        grid=(m // tm,),
        in_specs=[yspec, yspec,
                  pl.BlockSpec((tm, w), lambda i: (i, zm_col // w)),
                  pl.BlockSpec((tm, w), lambda i: (i, zs_col // w)),
                  gspec, gspec],
        out_specs=pl.BlockSpec((tm, 2 * w), lambda i: (i, 0)),
        out_shape=jax.ShapeDtypeStruct((m, 2 * w), _bf16),
        compiler_params=_params("parallel"),
        name="gate",
    )(y_mla, y_sb, big, big, g_mla.reshape(1, w), g_sb.reshape(1, w))


def _rope_tables(lp):
    inv_freq = ROPE_THETA ** (-jnp.arange(0, ROPE, 2, dtype=_f32) / ROPE)
    ang = jnp.arange(lp, dtype=jnp.int32).astype(_f32)[:, None] * inv_freq[None, :]
    cos, sin = jnp.cos(ang), jnp.sin(ang)
    zeros = jnp.zeros((lp, LANE - ROPE), _f32)
    return (jnp.concatenate([cos, cos, zeros], axis=1),
            jnp.concatenate([-sin, sin, zeros], axis=1))


@jax.jit
def _forward(x, meta_tokens, g_norm, w_in, g_q, g_kv, w_uq, w_ukv, g_out_mla, g_out_sb,
             w_o, g_final):
    b, seq, d = x.shape
    depth = w_in.shape[0]
    ql, kvl = g_q.shape[1], g_kv.shape[1]
    heads = w_uq.shape[2] // (NOPE + ROPE)
    w_grp = heads * VDIM
    assert w_ukv.shape[2] == heads * (NOPE + VDIM)
    assert w_in.shape[2] == ql + kvl + ROPE + 5 * w_grp
    assert ql % kvl == 0 and (ql + kvl) % LANE == 0 and kvl % LANE == 0

    l_real = N_META + seq
    lp = -(-l_real // LANE) * LANE
    mp = b * lp

    n_small = ql + kvl + ROPE
    n_in = w_in.shape[2]
    w_rows = jnp.swapaxes(w_in, 1, 2).astype(_bf16).reshape(depth * n_in, d)
    n_small_win = ql + kvl + LANE
    zm_col, qs_col, ks_col, vs_col, zs_col = (i * w_grp for i in range(5))
    colscale = jnp.ones((1, 5 * w_grp), _f32).at[:, qs_col:qs_col + w_grp].set(SB_SCALE2)
    wq = w_uq.reshape(depth, ql, heads, NOPE + ROPE)
    wq = jnp.pad(wq, ((0, 0), (0, 0), (0, 0), (0, QK_CAT - NOPE - ROPE)))
    wq = wq.reshape(depth, ql, heads * QK_CAT).astype(_bf16)
    wkv = w_ukv.reshape(depth, kvl, heads, NOPE + VDIM)
    wk = wkv[..., :NOPE].reshape(depth, kvl, heads * NOPE).astype(_bf16)
    wv = wkv[..., NOPE:].reshape(depth, kvl, heads * VDIM).astype(_bf16)
    wo = w_o.astype(_bf16)
    cos, sin = _rope_tables(lp)

    meta = jnp.broadcast_to(meta_tokens[None].astype(x.dtype), (b, N_META, d))
    h = jnp.concatenate([meta, x, jnp.zeros((b, lp - l_real, d), x.dtype)], axis=1)
    h = h.reshape(mp, d)

    tm_norm = _pick_tile(lp, 384)
    tm_mm = _pick_tile(lp, 528)
    tm_proj = _pick_tile(lp, 1056)
    tn_big = 1024 if (5 * w_grp) % 1024 == 0 else 512
    tn_out = 1024 if d % 1024 == 0 else 512

    for i in range(depth):
        u = _rmsnorm(h, g_norm[i], _bf16, tm_norm)
        small = _matmul_small(u, w_rows, i * n_in, n_small_win, tm_mm)
        big = _matmul_colscale(u, w_rows, i * n_in + n_small, 5 * w_grp, colscale, _bf16,
                               tm_proj, tn_big)
        q_cat = _q_proj(small, g_q[i], wq, i, cos, sin, lp, tm_proj, ql)
        k_nope, v_mla, k_rope = _kv_proj(small, g_kv[i], wk, wv, i, cos, sin, lp, tm_proj, ql, kvl)
        y_mla = _mla_attention(q_cat.reshape(b, lp, -1), k_nope.reshape(b, lp, -1),
                               k_rope.reshape(b, lp, -1), v_mla.reshape(b, lp, -1), heads)
        y_sb = _sb_attention(big.reshape(b, lp, -1), heads, qs_col, ks_col, vs_col)
        y = _gate(y_mla.reshape(mp, w_grp), y_sb.reshape(mp, w_grp), big,
                  g_out_mla[i], g_out_sb[i], zm_col, zs_col, tm_norm)
        h = _matmul_residual(y, wo, i, h, tm_mm, tn_out)

    return _final_norm(h.reshape(b, lp, d), g_final, seq, x.dtype, _pick_tile(seq, 512))


def kernel(x, meta_tokens, g_norm, w_in, g_q, g_kv, w_uq, w_ukv, g_out_mla, g_out_sb, w_o, g_final):
    return _forward(x, meta_tokens, g_norm, w_in, g_q, g_kv, w_uq, w_ukv, g_out_mla, g_out_sb,
                    w_o, g_final)
```

```python
import functools
import math

import jax
import jax.numpy as jnp
from jax import lax
from jax.experimental import pallas as pl
from jax.experimental.pallas import tpu as pltpu

N_META = 16
NOPE = 128
ROPE = 64
VDIM = 128
QK_CAT = 256
ROPE_THETA = 10000.0
EPS = 1e-6
LOG2E = math.log2(math.e)
MLA_SCALE2 = LOG2E / math.sqrt(NOPE + ROPE)
SB_SCALE2 = LOG2E / math.sqrt(VDIM)
LANE = 128
F32_SUBLANE = 8
BF16_SUBLANE = 16
ATT_BLK = 256
Q_BLK = 2 * ATT_BLK
MLA_HEADS_PER_STEP = 4
SB_HEADS_PER_STEP = 4
VMEM_LIMIT = 56 * 1024 * 1024
NEG_BIG = -1e30
SB_DEAD_LOG2 = -160.0

_f32 = jnp.float32
_bf16 = jnp.bfloat16


def _pick_tile(n, target):
    best = None
    for t in range(BF16_SUBLANE, min(n, target) + 1, BF16_SUBLANE):
        if n % t == 0:
            best = t
    assert best is not None, (n, target)
    return best


def _params(*sem):
    return pltpu.CompilerParams(dimension_semantics=sem, vmem_limit_bytes=VMEM_LIMIT)


def _rms(xf, g):
    return xf * lax.rsqrt(jnp.mean(xf * xf, axis=-1, keepdims=True) + EPS) * g


def _rmsnorm_kernel(x_ref, g_ref, o_ref):
    o_ref[...] = _rms(x_ref[...].astype(_f32), g_ref[...]).astype(o_ref.dtype)


def _rmsnorm(x, g, out_dtype, tm):
    m, d = x.shape
    return pl.pallas_call(
        _rmsnorm_kernel,
        grid=(m // tm,),
        in_specs=[pl.BlockSpec((tm, d), lambda i: (i, 0)),
                  pl.BlockSpec((1, d), lambda i: (0, 0))],
        out_specs=pl.BlockSpec((tm, d), lambda i: (i, 0)),
        out_shape=jax.ShapeDtypeStruct((m, d), out_dtype),
        compiler_params=_params("parallel"),
        name="rmsnorm",
    )(x, g.reshape(1, d))


def _final_norm(h, g, seq, out_dtype, tm):
    b, lp, d = h.shape
    sub = F32_SUBLANE
    assert N_META % sub == 0 and lp % sub == 0 and tm % sub == 0

    return pl.pallas_call(
        _rmsnorm_kernel,
        grid=(b, seq // tm),
        in_specs=[pl.BlockSpec((pl.Element(tm), pl.Element(d)),
                               lambda bi, i: (pl.multiple_of(bi * lp + N_META + i * tm, sub), 0)),
                  pl.BlockSpec((1, d), lambda bi, i: (0, 0))],
        out_specs=pl.BlockSpec((None, tm, d), lambda bi, i: (bi, i, 0)),
        out_shape=jax.ShapeDtypeStruct((b, seq, d), out_dtype),
        compiler_params=_params("parallel", "parallel"),
        name="final_norm",
    )(h.reshape(b * lp, d), g.reshape(1, d))


def _mm_scale_kernel(x_ref, w_ref, s_ref, o_ref):
    acc = lax.dot_general(x_ref[...], w_ref[...], (((1,), (1,)), ((), ())),
                          preferred_element_type=_f32)
    o_ref[...] = (acc * s_ref[...]).astype(o_ref.dtype)


def _matmul_colscale(x, w_rows, row0, n, colscale, out_dtype, tm, tn):
    m, k = x.shape
    assert row0 % BF16_SUBLANE == 0 and tn % BF16_SUBLANE == 0
    return pl.pallas_call(
        _mm_scale_kernel,
        grid=(n // tn, m // tm),
        in_specs=[pl.BlockSpec((tm, k), lambda j, i: (i, 0)),
                  pl.BlockSpec((pl.Element(tn), pl.Element(k)),
                               lambda j, i: (pl.multiple_of(row0 + j * tn, BF16_SUBLANE), 0)),
                  pl.BlockSpec((1, tn), lambda j, i: (0, j))],
        out_specs=pl.BlockSpec((tm, tn), lambda j, i: (i, j)),
        out_shape=jax.ShapeDtypeStruct((m, n), out_dtype),
        compiler_params=_params("parallel", "parallel"),
        name="matmul_colscale",
    )(x, w_rows, colscale)


def _mm_kernel(x_ref, w_ref, o_ref):
    o_ref[...] = lax.dot_general(x_ref[...], w_ref[...], (((1,), (1,)), ((), ())),
                                 preferred_element_type=_f32).astype(o_ref.dtype)


def _matmul_small(x, w_rows, row0, n, tm):
    m, k = x.shape
    assert row0 % BF16_SUBLANE == 0
    return pl.pallas_call(
        _mm_kernel,
        grid=(m // tm,),
        in_specs=[pl.BlockSpec((tm, k), lambda i: (i, 0)),
                  pl.BlockSpec((pl.Element(n), pl.Element(k)), lambda i: (row0, 0))],
        out_specs=pl.BlockSpec((tm, n), lambda i: (i, 0)),
        out_shape=jax.ShapeDtypeStruct((m, n), _f32),
        compiler_params=_params("parallel"),
        name="matmul_small",
    )(x, w_rows)


def _mm_res_kernel(x_ref, w_ref, r_ref, o_ref):
    o_ref[...] = r_ref[...] + jnp.dot(x_ref[...], w_ref[...], preferred_element_type=_f32)


def _matmul_residual(x, w, layer, res, tm, tn):
    m, k = x.shape
    n = w.shape[2]
    return pl.pallas_call(
        _mm_res_kernel,
        grid=(n // tn, m // tm),
        in_specs=[pl.BlockSpec((tm, k), lambda j, i: (i, 0)),
                  pl.BlockSpec((None, k, tn), lambda j, i: (layer, 0, j)),
                  pl.BlockSpec((tm, tn), lambda j, i: (i, j))],
        out_specs=pl.BlockSpec((tm, tn), lambda j, i: (i, j)),
        out_shape=jax.ShapeDtypeStruct((m, n), _f32),
        compiler_params=_params("parallel", "parallel"),
        name="matmul_residual",
    )(x, w, res)


def _rope128(x, cos, sin_signed):
    lane = lax.broadcasted_iota(jnp.int32, x.shape, 1)
    partner = jnp.where(lane < ROPE // 2,
                        pltpu.roll(x, LANE - ROPE // 2, 1),
                        pltpu.roll(x, ROPE // 2, 1))
    return x * cos + partner * sin_signed


def _qproj_kernel(x_ref, g_ref, w_ref, cos_ref, sin_ref, o_ref, xn_ref, *, heads_per_step):
    @pl.when(pl.program_id(1) == 0)
    def _():
        xn_ref[...] = (_rms(x_ref[...], g_ref[...]) * MLA_SCALE2).astype(_bf16)

    acc = jnp.dot(xn_ref[...], w_ref[...], preferred_element_type=_f32)
    cos = cos_ref[...]
    sin = sin_ref[...]
    for hh in range(heads_per_step):
        c0 = hh * QK_CAT
        o_ref[:, c0:c0 + NOPE] = acc[:, c0:c0 + NOPE].astype(o_ref.dtype)
        roped = _rope128(acc[:, c0 + NOPE:c0 + QK_CAT], cos, sin)
        o_ref[:, c0 + NOPE:c0 + QK_CAT] = roped.astype(o_ref.dtype)


def _q_proj(small, g_q, w_q, layer, cos, sin, lp, tm, ql):
    m = small.shape[0]
    n = w_q.shape[2]
    hps = 4 if (n // QK_CAT) % 4 == 0 else 1
    tn = hps * QK_CAT
    nrow = lp // tm
    return pl.pallas_call(
        functools.partial(_qproj_kernel, heads_per_step=hps),
        grid=(m // tm, n // tn),
        in_specs=[pl.BlockSpec((tm, ql), lambda i, j: (i, 0)),
                  pl.BlockSpec((1, ql), lambda i, j: (0, 0)),
                  pl.BlockSpec((None, ql, tn), lambda i, j: (layer, 0, j)),
                  pl.BlockSpec((tm, LANE), lambda i, j: (i % nrow, 0)),
                  pl.BlockSpec((tm, LANE), lambda i, j: (i % nrow, 0))],
        out_specs=pl.BlockSpec((tm, tn), lambda i, j: (i, j)),
        out_shape=jax.ShapeDtypeStruct((m, n), _bf16),
        scratch_shapes=[pltpu.VMEM((tm, ql), _bf16)],
        compiler_params=_params("parallel", "arbitrary"),
        name="q_proj",
    )(small, g_q.reshape(1, ql), w_q, cos, sin)


def _kvproj_kernel(x_ref, kr_ref, g_ref, w_ref, cos_ref, sin_ref,
                   k_ref, v_ref, krr_ref, xn_ref, *, heads_per_step):
    @pl.when(pl.program_id(1) == 0)
    def _():
        xn_ref[...] = _rms(x_ref[...], g_ref[...]).astype(_bf16)
        krr_ref[...] = _rope128(kr_ref[...], cos_ref[...], sin_ref[...]).astype(krr_ref.dtype)

    kv = jnp.dot(xn_ref[...], w_ref[...], preferred_element_type=_f32)
    for hh in range(heads_per_step):
        c0 = hh * (NOPE + VDIM)
        k_ref[:, hh * NOPE:(hh + 1) * NOPE] = kv[:, c0:c0 + NOPE].astype(k_ref.dtype)
        v_ref[:, hh * VDIM:(hh + 1) * VDIM] = kv[:, c0 + NOPE:c0 + NOPE + VDIM].astype(v_ref.dtype)


def _kv_proj(small, g_kv, w_kv, layer, cos, sin, lp, tm, ql, kvl):
    m = small.shape[0]
    heads = w_kv.shape[2] // (NOPE + VDIM)
    hps = 4 if heads % 4 == 0 else 1
    nrow = lp // tm
    return pl.pallas_call(
        functools.partial(_kvproj_kernel, heads_per_step=hps),
        grid=(m // tm, heads // hps),
        in_specs=[pl.BlockSpec((tm, kvl), lambda i, j: (i, ql // kvl)),
                  pl.BlockSpec((tm, LANE), lambda i, j: (i, (ql + kvl) // LANE)),
                  pl.BlockSpec((1, kvl), lambda i, j: (0, 0)),
                  pl.BlockSpec((None, kvl, hps * (NOPE + VDIM)), lambda i, j: (layer, 0, j)),
                  pl.BlockSpec((tm, LANE), lambda i, j: (i % nrow, 0)),
                  pl.BlockSpec((tm, LANE), lambda i, j: (i % nrow, 0))],
        out_specs=[pl.BlockSpec((tm, hps * NOPE), lambda i, j: (i, j)),
                   pl.BlockSpec((tm, hps * VDIM), lambda i, j: (i, j)),
                   pl.BlockSpec((tm, LANE), lambda i, j: (i, 0))],
        out_shape=[jax.ShapeDtypeStruct((m, heads * NOPE), _bf16),
                   jax.ShapeDtypeStruct((m, heads * VDIM), _bf16),
                   jax.ShapeDtypeStruct((m, LANE), _bf16)],
        scratch_shapes=[pltpu.VMEM((tm, kvl), _bf16)],
        compiler_params=_params("parallel", "arbitrary"),
        name="kv_proj",
    )(small, small, g_kv.reshape(1, kvl), w_kv, cos, sin)


def _dot_nt(a, b):
    return lax.dot_general(a, b, (((1,), (1,)), ((), ())), preferred_element_type=_f32)


def _dot_tn(a, b):
    return lax.dot_general(a, b, (((0,), (0,)), ((), ())), preferred_element_type=_f32)


def _neg_abs(x):
    bits = lax.bitcast_convert_type(x, jnp.uint32) | jnp.uint32(0x80000000)
    return lax.bitcast_convert_type(bits, _f32)


def _row_blocks(lp):
    n_main = lp // Q_BLK
    tail = lp - n_main * Q_BLK
    assert tail <= ATT_BLK, (lp, tail)
    return n_main, tail


def _key_query_iota(bk, bq):
    key = lax.broadcasted_iota(jnp.int32, (bk, bq), 0)
    qry = lax.broadcasted_iota(jnp.int32, (bk, bq), 1)
    return key, qry


def _mla_kernel(q_ref, k_ref, kr_ref, v_ref, o_ref, s_ref, p_ref, al_ref, m_ref, l_ref, acc_ref,
                *, lp, hps):
    n_main, tail = _row_blocks(lp)

    def q_block(q0, bq, n_full, is_tail):
        qs = [q_ref[pl.ds(q0, bq), a * QK_CAT:(a + 1) * QK_CAT] for a in range(hps)]

        full = (0, bq)
        upper = (ATT_BLK, bq)

        def scores(a, k0, bk, lanes=full):
            k_cat = jnp.concatenate([k_ref[pl.ds(k0, bk), a * NOPE:(a + 1) * NOPE],
                                     kr_ref[pl.ds(k0, bk), :]], axis=1)
            return _dot_nt(k_cat, qs[a][lanes[0]:lanes[1]])

        def probs(a, s, slot, bk, mask_off, lanes=full):
            lo, hi = lanes
            if mask_off is not None:
                key, qry = _key_query_iota(bk, hi - lo)
                s = jnp.where(key + mask_off <= qry, s, NEG_BIG)
            m_prev = m_ref[a, :, lo:hi]
            m_new = jnp.maximum(m_prev, jnp.max(s, axis=0, keepdims=True))
            alpha = jnp.exp2(m_prev - m_new)
            p = jnp.exp2(s - m_new)
            l_ref[a, :, lo:hi] = alpha * l_ref[a, :, lo:hi] + jnp.sum(p, axis=0, keepdims=True)
            m_ref[a, :, lo:hi] = m_new
            p_ref[slot, a, 0:bk, lo:hi] = p.astype(_bf16)
            al_ref[slot, a, :, lo:hi] = alpha

        def values(a, slot, k0, bk, lanes=full):
            lo, hi = lanes
            pv = _dot_tn(v_ref[pl.ds(k0, bk), a * VDIM:(a + 1) * VDIM], p_ref[slot, a, 0:bk, lo:hi])
            acc_ref[a, :, lo:hi] = al_ref[slot, a, :, lo:hi] * acc_ref[a, :, lo:hi] + pv

        first = 0 if is_tail else 1
        last = n_full - 1 if is_tail else n_full + 1
        for a in range(hps):
            m_ref[a, :, 0:bq] = jnp.full((1, bq), NEG_BIG, _f32)
            l_ref[a, :, 0:bq] = jnp.zeros((1, bq), _f32)
            acc_ref[a, :, 0:bq] = jnp.zeros((VDIM, bq), _f32)
            p_ref[1 - first, a, :, 0:bq] = jnp.zeros((ATT_BLK, bq), _bf16)
            al_ref[1 - first, a, :, 0:bq] = jnp.ones((1, bq), _f32)
            s_ref[first, a, :, 0:bq] = scores(a, 0, ATT_BLK)

        def step(c, slot, mask_off=None, prefetch=full, lanes=full):
            nxt = pl.multiple_of(jnp.minimum(c + 1, last) * ATT_BLK, ATT_BLK)
            prv = pl.multiple_of(jnp.maximum(c - 1, 0) * ATT_BLK, ATT_BLK)
            for a in range(hps):
                if prefetch is not None:
                    s_ref[1 - slot, a, :, prefetch[0]:prefetch[1]] = scores(a, nxt, ATT_BLK, prefetch)
                probs(a, s_ref[slot, a, :, lanes[0]:lanes[1]], slot, ATT_BLK, mask_off, lanes)
                values(a, 1 - slot, prv, ATT_BLK)

        def pair(t, carry):
            step(2 * t, first)
            step(2 * t + 1, 1 - first)
            return carry

        lax.fori_loop(0, n_full // 2, pair, 0)
        prv = pl.multiple_of(jnp.maximum(n_full - 1, 0) * ATT_BLK, ATT_BLK)
        if is_tail:
            for a in range(hps):
                probs(a, scores(a, q0, bq), 0, bq, 0)
                values(a, 1, prv, ATT_BLK)
                values(a, 0, q0, bq)
        else:
            step(n_full, 1, mask_off=0, prefetch=upper)
            step(n_full + 1, 0, mask_off=0, prefetch=None, lanes=upper)
            k_last = pl.multiple_of((n_full + 1) * ATT_BLK, ATT_BLK)
            for a in range(hps):
                values(a, 0, k_last, ATT_BLK, upper)
        for a in range(hps):
            out_t = acc_ref[a, :, 0:bq] * (1.0 / l_ref[a, :, 0:bq])
            o_ref[pl.ds(q0, bq), a * VDIM:(a + 1) * VDIM] = out_t.T.astype(o_ref.dtype)

    def main_block(i, carry):
        q_block(pl.multiple_of(i * Q_BLK, Q_BLK), Q_BLK, 2 * i, False)
        return carry

    lax.fori_loop(0, n_main, main_block, 0)
    if tail:
        q_block(n_main * Q_BLK, tail, n_main * (Q_BLK // ATT_BLK), True)


def _sb_kernel(q_ref, k_ref, v_ref, o_ref, s_ref, w_ref, r_ref, acc_ref, *, lp, hps):
    n_main, tail = _row_blocks(lp)
    key_i, key_j = _key_query_iota(ATT_BLK, ATT_BLK)
    later = (key_j > key_i).astype(_bf16)

    def q_block(q0, bq, n_full, is_tail):
        qs = [q_ref[pl.ds(q0, bq), a * VDIM:(a + 1) * VDIM] for a in range(hps)]

        def scores(a, k0, bk, lanes=None):
            q = qs[a] if lanes is None else qs[a][lanes[0]:lanes[1]]
            return _dot_nt(k_ref[pl.ds(k0, bk), a * VDIM:(a + 1) * VDIM], q)

        full = (0, bq)
        upper = (ATT_BLK, bq)

        def weights(a, z_of, slot, bk, mask_off, lanes=full):
            for lo in range(lanes[0], lanes[1], ATT_BLK):
                hi = min(lo + ATT_BLK, lanes[1])
                z = z_of(lo, hi)
                tail_sp = jnp.log(1.0 + jnp.exp2(_neg_abs(z))) * LOG2E
                log_beta = jnp.minimum(z, 0.0) - tail_sp
                log_rest = log_beta - z
                masked = mask_off is not None
                if masked:
                    key, qry = _key_query_iota(bk, hi - lo)
                    strict = key + (mask_off - lo) < qry
                    log_rest = jnp.where(strict, log_rest, 0.0)
                within = jnp.dot(later[0:bk, 0:bk], log_rest.astype(_bf16),
                                 preferred_element_type=_f32)
                r = r_ref[a, :, lo:hi]
                w = jnp.exp2(log_beta + within + r)
                if masked:
                    w = jnp.where(strict, w, 0.0)
                w_ref[slot, a, 0:bk, lo:hi] = w.astype(_bf16)
                r_ref[a, :, lo:hi] = r + within[0:1, :] + log_rest[0:1, :]

        def from_value(z, lane0=0):
            return lambda lo, hi: z[:, lo - lane0:hi - lane0]

        def values(a, slot, k0, bk, lanes=full):
            lo, hi = lanes
            acc_ref[a, :, lo:hi] = acc_ref[a, :, lo:hi] + _dot_tn(
                v_ref[pl.ds(k0, bk), a * VDIM:(a + 1) * VDIM], w_ref[slot, a, 0:bk, lo:hi])

        v_last = n_full - 1 if is_tail else n_full + 1

        def step(c, slot, mask_off=None, v_lanes=full):
            k_before = pl.multiple_of(jnp.maximum(c - 1, 0) * ATT_BLK, ATT_BLK)
            k_after = pl.multiple_of(jnp.minimum(c + 1, v_last) * ATT_BLK, ATT_BLK)
            for a in range(hps):
                s_ref[1 - slot, a, :, 0:bq] = scores(a, k_before, ATT_BLK)
                weights(a, lambda lo, hi: s_ref[slot, a, :, lo:hi], slot, ATT_BLK, mask_off)
                values(a, 1 - slot, k_after, ATT_BLK, v_lanes)

        k_top = pl.multiple_of(jnp.maximum(n_full - 1, 0) * ATT_BLK, ATT_BLK)
        for a in range(hps):
            r_ref[a, :, 0:bq] = jnp.zeros((1, bq), _f32)
            acc_ref[a, :, 0:bq] = jnp.zeros((VDIM, bq), _f32)
            if is_tail:
                s_ref[1, a, :, 0:bq] = scores(a, k_top, ATT_BLK)
                weights(a, from_value(scores(a, q0, bq)), 0, bq, 0)
                values(a, 0, q0, bq)
                w_ref[0, a, :, 0:bq] = jnp.zeros((ATT_BLK, bq), _bf16)
            else:
                k_hi = pl.multiple_of((n_full + 1) * ATT_BLK, ATT_BLK)
                s_ref[0, a, :, 0:bq] = scores(a, q0, ATT_BLK)
                weights(a, from_value(scores(a, k_hi, ATT_BLK, upper), ATT_BLK), 1, ATT_BLK,
                        ATT_BLK, upper)
        if not is_tail:
            step(n_full, 0, mask_off=0, v_lanes=upper)

        n_pairs = n_full // 2

        def alive():
            r_max = r_ref[0, :, 0:bq]
            for a in range(1, hps):
                r_max = jnp.maximum(r_max, r_ref[a, :, 0:bq])
            return (jnp.max(r_max) > SB_DEAD_LOG2).astype(jnp.int32)

        def pair(carry):
            tt, _, _ = carry
            c = 2 * (n_pairs - 1 - tt) + 1
            step(c, 1)
            go_on = alive()

            @pl.when(go_on == 1)
            def _():
                step(c - 1, 0)

            return tt + 1, c - go_on, jnp.where(go_on == 1, alive(), 0)

        _, c_fin, _ = lax.while_loop(
            lambda carry: jnp.logical_and(carry[0] < n_pairs, carry[2] == 1), pair,
            (jnp.int32(0), jnp.int32(0) + n_full, jnp.int32(1)))
        k_fin = pl.multiple_of(jnp.minimum(c_fin, v_last) * ATT_BLK, ATT_BLK)
        for a in range(hps):
            values(a, c_fin % 2, k_fin, ATT_BLK)
            o_ref[pl.ds(q0, bq), a * VDIM:(a + 1) * VDIM] = acc_ref[a, :, 0:bq].T.astype(o_ref.dtype)

    def main_block(i, carry):
        q_block(pl.multiple_of(i * Q_BLK, Q_BLK), Q_BLK, 2 * i, False)
        return carry

    lax.fori_loop(0, n_main, main_block, 0)
    if tail:
        q_block(n_main * Q_BLK, tail, n_main * (Q_BLK // ATT_BLK), True)


def _att_scratch(hps, slot_stats, n_stat):
    return ([pltpu.VMEM((2, hps, ATT_BLK, Q_BLK), _f32),
             pltpu.VMEM((2, hps, ATT_BLK, Q_BLK), _bf16)]
            + [pltpu.VMEM((2, hps, 1, Q_BLK), _f32)] * slot_stats
            + [pltpu.VMEM((hps, 1, Q_BLK), _f32)] * n_stat
            + [pltpu.VMEM((hps, VDIM, Q_BLK), _f32)])


def _mla_attention(q_cat, k_nope, k_rope, v, heads):
    b, lp, _ = q_cat.shape
    hps = MLA_HEADS_PER_STEP if heads % MLA_HEADS_PER_STEP == 0 else 1
    q_spec = pl.BlockSpec((None, lp, hps * QK_CAT), lambda bi, h: (bi, 0, h))
    kr_spec = pl.BlockSpec((None, lp, LANE), lambda bi, h: (bi, 0, 0))
    v_spec = pl.BlockSpec((None, lp, hps * VDIM), lambda bi, h: (bi, 0, h))
    return pl.pallas_call(
        functools.partial(_mla_kernel, lp=lp, hps=hps),
        grid=(b, heads // hps),
        in_specs=[q_spec, v_spec, kr_spec, v_spec],
        out_specs=v_spec,
        out_shape=jax.ShapeDtypeStruct((b, lp, heads * VDIM), _bf16),
        scratch_shapes=_att_scratch(hps, 1, 2),
        compiler_params=_params("parallel", "parallel"),
        name="mla_attention",
    )(q_cat, k_nope, k_rope, v)


def _sb_attention(big, heads, q_col, k_col, v_col):
    b, lp, _ = big.shape
    hps = SB_HEADS_PER_STEP if heads % SB_HEADS_PER_STEP == 0 else 1
    w = hps * VDIM

    def spec(col0):
        return pl.BlockSpec((None, lp, w), lambda bi, h: (bi, 0, col0 // w + h))

    return pl.pallas_call(
        functools.partial(_sb_kernel, lp=lp, hps=hps),
        grid=(b, heads // hps),
        in_specs=[spec(q_col), spec(k_col), spec(v_col)],
        out_specs=pl.BlockSpec((None, lp, w), lambda bi, h: (bi, 0, h)),
        out_shape=jax.ShapeDtypeStruct((b, lp, heads * VDIM), _bf16),
        scratch_shapes=_att_scratch(hps, 0, 1),
        compiler_params=_params("parallel", "parallel"),
        name="sb_attention",
    )(big, big, big)


def _gate_kernel(ym_ref, ys_ref, zm_ref, zs_ref, gm_ref, gs_ref, o_ref, *, w):
    rows = BF16_SUBLANE

    def one(y_ref, z_ref, g_ref, r0):
        z = z_ref[pl.ds(r0, rows), :].astype(_f32)
        silu = z / (1.0 + jnp.exp(-z))
        return (_rms(y_ref[pl.ds(r0, rows), :].astype(_f32), g_ref[...]) * silu).astype(o_ref.dtype)

    def group(i, carry):
        r0 = pl.multiple_of(i * rows, rows)
        o_ref[pl.ds(r0, rows), 0:w] = one(ym_ref, zm_ref, gm_ref, r0)
        o_ref[pl.ds(r0, rows), w:2 * w] = one(ys_ref, zs_ref, gs_ref, r0)
        return carry

    lax.fori_loop(0, o_ref.shape[0] // rows, group, 0, unroll=2)


def _gate(y_mla, y_sb, big, g_mla, g_sb, zm_col, zs_col, tm):
    m, w = y_mla.shape
    yspec = pl.BlockSpec((tm, w), lambda i: (i, 0))
    gspec = pl.BlockSpec((1, w), lambda i: (0, 0))
    return pl.pallas_call(
        functools.partial(_gate_kernel, w=w),
        grid=(m // tm,),
        in_specs=[yspec, yspec,
                  pl.BlockSpec((tm, w), lambda i: (i, zm_col // w)),
                  pl.BlockSpec((tm, w), lambda i: (i, zs_col // w)),
                  gspec, gspec],
        out_specs=pl.BlockSpec((tm, 2 * w), lambda i: (i, 0)),
        out_shape=jax.ShapeDtypeStruct((m, 2 * w), _bf16),
        compiler_params=_params("parallel"),
        name="gate",
    )(y_mla, y_sb, big, big, g_mla.reshape(1, w), g_sb.reshape(1, w))


def _rope_tables(lp):
    inv_freq = ROPE_THETA ** (-jnp.arange(0, ROPE, 2, dtype=_f32) / ROPE)
    ang = jnp.arange(lp, dtype=jnp.int32).astype(_f32)[:, None] * inv_freq[None, :]
    cos, sin = jnp.cos(ang), jnp.sin(ang)
    zeros = jnp.zeros((lp, LANE - ROPE), _f32)
    return (jnp.concatenate([cos, cos, zeros], axis=1),
            jnp.concatenate([-sin, sin, zeros], axis=1))


@jax.jit
def _forward(x, meta_tokens, g_norm, w_in, g_q, g_kv, w_uq, w_ukv, g_out_mla, g_out_sb,
             w_o, g_final):
    b, seq, d = x.shape
    depth = w_in.shape[0]
    ql, kvl = g_q.shape[1], g_kv.shape[1]
    heads = w_uq.shape[2] // (NOPE + ROPE)
    w_grp = heads * VDIM
    assert w_ukv.shape[2] == heads * (NOPE + VDIM)
    assert w_in.shape[2] == ql + kvl + ROPE + 5 * w_grp
    assert ql % kvl == 0 and (ql + kvl) % LANE == 0 and kvl % LANE == 0

    l_real = N_META + seq
    lp = -(-l_real // LANE) * LANE
    mp = b * lp

    n_small = ql + kvl + ROPE
    n_in = w_in.shape[2]
    w_rows = jnp.swapaxes(w_in, 1, 2).astype(_bf16).reshape(depth * n_in, d)
    n_small_win = ql + kvl + LANE
    zm_col, qs_col, ks_col, vs_col, zs_col = (i * w_grp for i in range(5))
    colscale = jnp.ones((1, 5 * w_grp), _f32).at[:, qs_col:qs_col + w_grp].set(SB_SCALE2)
    wq = w_uq.reshape(depth, ql, heads, NOPE + ROPE)
    wq = jnp.pad(wq, ((0, 0), (0, 0), (0, 0), (0, QK_CAT - NOPE - ROPE)))
    wq = wq.reshape(depth, ql, heads * QK_CAT).astype(_bf16)
    wkv = w_ukv.astype(_bf16)
    wo = w_o.astype(_bf16)
    cos, sin = _rope_tables(lp)

    meta = jnp.broadcast_to(meta_tokens[None].astype(x.dtype), (b, N_META, d))
    h = jnp.concatenate([meta, x, jnp.zeros((b, lp - l_real, d), x.dtype)], axis=1)
    h = h.reshape(mp, d)

    tm_norm = _pick_tile(lp, 384)
    tm_mm = _pick_tile(lp, 528)
    tm_proj = _pick_tile(lp, 1056)
    tn_big = 1024 if (5 * w_grp) % 1024 == 0 else 512
    tn_out = 1024 if d % 1024 == 0 else 512

    for i in range(depth):
        u = _rmsnorm(h, g_norm[i], _bf16, tm_norm)
        small = _matmul_small(u, w_rows, i * n_in, n_small_win, tm_mm)
        big = _matmul_colscale(u, w_rows, i * n_in + n_small, 5 * w_grp, colscale, _bf16,
                               tm_proj, tn_big)
        q_cat = _q_proj(small, g_q[i], wq, i, cos, sin, lp, tm_proj, ql)
        k_nope, v_mla, k_rope = _kv_proj(small, g_kv[i], wkv, i, cos, sin, lp, tm_proj, ql, kvl)
        y_mla = _mla_attention(q_cat.reshape(b, lp, -1), k_nope.reshape(b, lp, -1),
                               k_rope.reshape(b, lp, -1), v_mla.reshape(b, lp, -1), heads)
        y_sb = _sb_attention(big.reshape(b, lp, -1), heads, qs_col, ks_col, vs_col)
        y = _gate(y_mla.reshape(mp, w_grp), y_sb.reshape(mp, w_grp), big,
                  g_out_mla[i], g_out_sb[i], zm_col, zs_col, tm_norm)
        h = _matmul_residual(y, wo, i, h, tm_mm, tn_out)

    return _final_norm(h.reshape(b, lp, d), g_final, seq, x.dtype, _pick_tile(seq, 512))


def kernel(x, meta_tokens, g_norm, w_in, g_q, g_kv, w_uq, w_ukv, g_out_mla, g_out_sb, w_o, g_final):
    return _forward(x, meta_tokens, g_norm, w_in, g_q, g_kv, w_uq, w_ukv, g_out_mla, g_out_sb,
                    w_o, g_final)
```

```python
import functools
import math

import jax
import jax.numpy as jnp
from jax import lax
from jax.experimental import pallas as pl
from jax.experimental.pallas import tpu as pltpu

N_META = 16
NOPE = 128
ROPE = 64
VDIM = 128
QK_CAT = 256
ROPE_THETA = 10000.0
EPS = 1e-6
LOG2E = math.log2(math.e)
MLA_SCALE2 = LOG2E / math.sqrt(NOPE + ROPE)
SB_SCALE2 = LOG2E / math.sqrt(VDIM)
LANE = 128
F32_SUBLANE = 8
BF16_SUBLANE = 16
ATT_BLK = 256
Q_BLK = 2 * ATT_BLK
MLA_HEADS_PER_STEP = 4
SB_HEADS_PER_STEP = 4
VMEM_LIMIT = 56 * 1024 * 1024
NEG_BIG = -1e30
SB_DEAD_LOG2 = -160.0

_f32 = jnp.float32
_bf16 = jnp.bfloat16


def _pick_tile(n, target):
    best = None
    for t in range(BF16_SUBLANE, min(n, target) + 1, BF16_SUBLANE):
        if n % t == 0:
            best = t
    assert best is not None, (n, target)
    return best


def _params(*sem):
    return pltpu.CompilerParams(dimension_semantics=sem, vmem_limit_bytes=VMEM_LIMIT)


def _rms(xf, g):
    return xf * lax.rsqrt(jnp.mean(xf * xf, axis=-1, keepdims=True) + EPS) * g


def _rmsnorm_kernel(x_ref, g_ref, o_ref):
    o_ref[...] = _rms(x_ref[...].astype(_f32), g_ref[...]).astype(o_ref.dtype)


def _rmsnorm(x, g, out_dtype, tm):
    m, d = x.shape
    return pl.pallas_call(
        _rmsnorm_kernel,
        grid=(m // tm,),
        in_specs=[pl.BlockSpec((tm, d), lambda i: (i, 0)),
                  pl.BlockSpec((1, d), lambda i: (0, 0))],
        out_specs=pl.BlockSpec((tm, d), lambda i: (i, 0)),
        out_shape=jax.ShapeDtypeStruct((m, d), out_dtype),
        compiler_params=_params("parallel"),
        name="rmsnorm",
    )(x, g.reshape(1, d))


def _final_norm(h, g, seq, out_dtype, tm):
    b, lp, d = h.shape
    sub = F32_SUBLANE
    assert N_META % sub == 0 and lp % sub == 0 and tm % sub == 0

    return pl.pallas_call(
        _rmsnorm_kernel,
        grid=(b, seq // tm),
        in_specs=[pl.BlockSpec((pl.Element(tm), pl.Element(d)),
                               lambda bi, i: (pl.multiple_of(bi * lp + N_META + i * tm, sub), 0)),
                  pl.BlockSpec((1, d), lambda bi, i: (0, 0))],
        out_specs=pl.BlockSpec((None, tm, d), lambda bi, i: (bi, i, 0)),
        out_shape=jax.ShapeDtypeStruct((b, seq, d), out_dtype),
        compiler_params=_params("parallel", "parallel"),
        name="final_norm",
    )(h.reshape(b * lp, d), g.reshape(1, d))


def _mm_scale_kernel(x_ref, w_ref, s_ref, o_ref):
    acc = lax.dot_general(x_ref[...], w_ref[...], (((1,), (1,)), ((), ())),
                          preferred_element_type=_f32)
    o_ref[...] = (acc * s_ref[...]).astype(o_ref.dtype)


def _matmul_colscale(x, w_rows, row0, n, colscale, out_dtype, tm, tn):
    m, k = x.shape
    assert row0 % BF16_SUBLANE == 0 and tn % BF16_SUBLANE == 0
    return pl.pallas_call(
        _mm_scale_kernel,
        grid=(n // tn, m // tm),
        in_specs=[pl.BlockSpec((tm, k), lambda j, i: (i, 0)),
                  pl.BlockSpec((pl.Element(tn), pl.Element(k)),
                               lambda j, i: (pl.multiple_of(row0 + j * tn, BF16_SUBLANE), 0)),
                  pl.BlockSpec((1, tn), lambda j, i: (0, j))],
        out_specs=pl.BlockSpec((tm, tn), lambda j, i: (i, j)),
        out_shape=jax.ShapeDtypeStruct((m, n), out_dtype),
        compiler_params=_params("parallel", "parallel"),
        name="matmul_colscale",
    )(x, w_rows, colscale)


def _mm_kernel(x_ref, w_ref, o_ref):
    o_ref[...] = lax.dot_general(x_ref[...], w_ref[...], (((1,), (1,)), ((), ())),
                                 preferred_element_type=_f32).astype(o_ref.dtype)


def _matmul_small(x, w_rows, row0, n, tm):
    m, k = x.shape
    assert row0 % BF16_SUBLANE == 0
    return pl.pallas_call(
        _mm_kernel,
        grid=(m // tm,),
        in_specs=[pl.BlockSpec((tm, k), lambda i: (i, 0)),
                  pl.BlockSpec((pl.Element(n), pl.Element(k)), lambda i: (row0, 0))],
        out_specs=pl.BlockSpec((tm, n), lambda i: (i, 0)),
        out_shape=jax.ShapeDtypeStruct((m, n), _f32),
        compiler_params=_params("parallel"),
        name="matmul_small",
    )(x, w_rows)


def _mm_res_kernel(x_ref, w_ref, r_ref, o_ref):
    o_ref[...] = r_ref[...] + jnp.dot(x_ref[...], w_ref[...], preferred_element_type=_f32)


def _matmul_residual(x, w, layer, res, tm, tn):
    m, k = x.shape
    n = w.shape[2]
    return pl.pallas_call(
        _mm_res_kernel,
        grid=(n // tn, m // tm),
        in_specs=[pl.BlockSpec((tm, k), lambda j, i: (i, 0)),
                  pl.BlockSpec((None, k, tn), lambda j, i: (layer, 0, j)),
                  pl.BlockSpec((tm, tn), lambda j, i: (i, j))],
        out_specs=pl.BlockSpec((tm, tn), lambda j, i: (i, j)),
        out_shape=jax.ShapeDtypeStruct((m, n), _f32),
        compiler_params=_params("parallel", "parallel"),
        name="matmul_residual",
    )(x, w, res)


def _rope128(x, cos, sin_signed):
    lane = lax.broadcasted_iota(jnp.int32, x.shape, 1)
    partner = jnp.where(lane < ROPE // 2,
                        pltpu.roll(x, LANE - ROPE // 2, 1),
                        pltpu.roll(x, ROPE // 2, 1))
    return x * cos + partner * sin_signed


def _qproj_kernel(x_ref, g_ref, w_ref, cos_ref, sin_ref, o_ref, xn_ref, *, heads_per_step):
    @pl.when(pl.program_id(1) == 0)
    def _():
        xn_ref[...] = (_rms(x_ref[...], g_ref[...]) * MLA_SCALE2).astype(_bf16)

    acc = jnp.dot(xn_ref[...], w_ref[...], preferred_element_type=_f32)
    cos = cos_ref[...]
    sin = sin_ref[...]
    for hh in range(heads_per_step):
        c0 = hh * QK_CAT
        o_ref[:, c0:c0 + NOPE] = acc[:, c0:c0 + NOPE].astype(o_ref.dtype)
        roped = _rope128(acc[:, c0 + NOPE:c0 + QK_CAT], cos, sin)
        o_ref[:, c0 + NOPE:c0 + QK_CAT] = roped.astype(o_ref.dtype)


def _q_proj(small, g_q, w_q, layer, cos, sin, lp, tm, ql):
    m = small.shape[0]
    n = w_q.shape[2]
    hps = 4 if (n // QK_CAT) % 4 == 0 else 1
    tn = hps * QK_CAT
    nrow = lp // tm
    return pl.pallas_call(
        functools.partial(_qproj_kernel, heads_per_step=hps),
        grid=(m // tm, n // tn),
        in_specs=[pl.BlockSpec((tm, ql), lambda i, j: (i, 0)),
                  pl.BlockSpec((1, ql), lambda i, j: (0, 0)),
                  pl.BlockSpec((None, ql, tn), lambda i, j: (layer, 0, j)),
                  pl.BlockSpec((tm, LANE), lambda i, j: (i % nrow, 0)),
                  pl.BlockSpec((tm, LANE), lambda i, j: (i % nrow, 0))],
        out_specs=pl.BlockSpec((tm, tn), lambda i, j: (i, j)),
        out_shape=jax.ShapeDtypeStruct((m, n), _bf16),
        scratch_shapes=[pltpu.VMEM((tm, ql), _bf16)],
        compiler_params=_params("parallel", "arbitrary"),
        name="q_proj",
    )(small, g_q.reshape(1, ql), w_q, cos, sin)


def _kvproj_kernel(x_ref, kr_ref, g_ref, w_ref, cos_ref, sin_ref,
                   k_ref, v_ref, krr_ref, xn_ref, *, heads_per_step):
    @pl.when(pl.program_id(1) == 0)
    def _():
        xn_ref[...] = _rms(x_ref[...], g_ref[...]).astype(_bf16)
        krr_ref[...] = _rope128(kr_ref[...], cos_ref[...], sin_ref[...]).astype(krr_ref.dtype)

    kv = jnp.dot(xn_ref[...], w_ref[...], preferred_element_type=_f32)
    for hh in range(heads_per_step):
        c0 = hh * (NOPE + VDIM)
        k_ref[:, hh * NOPE:(hh + 1) * NOPE] = kv[:, c0:c0 + NOPE].astype(k_ref.dtype)
        v_ref[:, hh * VDIM:(hh + 1) * VDIM] = kv[:, c0 + NOPE:c0 + NOPE + VDIM].astype(v_ref.dtype)


def _kv_proj(small, g_kv, w_kv, layer, cos, sin, lp, tm, ql, kvl):
    m = small.shape[0]
    heads = w_kv.shape[2] // (NOPE + VDIM)
    hps = 4 if heads % 4 == 0 else 1
    nrow = lp // tm
    return pl.pallas_call(
        functools.partial(_kvproj_kernel, heads_per_step=hps),
        grid=(m // tm, heads // hps),
        in_specs=[pl.BlockSpec((tm, kvl), lambda i, j: (i, ql // kvl)),
                  pl.BlockSpec((tm, LANE), lambda i, j: (i, (ql + kvl) // LANE)),
                  pl.BlockSpec((1, kvl), lambda i, j: (0, 0)),
                  pl.BlockSpec((None, kvl, hps * (NOPE + VDIM)), lambda i, j: (layer, 0, j)),
                  pl.BlockSpec((tm, LANE), lambda i, j: (i % nrow, 0)),
                  pl.BlockSpec((tm, LANE), lambda i, j: (i % nrow, 0))],
        out_specs=[pl.BlockSpec((tm, hps * NOPE), lambda i, j: (i, j)),
                   pl.BlockSpec((tm, hps * VDIM), lambda i, j: (i, j)),
                   pl.BlockSpec((tm, LANE), lambda i, j: (i, 0))],
        out_shape=[jax.ShapeDtypeStruct((m, heads * NOPE), _bf16),
                   jax.ShapeDtypeStruct((m, heads * VDIM), _bf16),
                   jax.ShapeDtypeStruct((m, LANE), _bf16)],
        scratch_shapes=[pltpu.VMEM((tm, kvl), _bf16)],
        compiler_params=_params("parallel", "arbitrary"),
        name="kv_proj",
    )(small, small, g_kv.reshape(1, kvl), w_kv, cos, sin)


def _dot_nt(a, b):
    return lax.dot_general(a, b, (((1,), (1,)), ((), ())), preferred_element_type=_f32)


def _dot_tn(a, b):
    return lax.dot_general(a, b, (((0,), (0,)), ((), ())), preferred_element_type=_f32)


def _neg_abs(x):
    bits = lax.bitcast_convert_type(x, jnp.uint32) | jnp.uint32(0x80000000)
    return lax.bitcast_convert_type(bits, _f32)


def _row_blocks(lp):
    n_main = lp // Q_BLK
    tail = lp - n_main * Q_BLK
    assert tail <= ATT_BLK, (lp, tail)
    return n_main, tail


def _key_query_iota(bk, bq):
    key = lax.broadcasted_iota(jnp.int32, (bk, bq), 0)
    qry = lax.broadcasted_iota(jnp.int32, (bk, bq), 1)
    return key, qry


def _mla_kernel(q_ref, k_ref, kr_ref, v_ref, o_ref, s_ref, p_ref, al_ref, m_ref, l_ref, acc_ref,
                *, lp, hps):
    n_main, tail = _row_blocks(lp)

    def q_block(q0, bq, n_full, is_tail):
        qs = [q_ref[pl.ds(q0, bq), a * QK_CAT:(a + 1) * QK_CAT] for a in range(hps)]

        full = (0, bq)
        upper = (ATT_BLK, bq)

        def scores(a, k0, bk, lanes=full):
            k_cat = jnp.concatenate([k_ref[pl.ds(k0, bk), a * NOPE:(a + 1) * NOPE],
                                     kr_ref[pl.ds(k0, bk), :]], axis=1)
            return _dot_nt(k_cat, qs[a][lanes[0]:lanes[1]])

        def probs(a, s, slot, bk, mask_off, lanes=full):
            lo, hi = lanes
            if mask_off is not None:
                key, qry = _key_query_iota(bk, hi - lo)
                s = jnp.where(key + mask_off <= qry, s, NEG_BIG)
            m_prev = m_ref[a, :, lo:hi]
            m_new = jnp.maximum(m_prev, jnp.max(s, axis=0, keepdims=True))
            alpha = jnp.exp2(m_prev - m_new)
            p = jnp.exp2(s - m_new)
            l_ref[a, :, lo:hi] = alpha * l_ref[a, :, lo:hi] + jnp.sum(p, axis=0, keepdims=True)
            m_ref[a, :, lo:hi] = m_new
            p_ref[slot, a, 0:bk, lo:hi] = p.astype(_bf16)
            al_ref[slot, a, :, lo:hi] = alpha

        def values(a, slot, k0, bk, lanes=full):
            lo, hi = lanes
            pv = _dot_tn(v_ref[pl.ds(k0, bk), a * VDIM:(a + 1) * VDIM], p_ref[slot, a, 0:bk, lo:hi])
            acc_ref[a, :, lo:hi] = al_ref[slot, a, :, lo:hi] * acc_ref[a, :, lo:hi] + pv

        first = 0 if is_tail else 1
        last = n_full - 1 if is_tail else n_full + 1
        for a in range(hps):
            m_ref[a, :, 0:bq] = jnp.full((1, bq), NEG_BIG, _f32)
            l_ref[a, :, 0:bq] = jnp.zeros((1, bq), _f32)
            acc_ref[a, :, 0:bq] = jnp.zeros((VDIM, bq), _f32)
            p_ref[1 - first, a, :, 0:bq] = jnp.zeros((ATT_BLK, bq), _bf16)
            al_ref[1 - first, a, :, 0:bq] = jnp.ones((1, bq), _f32)
            s_ref[first, a, :, 0:bq] = scores(a, 0, ATT_BLK)

        def step(c, slot, mask_off=None, prefetch=full, lanes=full):
            nxt = pl.multiple_of(jnp.minimum(c + 1, last) * ATT_BLK, ATT_BLK)
            prv = pl.multiple_of(jnp.maximum(c - 1, 0) * ATT_BLK, ATT_BLK)
            for a in range(hps):
                if prefetch is not None:
                    s_ref[1 - slot, a, :, prefetch[0]:prefetch[1]] = scores(a, nxt, ATT_BLK, prefetch)
                probs(a, s_ref[slot, a, :, lanes[0]:lanes[1]], slot, ATT_BLK, mask_off, lanes)
                values(a, 1 - slot, prv, ATT_BLK)

        def pair(t, carry):
            step(2 * t, first)
            step(2 * t + 1, 1 - first)
            return carry

        lax.fori_loop(0, n_full // 2, pair, 0)
        prv = pl.multiple_of(jnp.maximum(n_full - 1, 0) * ATT_BLK, ATT_BLK)
        if is_tail:
            for a in range(hps):
                probs(a, scores(a, q0, bq), 0, bq, 0)
                values(a, 1, prv, ATT_BLK)
                values(a, 0, q0, bq)
        else:
            step(n_full, 1, mask_off=0, prefetch=upper)
            step(n_full + 1, 0, mask_off=0, prefetch=None, lanes=upper)
            k_last = pl.multiple_of((n_full + 1) * ATT_BLK, ATT_BLK)
            for a in range(hps):
                values(a, 0, k_last, ATT_BLK, upper)
        for a in range(hps):
            out_t = acc_ref[a, :, 0:bq] * (1.0 / l_ref[a, :, 0:bq])
            o_ref[pl.ds(q0, bq), a * VDIM:(a + 1) * VDIM] = out_t.T.astype(o_ref.dtype)

    def main_block(i, carry):
        q_block(pl.multiple_of(i * Q_BLK, Q_BLK), Q_BLK, 2 * i, False)
        return carry

    lax.fori_loop(0, n_main, main_block, 0)
    if tail:
        q_block(n_main * Q_BLK, tail, n_main * (Q_BLK // ATT_BLK), True)


def _sb_kernel(q_ref, k_ref, v_ref, o_ref, s_ref, w_ref, r_ref, acc_ref, *, lp, hps):
    n_main, tail = _row_blocks(lp)
    key_i, key_j = _key_query_iota(ATT_BLK, ATT_BLK)
    later = (key_j > key_i).astype(_bf16)

    def q_block(q0, bq, n_full, is_tail):
        qs = [q_ref[pl.ds(q0, bq), a * VDIM:(a + 1) * VDIM] for a in range(hps)]

        def scores(a, k0, bk, lanes=None):
            q = qs[a] if lanes is None else qs[a][lanes[0]:lanes[1]]
            return _dot_nt(k_ref[pl.ds(k0, bk), a * VDIM:(a + 1) * VDIM], q)

        full = (0, bq)
        upper = (ATT_BLK, bq)

        def weights(a, z_of, slot, bk, mask_off, lanes=full):
            for lo in range(lanes[0], lanes[1], ATT_BLK):
                hi = min(lo + ATT_BLK, lanes[1])
                z = z_of(lo, hi)
                tail_sp = jnp.log(1.0 + jnp.exp2(_neg_abs(z))) * LOG2E
                log_beta = jnp.minimum(z, 0.0) - tail_sp
                log_rest = log_beta - z
                masked = mask_off is not None
                if masked:
                    key, qry = _key_query_iota(bk, hi - lo)
                    strict = key + (mask_off - lo) < qry
                    log_rest = jnp.where(strict, log_rest, 0.0)
                within = jnp.dot(later[0:bk, 0:bk], log_rest.astype(_bf16),
                                 preferred_element_type=_f32)
                r = r_ref[a, :, lo:hi]
                w = jnp.exp2(log_beta + within + r)
                if masked:
                    w = jnp.where(strict, w, 0.0)
                w_ref[slot, a, 0:bk, lo:hi] = w.astype(_bf16)
                r_ref[a, :, lo:hi] = r + within[0:1, :] + log_rest[0:1, :]

        def from_value(z, lane0=0):
            return lambda lo, hi: z[:, lo - lane0:hi - lane0]

        def values(a, slot, k0, bk, lanes=full):
            lo, hi = lanes
            acc_ref[a, :, lo:hi] = acc_ref[a, :, lo:hi] + _dot_tn(
                v_ref[pl.ds(k0, bk), a * VDIM:(a + 1) * VDIM], w_ref[slot, a, 0:bk, lo:hi])

        v_last = n_full - 1 if is_tail else n_full + 1

        def step(c, slot, mask_off=None, v_lanes=full):
            k_before = pl.multiple_of(jnp.maximum(c - 1, 0) * ATT_BLK, ATT_BLK)
            k_after = pl.multiple_of(jnp.minimum(c + 1, v_last) * ATT_BLK, ATT_BLK)
            for a in range(hps):
                s_ref[1 - slot, a, :, 0:bq] = scores(a, k_before, ATT_BLK)
                weights(a, lambda lo, hi: s_ref[slot, a, :, lo:hi], slot, ATT_BLK, mask_off)
                values(a, 1 - slot, k_after, ATT_BLK, v_lanes)

        k_top = pl.multiple_of(jnp.maximum(n_full - 1, 0) * ATT_BLK, ATT_BLK)
        for a in range(hps):
            r_ref[a, :, 0:bq] = jnp.zeros((1, bq), _f32)
            acc_ref[a, :, 0:bq] = jnp.zeros((VDIM, bq), _f32)
            if is_tail:
                s_ref[1, a, :, 0:bq] = scores(a, k_top, ATT_BLK)
                weights(a, from_value(scores(a, q0, bq)), 0, bq, 0)
                values(a, 0, q0, bq)
                w_ref[0, a, :, 0:bq] = jnp.zeros((ATT_BLK, bq), _bf16)
            else:
                k_hi = pl.multiple_of((n_full + 1) * ATT_BLK, ATT_BLK)
                s_ref[0, a, :, 0:bq] = scores(a, q0, ATT_BLK)
                weights(a, from_value(scores(a, k_hi, ATT_BLK, upper), ATT_BLK), 1, ATT_BLK,
                        ATT_BLK, upper)
        if not is_tail:
            step(n_full, 0, mask_off=0, v_lanes=upper)

        n_pairs = n_full // 2

        def alive():
            r_max = r_ref[0, :, 0:bq]
            for a in range(1, hps):
                r_max = jnp.maximum(r_max, r_ref[a, :, 0:bq])
            return (jnp.max(r_max) > SB_DEAD_LOG2).astype(jnp.int32)

        def pair(carry):
            tt, _, _ = carry
            c = 2 * (n_pairs - 1 - tt) + 1
            step(c, 1)
            go_on = alive()

            @pl.when(go_on == 1)
            def _():
                step(c - 1, 0)

            return tt + 1, c - go_on, go_on

        _, c_fin, _ = lax.while_loop(
            lambda carry: jnp.logical_and(carry[0] < n_pairs, carry[2] == 1), pair,
            (jnp.int32(0), jnp.int32(0) + n_full, jnp.int32(1)))
        k_fin = pl.multiple_of(jnp.minimum(c_fin, v_last) * ATT_BLK, ATT_BLK)
        for a in range(hps):
            values(a, c_fin % 2, k_fin, ATT_BLK)
            o_ref[pl.ds(q0, bq), a * VDIM:(a + 1) * VDIM] = acc_ref[a, :, 0:bq].T.astype(o_ref.dtype)

    def main_block(i, carry):
        q_block(pl.multiple_of(i * Q_BLK, Q_BLK), Q_BLK, 2 * i, False)
        return carry

    lax.fori_loop(0, n_main, main_block, 0)
    if tail:
        q_block(n_main * Q_BLK, tail, n_main * (Q_BLK // ATT_BLK), True)


def _att_scratch(hps, slot_stats, n_stat):
    return ([pltpu.VMEM((2, hps, ATT_BLK, Q_BLK), _f32),
             pltpu.VMEM((2, hps, ATT_BLK, Q_BLK), _bf16)]
            + [pltpu.VMEM((2, hps, 1, Q_BLK), _f32)] * slot_stats
            + [pltpu.VMEM((hps, 1, Q_BLK), _f32)] * n_stat
            + [pltpu.VMEM((hps, VDIM, Q_BLK), _f32)])


def _mla_attention(q_cat, k_nope, k_rope, v, heads):
    b, lp, _ = q_cat.shape
    hps = MLA_HEADS_PER_STEP if heads % MLA_HEADS_PER_STEP == 0 else 1
    q_spec = pl.BlockSpec((None, lp, hps * QK_CAT), lambda bi, h: (bi, 0, h))
    kr_spec = pl.BlockSpec((None, lp, LANE), lambda bi, h: (bi, 0, 0))
    v_spec = pl.BlockSpec((None, lp, hps * VDIM), lambda bi, h: (bi, 0, h))
    return pl.pallas_call(
        functools.partial(_mla_kernel, lp=lp, hps=hps),
        grid=(b, heads // hps),
        in_specs=[q_spec, v_spec, kr_spec, v_spec],
        out_specs=v_spec,
        out_shape=jax.ShapeDtypeStruct((b, lp, heads * VDIM), _bf16),
        scratch_shapes=_att_scratch(hps, 1, 2),
        compiler_params=_params("parallel", "parallel"),
        name="mla_attention",
    )(q_cat, k_nope, k_rope, v)


def _sb_attention(big, heads, q_col, k_col, v_col):
    b, lp, _ = big.shape
    hps = SB_HEADS_PER_STEP if heads % SB_HEADS_PER_STEP == 0 else 1
    w = hps * VDIM

    def spec(col0):
        return pl.BlockSpec((None, lp, w), lambda bi, h: (bi, 0, col0 // w + h))

    return pl.pallas_call(
        functools.partial(_sb_kernel, lp=lp, hps=hps),
        grid=(b, heads // hps),
        in_specs=[spec(q_col), spec(k_col), spec(v_col)],
        out_specs=pl.BlockSpec((None, lp, w), lambda bi, h: (bi, 0, h)),
        out_shape=jax.ShapeDtypeStruct((b, lp, heads * VDIM), _bf16),
        scratch_shapes=_att_scratch(hps, 0, 1),
        compiler_params=_params("parallel", "parallel"),
        name="sb_attention",
    )(big, big, big)


def _gate_kernel(ym_ref, ys_ref, zm_ref, zs_ref, gm_ref, gs_ref, o_ref, *, w):
    rows = BF16_SUBLANE

    def one(y_ref, z_ref, g_ref, r0):
        z = z_ref[pl.ds(r0, rows), :].astype(_f32)
        silu = z / (1.0 + jnp.exp(-z))
        return (_rms(y_ref[pl.ds(r0, rows), :].astype(_f32), g_ref[...]) * silu).astype(o_ref.dtype)

    def group(i, carry):
        r0 = pl.multiple_of(i * rows, rows)
        o_ref[pl.ds(r0, rows), 0:w] = one(ym_ref, zm_ref, gm_ref, r0)
        o_ref[pl.ds(r0, rows), w:2 * w] = one(ys_ref, zs_ref, gs_ref, r0)
        return carry

    lax.fori_loop(0, o_ref.shape[0] // rows, group, 0, unroll=2)


def _gate(y_mla, y_sb, big, g_mla, g_sb, zm_col, zs_col, tm):
    m, w = y_mla.shape
    yspec = pl.BlockSpec((tm, w), lambda i: (i, 0))
    gspec = pl.BlockSpec((1, w), lambda i: (0, 0))
    return pl.pallas_call(
        functools.partial(_gate_kernel, w=w),
        grid=(m // tm,),
        in_specs=[yspec, yspec,
                  pl.BlockSpec((tm, w), lambda i: (i, zm_col // w)),
                  pl.BlockSpec((tm, w), lambda i: (i, zs_col // w)),
                  gspec, gspec],
        out_specs=pl.BlockSpec((tm, 2 * w), lambda i: (i, 0)),
        out_shape=jax.ShapeDtypeStruct((m, 2 * w), _bf16),
        compiler_params=_params("parallel"),
        name="gate",
    )(y_mla, y_sb, big, big, g_mla.reshape(1, w), g_sb.reshape(1, w))


def _rope_tables(lp):
    inv_freq = ROPE_THETA ** (-jnp.arange(0, ROPE, 2, dtype=_f32) / ROPE)
    ang = jnp.arange(lp, dtype=jnp.int32).astype(_f32)[:, None] * inv_freq[None, :]
    cos, sin = jnp.cos(ang), jnp.sin(ang)
    zeros = jnp.zeros((lp, LANE - ROPE), _f32)
    return (jnp.concatenate([cos, cos, zeros], axis=1),
            jnp.concatenate([-sin, sin, zeros], axis=1))


@jax.jit
def _forward(x, meta_tokens, g_norm, w_in, g_q, g_kv, w_uq, w_ukv, g_out_mla, g_out_sb,
             w_o, g_final):
    b, seq, d = x.shape
    depth = w_in.shape[0]
    ql, kvl = g_q.shape[1], g_kv.shape[1]
    heads = w_uq.shape[2] // (NOPE + ROPE)
    w_grp = heads * VDIM
    assert w_ukv.shape[2] == heads * (NOPE + VDIM)
    assert w_in.shape[2] == ql + kvl + ROPE + 5 * w_grp
    assert ql % kvl == 0 and (ql + kvl) % LANE == 0 and kvl % LANE == 0

    l_real = N_META + seq
    lp = -(-l_real // LANE) * LANE
    mp = b * lp

    n_small = ql + kvl + ROPE
    n_in = w_in.shape[2]
    w_rows = jnp.swapaxes(w_in, 1, 2).astype(_bf16).reshape(depth * n_in, d)
    n_small_win = ql + kvl + LANE
    zm_col, qs_col, ks_col, vs_col, zs_col = (i * w_grp for i in range(5))
    colscale = jnp.ones((1, 5 * w_grp), _f32).at[:, qs_col:qs_col + w_grp].set(SB_SCALE2)
    wq = w_uq.reshape(depth, ql, heads, NOPE + ROPE)
    wq = jnp.pad(wq, ((0, 0), (0, 0), (0, 0), (0, QK_CAT - NOPE - ROPE)))
    wq = wq.reshape(depth, ql, heads * QK_CAT).astype(_bf16)
    wkv = w_ukv.astype(_bf16)
    wo = w_o.astype(_bf16)
    cos, sin = _rope_tables(lp)

    meta = jnp.broadcast_to(meta_tokens[None].astype(x.dtype), (b, N_META, d))
    h = jnp.concatenate([meta, x, jnp.zeros((b, lp - l_real, d), x.dtype)], axis=1)
    h = h.reshape(mp, d)

    tm_norm = _pick_tile(lp, 384)
    tm_mm = _pick_tile(lp, 528)
    tm_proj = _pick_tile(lp, 1056)
    tn_big = 1024 if (5 * w_grp) % 1024 == 0 else 512
    tn_out = 1024 if d % 1024 == 0 else 512

    for i in range(depth):
        u = _rmsnorm(h, g_norm[i], _bf16, tm_norm)
        small = _matmul_small(u, w_rows, i * n_in, n_small_win, tm_mm)
        big = _matmul_colscale(u, w_rows, i * n_in + n_small, 5 * w_grp, colscale, _bf16,
                               tm_proj, tn_big)
        q_cat = _q_proj(small, g_q[i], wq, i, cos, sin, lp, tm_proj, ql)
        k_nope, v_mla, k_rope = _kv_proj(small, g_kv[i], wkv, i, cos, sin, lp, tm_proj, ql, kvl)
        y_mla = _mla_attention(q_cat.reshape(b, lp, -1), k_nope.reshape(b, lp, -1),
                               k_rope.reshape(b, lp, -1), v_mla.reshape(b, lp, -1), heads)
        y_sb = _sb_attention(big.reshape(b, lp, -1), heads, qs_col, ks_col, vs_col)
        y = _gate(y_mla.reshape(mp, w_grp), y_sb.reshape(mp, w_grp), big,
                  g_out_mla[i], g_out_sb[i], zm_col, zs_col, tm_norm)
        h = _matmul_residual(y, wo, i, h, tm_mm, tn_out)

    return _final_norm(h.reshape(b, lp, d), g_final, seq, x.dtype, _pick_tile(seq, 512))


def kernel(x, meta_tokens, g_norm, w_in, g_q, g_kv, w_uq, w_ukv, g_out_mla, g_out_sb, w_o, g_final):
    return _forward(x, meta_tokens, g_norm, w_in, g_q, g_kv, w_uq, w_ukv, g_out_mla, g_out_sb,
                    w_o, g_final)
```

```python
import functools
import math

import jax
import jax.numpy as jnp
from jax import lax
from jax.experimental import pallas as pl
from jax.experimental.pallas import tpu as pltpu

N_META = 16
NOPE = 128
ROPE = 64
VDIM = 128
QK_CAT = 256
ROPE_THETA = 10000.0
EPS = 1e-6
LOG2E = math.log2(math.e)
MLA_SCALE2 = LOG2E / math.sqrt(NOPE + ROPE)
SB_SCALE2 = LOG2E / math.sqrt(VDIM)
LANE = 128
F32_SUBLANE = 8
BF16_SUBLANE = 16
ATT_BLK = 256
Q_BLK = 2 * ATT_BLK
MLA_HEADS_PER_STEP = 4
SB_HEADS_PER_STEP = 4
VMEM_LIMIT = 56 * 1024 * 1024
NEG_BIG = -1e30
SB_DEAD_LOG2 = -160.0

_f32 = jnp.float32
_bf16 = jnp.bfloat16


def _pick_tile(n, target):
    best = None
    for t in range(BF16_SUBLANE, min(n, target) + 1, BF16_SUBLANE):
        if n % t == 0:
            best = t
    assert best is not None, (n, target)
    return best


def _params(*sem):
    return pltpu.CompilerParams(dimension_semantics=sem, vmem_limit_bytes=VMEM_LIMIT)


def _rms(xf, g):
    return xf * lax.rsqrt(jnp.mean(xf * xf, axis=-1, keepdims=True) + EPS) * g


def _rmsnorm_kernel(x_ref, g_ref, o_ref):
    o_ref[...] = _rms(x_ref[...].astype(_f32), g_ref[...]).astype(o_ref.dtype)


def _rmsnorm(x, g, out_dtype, tm):
    m, d = x.shape
    return pl.pallas_call(
        _rmsnorm_kernel,
        grid=(m // tm,),
        in_specs=[pl.BlockSpec((tm, d), lambda i: (i, 0)),
                  pl.BlockSpec((1, d), lambda i: (0, 0))],
        out_specs=pl.BlockSpec((tm, d), lambda i: (i, 0)),
        out_shape=jax.ShapeDtypeStruct((m, d), out_dtype),
        compiler_params=_params("parallel"),
        name="rmsnorm",
    )(x, g.reshape(1, d))


def _final_norm(h, g, seq, out_dtype, tm):
    b, lp, d = h.shape
    sub = F32_SUBLANE
    assert N_META % sub == 0 and lp % sub == 0 and tm % sub == 0

    return pl.pallas_call(
        _rmsnorm_kernel,
        grid=(b, seq // tm),
        in_specs=[pl.BlockSpec((pl.Element(tm), pl.Element(d)),
                               lambda bi, i: (pl.multiple_of(bi * lp + N_META + i * tm, sub), 0)),
                  pl.BlockSpec((1, d), lambda bi, i: (0, 0))],
        out_specs=pl.BlockSpec((None, tm, d), lambda bi, i: (bi, i, 0)),
        out_shape=jax.ShapeDtypeStruct((b, seq, d), out_dtype),
        compiler_params=_params("parallel", "parallel"),
        name="final_norm",
    )(h.reshape(b * lp, d), g.reshape(1, d))


def _row_scale(rs_ref, n):
    return jnp.tile(rs_ref[...], (1, n // LANE))


def _mm_scale_kernel(x_ref, w_ref, s_ref, rs_ref, o_ref):
    acc = lax.dot_general(x_ref[...], w_ref[...], (((1,), (1,)), ((), ())),
                          preferred_element_type=_f32)
    o_ref[...] = (acc * _row_scale(rs_ref, acc.shape[1]) * s_ref[...]).astype(o_ref.dtype)


def _matmul_colscale(x, rs, w_rows, row0, n, colscale, out_dtype, tm, tn):
    m, k = x.shape
    assert row0 % BF16_SUBLANE == 0 and tn % BF16_SUBLANE == 0
    return pl.pallas_call(
        _mm_scale_kernel,
        grid=(n // tn, m // tm),
        in_specs=[pl.BlockSpec((tm, k), lambda j, i: (i, 0)),
                  pl.BlockSpec((pl.Element(tn), pl.Element(k)),
                               lambda j, i: (pl.multiple_of(row0 + j * tn, BF16_SUBLANE), 0)),
                  pl.BlockSpec((1, tn), lambda j, i: (0, j)),
                  pl.BlockSpec((tm, LANE), lambda j, i: (i, 0))],
        out_specs=pl.BlockSpec((tm, tn), lambda j, i: (i, j)),
        out_shape=jax.ShapeDtypeStruct((m, n), out_dtype),
        compiler_params=_params("parallel", "parallel"),
        name="matmul_colscale",
    )(x, w_rows, colscale, rs)


def _mm_kernel(x_ref, w_ref, rs_ref, o_ref):
    acc = lax.dot_general(x_ref[...], w_ref[...], (((1,), (1,)), ((), ())),
                          preferred_element_type=_f32)
    o_ref[...] = (acc * _row_scale(rs_ref, acc.shape[1])).astype(o_ref.dtype)


def _matmul_small(x, rs, w_rows, row0, n, tm):
    m, k = x.shape
    assert row0 % BF16_SUBLANE == 0
    return pl.pallas_call(
        _mm_kernel,
        grid=(m // tm,),
        in_specs=[pl.BlockSpec((tm, k), lambda i: (i, 0)),
                  pl.BlockSpec((pl.Element(n), pl.Element(k)), lambda i: (row0, 0)),
                  pl.BlockSpec((tm, LANE), lambda i: (i, 0))],
        out_specs=pl.BlockSpec((tm, n), lambda i: (i, 0)),
        out_shape=jax.ShapeDtypeStruct((m, n), _f32),
        compiler_params=_params("parallel"),
        name="matmul_small",
    )(x, w_rows, rs)


def _mm_res_kernel(x_ref, w_ref, r_ref, g_ref, o_ref, u_ref, ss_ref):
    h = r_ref[...] + jnp.dot(x_ref[...], w_ref[...], preferred_element_type=_f32)
    o_ref[...] = h
    u_ref[...] = (h * g_ref[...]).astype(u_ref.dtype)
    ss_ref[...] = jnp.broadcast_to(jnp.sum(h * h, axis=-1, keepdims=True), ss_ref.shape)


def _matmul_residual(x, w, layer, res, g_next, tm, tn):
    m, k = x.shape
    n = w.shape[2]
    tile = pl.BlockSpec((tm, tn), lambda j, i: (i, j))
    return pl.pallas_call(
        _mm_res_kernel,
        grid=(n // tn, m // tm),
        in_specs=[pl.BlockSpec((tm, k), lambda j, i: (i, 0)),
                  pl.BlockSpec((None, k, tn), lambda j, i: (layer, 0, j)),
                  tile,
                  pl.BlockSpec((1, tn), lambda j, i: (0, j))],
        out_specs=[tile, tile, pl.BlockSpec((None, tm, LANE), lambda j, i: (j, i, 0))],
        out_shape=[jax.ShapeDtypeStruct((m, n), _f32), jax.ShapeDtypeStruct((m, n), _bf16),
                   jax.ShapeDtypeStruct((n // tn, m, LANE), _f32)],
        compiler_params=_params("parallel", "parallel"),
        name="matmul_residual",
    )(x, w, res, g_next.reshape(1, n))


def _rope128(x, cos, sin_signed):
    lane = lax.broadcasted_iota(jnp.int32, x.shape, 1)
    partner = jnp.where(lane < ROPE // 2,
                        pltpu.roll(x, LANE - ROPE // 2, 1),
                        pltpu.roll(x, ROPE // 2, 1))
    return x * cos + partner * sin_signed


def _qproj_kernel(x_ref, g_ref, w_ref, cos_ref, sin_ref, o_ref, xn_ref, *, heads_per_step):
    @pl.when(pl.program_id(1) == 0)
    def _():
        xn_ref[...] = (_rms(x_ref[...], g_ref[...]) * MLA_SCALE2).astype(_bf16)

    acc = jnp.dot(xn_ref[...], w_ref[...], preferred_element_type=_f32)
    cos = cos_ref[...]
    sin = sin_ref[...]
    for hh in range(heads_per_step):
        c0 = hh * QK_CAT
        o_ref[:, c0:c0 + NOPE] = acc[:, c0:c0 + NOPE].astype(o_ref.dtype)
        roped = _rope128(acc[:, c0 + NOPE:c0 + QK_CAT], cos, sin)
        o_ref[:, c0 + NOPE:c0 + QK_CAT] = roped.astype(o_ref.dtype)


def _q_proj(small, g_q, w_q, layer, cos, sin, lp, tm, ql):
    m = small.shape[0]
    n = w_q.shape[2]
    hps = 4 if (n // QK_CAT) % 4 == 0 else 1
    tn = hps * QK_CAT
    nrow = lp // tm
    return pl.pallas_call(
        functools.partial(_qproj_kernel, heads_per_step=hps),
        grid=(m // tm, n // tn),
        in_specs=[pl.BlockSpec((tm, ql), lambda i, j: (i, 0)),
                  pl.BlockSpec((1, ql), lambda i, j: (0, 0)),
                  pl.BlockSpec((None, ql, tn), lambda i, j: (layer, 0, j)),
                  pl.BlockSpec((tm, LANE), lambda i, j: (i % nrow, 0)),
                  pl.BlockSpec((tm, LANE), lambda i, j: (i % nrow, 0))],
        out_specs=pl.BlockSpec((tm, tn), lambda i, j: (i, j)),
        out_shape=jax.ShapeDtypeStruct((m, n), _bf16),
        scratch_shapes=[pltpu.VMEM((tm, ql), _bf16)],
        compiler_params=_params("parallel", "arbitrary"),
        name="q_proj",
    )(small, g_q.reshape(1, ql), w_q, cos, sin)


def _kvproj_kernel(x_ref, kr_ref, g_ref, w_ref, cos_ref, sin_ref,
                   k_ref, v_ref, krr_ref, xn_ref, *, heads_per_step):
    @pl.when(pl.program_id(1) == 0)
    def _():
        xn_ref[...] = _rms(x_ref[...], g_ref[...]).astype(_bf16)
        krr_ref[...] = _rope128(kr_ref[...], cos_ref[...], sin_ref[...]).astype(krr_ref.dtype)

    kv = jnp.dot(xn_ref[...], w_ref[...], preferred_element_type=_f32)
    for hh in range(heads_per_step):
        c0 = hh * (NOPE + VDIM)
        k_ref[:, hh * NOPE:(hh + 1) * NOPE] = kv[:, c0:c0 + NOPE].astype(k_ref.dtype)
        v_ref[:, hh * VDIM:(hh + 1) * VDIM] = kv[:, c0 + NOPE:c0 + NOPE + VDIM].astype(v_ref.dtype)


def _kv_proj(small, g_kv, w_kv, layer, cos, sin, lp, tm, ql, kvl):
    m = small.shape[0]
    heads = w_kv.shape[2] // (NOPE + VDIM)
    hps = 4 if heads % 4 == 0 else 1
    nrow = lp // tm
    return pl.pallas_call(
        functools.partial(_kvproj_kernel, heads_per_step=hps),
        grid=(m // tm, heads // hps),
        in_specs=[pl.BlockSpec((tm, kvl), lambda i, j: (i, ql // kvl)),
                  pl.BlockSpec((tm, LANE), lambda i, j: (i, (ql + kvl) // LANE)),
                  pl.BlockSpec((1, kvl), lambda i, j: (0, 0)),
                  pl.BlockSpec((None, kvl, hps * (NOPE + VDIM)), lambda i, j: (layer, 0, j)),
                  pl.BlockSpec((tm, LANE), lambda i, j: (i % nrow, 0)),
                  pl.BlockSpec((tm, LANE), lambda i, j: (i % nrow, 0))],
        out_specs=[pl.BlockSpec((tm, hps * NOPE), lambda i, j: (i, j)),
                   pl.BlockSpec((tm, hps * VDIM), lambda i, j: (i, j)),
                   pl.BlockSpec((tm, LANE), lambda i, j: (i, 0))],
        out_shape=[jax.ShapeDtypeStruct((m, heads * NOPE), _bf16),
                   jax.ShapeDtypeStruct((m, heads * VDIM), _bf16),
                   jax.ShapeDtypeStruct((m, LANE), _bf16)],
        scratch_shapes=[pltpu.VMEM((tm, kvl), _bf16)],
        compiler_params=_params("parallel", "arbitrary"),
        name="kv_proj",
    )(small, small, g_kv.reshape(1, kvl), w_kv, cos, sin)


def _dot_nt(a, b):
    return lax.dot_general(a, b, (((1,), (1,)), ((), ())), preferred_element_type=_f32)


def _dot_tn(a, b):
    return lax.dot_general(a, b, (((0,), (0,)), ((), ())), preferred_element_type=_f32)


def _neg_abs(x):
    bits = lax.bitcast_convert_type(x, jnp.uint32) | jnp.uint32(0x80000000)
    return lax.bitcast_convert_type(bits, _f32)


def _row_blocks(lp):
    n_main = lp // Q_BLK
    tail = lp - n_main * Q_BLK
    assert tail <= ATT_BLK, (lp, tail)
    return n_main, tail


def _key_query_iota(bk, bq):
    key = lax.broadcasted_iota(jnp.int32, (bk, bq), 0)
    qry = lax.broadcasted_iota(jnp.int32, (bk, bq), 1)
    return key, qry


def _mla_kernel(q_ref, k_ref, kr_ref, v_ref, o_ref, s_ref, p_ref, al_ref, m_ref, l_ref, acc_ref,
                *, lp, hps):
    n_main, tail = _row_blocks(lp)

    def q_block(q0, bq, n_full, is_tail):
        qs = [q_ref[pl.ds(q0, bq), a * QK_CAT:(a + 1) * QK_CAT] for a in range(hps)]

        full = (0, bq)
        upper = (ATT_BLK, bq)

        def scores(a, k0, bk, lanes=full):
            k_cat = jnp.concatenate([k_ref[pl.ds(k0, bk), a * NOPE:(a + 1) * NOPE],
                                     kr_ref[pl.ds(k0, bk), :]], axis=1)
            return _dot_nt(k_cat, qs[a][lanes[0]:lanes[1]])

        def probs(a, s, slot, bk, mask_off, lanes=full):
            lo, hi = lanes
            if mask_off is not None:
                key, qry = _key_query_iota(bk, hi - lo)
                s = jnp.where(key + mask_off <= qry, s, NEG_BIG)
            m_prev = m_ref[a, :, lo:hi]
            m_new = jnp.maximum(m_prev, jnp.max(s, axis=0, keepdims=True))
            alpha = jnp.exp2(m_prev - m_new)
            p = jnp.exp2(s - m_new)
            l_ref[a, :, lo:hi] = alpha * l_ref[a, :, lo:hi] + jnp.sum(p, axis=0, keepdims=True)
            m_ref[a, :, lo:hi] = m_new
            p_ref[slot, a, 0:bk, lo:hi] = p.astype(_bf16)
            al_ref[slot, a, :, lo:hi] = alpha

        def values(a, slot, k0, bk, lanes=full):
            lo, hi = lanes
            pv = _dot_tn(v_ref[pl.ds(k0, bk), a * VDIM:(a + 1) * VDIM], p_ref[slot, a, 0:bk, lo:hi])
            acc_ref[a, :, lo:hi] = al_ref[slot, a, :, lo:hi] * acc_ref[a, :, lo:hi] + pv

        first = 0 if is_tail else 1
        last = n_full - 1 if is_tail else n_full + 1
        for a in range(hps):
            m_ref[a, :, 0:bq] = jnp.full((1, bq), NEG_BIG, _f32)
            l_ref[a, :, 0:bq] = jnp.zeros((1, bq), _f32)
            acc_ref[a, :, 0:bq] = jnp.zeros((VDIM, bq), _f32)
            p_ref[1 - first, a, :, 0:bq] = jnp.zeros((ATT_BLK, bq), _bf16)
            al_ref[1 - first, a, :, 0:bq] = jnp.ones((1, bq), _f32)
            s_ref[first, a, :, 0:bq] = scores(a, 0, ATT_BLK)

        def step(c, slot, mask_off=None, prefetch=full, lanes=full):
            nxt = pl.multiple_of(jnp.minimum(c + 1, last) * ATT_BLK, ATT_BLK)
            prv = pl.multiple_of(jnp.maximum(c - 1, 0) * ATT_BLK, ATT_BLK)
            for a in range(hps):
                if prefetch is not None:
                    s_ref[1 - slot, a, :, prefetch[0]:prefetch[1]] = scores(a, nxt, ATT_BLK, prefetch)
                probs(a, s_ref[slot, a, :, lanes[0]:lanes[1]], slot, ATT_BLK, mask_off, lanes)
                values(a, 1 - slot, prv, ATT_BLK)

        def pair(t, carry):
            step(2 * t, first)
            step(2 * t + 1, 1 - first)
            return carry

        lax.fori_loop(0, n_full // 2, pair, 0)
        prv = pl.multiple_of(jnp.maximum(n_full - 1, 0) * ATT_BLK, ATT_BLK)
        if is_tail:
            for a in range(hps):
                probs(a, scores(a, q0, bq), 0, bq, 0)
                values(a, 1, prv, ATT_BLK)
                values(a, 0, q0, bq)
        else:
            step(n_full, 1, mask_off=0, prefetch=upper)
            step(n_full + 1, 0, mask_off=0, prefetch=None, lanes=upper)
            k_last = pl.multiple_of((n_full + 1) * ATT_BLK, ATT_BLK)
            for a in range(hps):
                values(a, 0, k_last, ATT_BLK, upper)
        for a in range(hps):
            out_t = acc_ref[a, :, 0:bq] * (1.0 / l_ref[a, :, 0:bq])
            o_ref[pl.ds(q0, bq), a * VDIM:(a + 1) * VDIM] = out_t.T.astype(o_ref.dtype)

    def main_block(i, carry):
        q_block(pl.multiple_of(i * Q_BLK, Q_BLK), Q_BLK, 2 * i, False)
        return carry

    lax.fori_loop(0, n_main, main_block, 0)
    if tail:
        q_block(n_main * Q_BLK, tail, n_main * (Q_BLK // ATT_BLK), True)


def _sb_kernel(q_ref, k_ref, v_ref, o_ref, s_ref, w_ref, r_ref, acc_ref, *, lp, hps):
    n_main, tail = _row_blocks(lp)
    key_i, key_j = _key_query_iota(ATT_BLK, ATT_BLK)
    later = (key_j > key_i).astype(_bf16)

    def q_block(q0, bq, n_full, is_tail):
        qs = [q_ref[pl.ds(q0, bq), a * VDIM:(a + 1) * VDIM] for a in range(hps)]

        def scores(a, k0, bk, lanes=None):
            q = qs[a] if lanes is None else qs[a][lanes[0]:lanes[1]]
            return _dot_nt(k_ref[pl.ds(k0, bk), a * VDIM:(a + 1) * VDIM], q)

        full = (0, bq)
        upper = (ATT_BLK, bq)

        def weights(a, z_of, slot, bk, mask_off, lanes=full):
            for lo in range(lanes[0], lanes[1], ATT_BLK):
                hi = min(lo + ATT_BLK, lanes[1])
                z = z_of(lo, hi)
                tail_sp = jnp.log(1.0 + jnp.exp2(_neg_abs(z))) * LOG2E
                log_beta = jnp.minimum(z, 0.0) - tail_sp
                log_rest = log_beta - z
                masked = mask_off is not None
                if masked:
                    key, qry = _key_query_iota(bk, hi - lo)
                    strict = key + (mask_off - lo) < qry
                    log_rest = jnp.where(strict, log_rest, 0.0)
                within = jnp.dot(later[0:bk, 0:bk], log_rest.astype(_bf16),
                                 preferred_element_type=_f32)
                r = r_ref[a, :, lo:hi]
                w = jnp.exp2(log_beta + within + r)
                if masked:
                    w = jnp.where(strict, w, 0.0)
                w_ref[slot, a, 0:bk, lo:hi] = w.astype(_bf16)
                r_ref[a, :, lo:hi] = r + within[0:1, :] + log_rest[0:1, :]

        def from_value(z, lane0=0):
            return lambda lo, hi: z[:, lo - lane0:hi - lane0]

        def values(a, slot, k0, bk, lanes=full):
            lo, hi = lanes
            acc_ref[a, :, lo:hi] = acc_ref[a, :, lo:hi] + _dot_tn(
                v_ref[pl.ds(k0, bk), a * VDIM:(a + 1) * VDIM], w_ref[slot, a, 0:bk, lo:hi])

        v_last = n_full - 1 if is_tail else n_full + 1

        def step(c, slot, mask_off=None, v_lanes=full):
            k_before = pl.multiple_of(jnp.maximum(c - 1, 0) * ATT_BLK, ATT_BLK)
            k_after = pl.multiple_of(jnp.minimum(c + 1, v_last) * ATT_BLK, ATT_BLK)
            for a in range(hps):
                s_ref[1 - slot, a, :, 0:bq] = scores(a, k_before, ATT_BLK)
                weights(a, lambda lo, hi: s_ref[slot, a, :, lo:hi], slot, ATT_BLK, mask_off)
                values(a, 1 - slot, k_after, ATT_BLK, v_lanes)

        k_top = pl.multiple_of(jnp.maximum(n_full - 1, 0) * ATT_BLK, ATT_BLK)
        for a in range(hps):
            r_ref[a, :, 0:bq] = jnp.zeros((1, bq), _f32)
            acc_ref[a, :, 0:bq] = jnp.zeros((VDIM, bq), _f32)
            if is_tail:
                s_ref[1, a, :, 0:bq] = scores(a, k_top, ATT_BLK)
                weights(a, from_value(scores(a, q0, bq)), 0, bq, 0)
                values(a, 0, q0, bq)
                w_ref[0, a, :, 0:bq] = jnp.zeros((ATT_BLK, bq), _bf16)
            else:
                k_hi = pl.multiple_of((n_full + 1) * ATT_BLK, ATT_BLK)
                s_ref[0, a, :, 0:bq] = scores(a, q0, ATT_BLK)
                weights(a, from_value(scores(a, k_hi, ATT_BLK, upper), ATT_BLK), 1, ATT_BLK,
                        ATT_BLK, upper)
        if not is_tail:
            step(n_full, 0, mask_off=0, v_lanes=upper)

        n_pairs = n_full // 2

        def alive():
            r_max = r_ref[0, :, 0:bq]
            for a in range(1, hps):
                r_max = jnp.maximum(r_max, r_ref[a, :, 0:bq])
            return (jnp.max(r_max) > SB_DEAD_LOG2).astype(jnp.int32)

        def pair(carry):
            tt, _, _ = carry
            c = 2 * (n_pairs - 1 - tt) + 1
            step(c, 1)
            go_on = alive()

            @pl.when(go_on == 1)
            def _():
                step(c - 1, 0)

            return tt + 1, c - go_on, go_on

        _, c_fin, _ = lax.while_loop(
            lambda carry: jnp.logical_and(carry[0] < n_pairs, carry[2] == 1), pair,
            (jnp.int32(0), jnp.int32(0) + n_full, jnp.int32(1)))
        k_fin = pl.multiple_of(jnp.minimum(c_fin, v_last) * ATT_BLK, ATT_BLK)
        for a in range(hps):
            values(a, c_fin % 2, k_fin, ATT_BLK)
            o_ref[pl.ds(q0, bq), a * VDIM:(a + 1) * VDIM] = acc_ref[a, :, 0:bq].T.astype(o_ref.dtype)

    def main_block(i, carry):
        q_block(pl.multiple_of(i * Q_BLK, Q_BLK), Q_BLK, 2 * i, False)
        return carry

    lax.fori_loop(0, n_main, main_block, 0)
    if tail:
        q_block(n_main * Q_BLK, tail, n_main * (Q_BLK // ATT_BLK), True)


def _att_scratch(hps, slot_stats, n_stat):
    return ([pltpu.VMEM((2, hps, ATT_BLK, Q_BLK), _f32),
             pltpu.VMEM((2, hps, ATT_BLK, Q_BLK), _bf16)]
            + [pltpu.VMEM((2, hps, 1, Q_BLK), _f32)] * slot_stats
            + [pltpu.VMEM((hps, 1, Q_BLK), _f32)] * n_stat
            + [pltpu.VMEM((hps, VDIM, Q_BLK), _f32)])


def _mla_attention(q_cat, k_nope, k_rope, v, heads):
    b, lp, _ = q_cat.shape
    hps = MLA_HEADS_PER_STEP if heads % MLA_HEADS_PER_STEP == 0 else 1
    q_spec = pl.BlockSpec((None, lp, hps * QK_CAT), lambda bi, h: (bi, 0, h))
    kr_spec = pl.BlockSpec((None, lp, LANE), lambda bi, h: (bi, 0, 0))
    v_spec = pl.BlockSpec((None, lp, hps * VDIM), lambda bi, h: (bi, 0, h))
    return pl.pallas_call(
        functools.partial(_mla_kernel, lp=lp, hps=hps),
        grid=(b, heads // hps),
        in_specs=[q_spec, v_spec, kr_spec, v_spec],
        out_specs=v_spec,
        out_shape=jax.ShapeDtypeStruct((b, lp, heads * VDIM), _bf16),
        scratch_shapes=_att_scratch(hps, 1, 2),
        compiler_params=_params("parallel", "parallel"),
        name="mla_attention",
    )(q_cat, k_nope, k_rope, v)


def _sb_attention(big, heads, q_col, k_col, v_col):
    b, lp, _ = big.shape
    hps = SB_HEADS_PER_STEP if heads % SB_HEADS_PER_STEP == 0 else 1
    w = hps * VDIM

    def spec(col0):
        return pl.BlockSpec((None, lp, w), lambda bi, h: (bi, 0, col0 // w + h))

    return pl.pallas_call(
        functools.partial(_sb_kernel, lp=lp, hps=hps),
        grid=(b, heads // hps),
        in_specs=[spec(q_col), spec(k_col), spec(v_col)],
        out_specs=pl.BlockSpec((None, lp, w), lambda bi, h: (bi, 0, h)),
        out_shape=jax.ShapeDtypeStruct((b, lp, heads * VDIM), _bf16),
        scratch_shapes=_att_scratch(hps, 0, 1),
        compiler_params=_params("parallel", "parallel"),
        name="sb_attention",
    )(big, big, big)


def _gate_kernel(ym_ref, ys_ref, zm_ref, zs_ref, gm_ref, gs_ref, o_ref, *, w):
    rows = BF16_SUBLANE

    def one(y_ref, z_ref, g_ref, r0):
        z = z_ref[pl.ds(r0, rows), :].astype(_f32)
        silu = z / (1.0 + jnp.exp(-z))
        return (_rms(y_ref[pl.ds(r0, rows), :].astype(_f32), g_ref[...]) * silu).astype(o_ref.dtype)

    def group(i, carry):
        r0 = pl.multiple_of(i * rows, rows)
        o_ref[pl.ds(r0, rows), 0:w] = one(ym_ref, zm_ref, gm_ref, r0)
        o_ref[pl.ds(r0, rows), w:2 * w] = one(ys_ref, zs_ref, gs_ref, r0)
        return carry

    lax.fori_loop(0, o_ref.shape[0] // rows, group, 0, unroll=2)


def _gate(y_mla, y_sb, big, g_mla, g_sb, zm_col, zs_col, tm):
    m, w = y_mla.shape
    yspec = pl.BlockSpec((tm, w), lambda i: (i, 0))
    gspec = pl.BlockSpec((1, w), lambda i: (0, 0))
    return pl.pallas_call(
        functools.partial(_gate_kernel, w=w),
        grid=(m // tm,),
        in_specs=[yspec, yspec,
                  pl.BlockSpec((tm, w), lambda i: (i, zm_col // w)),
                  pl.BlockSpec((tm, w), lambda i: (i, zs_col // w)),
                  gspec, gspec],
        out_specs=pl.BlockSpec((tm, 2 * w), lambda i: (i, 0)),
        out_shape=jax.ShapeDtypeStruct((m, 2 * w), _bf16),
        compiler_params=_params("parallel"),
        name="gate",
    )(y_mla, y_sb, big, big, g_mla.reshape(1, w), g_sb.reshape(1, w))


def _rope_tables(lp):
    inv_freq = ROPE_THETA ** (-jnp.arange(0, ROPE, 2, dtype=_f32) / ROPE)
    ang = jnp.arange(lp, dtype=jnp.int32).astype(_f32)[:, None] * inv_freq[None, :]
    cos, sin = jnp.cos(ang), jnp.sin(ang)
    zeros = jnp.zeros((lp, LANE - ROPE), _f32)
    return (jnp.concatenate([cos, cos, zeros], axis=1),
            jnp.concatenate([-sin, sin, zeros], axis=1))


@jax.jit
def _forward(x, meta_tokens, g_norm, w_in, g_q, g_kv, w_uq, w_ukv, g_out_mla, g_out_sb,
             w_o, g_final):
    b, seq, d = x.shape
    depth = w_in.shape[0]
    ql, kvl = g_q.shape[1], g_kv.shape[1]
    heads = w_uq.shape[2] // (NOPE + ROPE)
    w_grp = heads * VDIM
    assert w_ukv.shape[2] == heads * (NOPE + VDIM)
    assert w_in.shape[2] == ql + kvl + ROPE + 5 * w_grp
    assert ql % kvl == 0 and (ql + kvl) % LANE == 0 and kvl % LANE == 0

    l_real = N_META + seq
    lp = -(-l_real // LANE) * LANE
    mp = b * lp

    n_small = ql + kvl + ROPE
    n_in = w_in.shape[2]
    w_rows = jnp.swapaxes(w_in, 1, 2).astype(_bf16).reshape(depth * n_in, d)
    n_small_win = ql + kvl + LANE
    zm_col, qs_col, ks_col, vs_col, zs_col = (i * w_grp for i in range(5))
    colscale = jnp.ones((1, 5 * w_grp), _f32).at[:, qs_col:qs_col + w_grp].set(SB_SCALE2)
    wq = w_uq.reshape(depth, ql, heads, NOPE + ROPE)
    wq = jnp.pad(wq, ((0, 0), (0, 0), (0, 0), (0, QK_CAT - NOPE - ROPE)))
    wq = wq.reshape(depth, ql, heads * QK_CAT).astype(_bf16)
    wkv = w_ukv.astype(_bf16)
    wo = w_o.astype(_bf16)
    cos, sin = _rope_tables(lp)

    meta = jnp.broadcast_to(meta_tokens[None].astype(x.dtype), (b, N_META, d))
    h = jnp.concatenate([meta, x, jnp.zeros((b, lp - l_real, d), x.dtype)], axis=1)
    h = h.reshape(mp, d)

    tm_norm = _pick_tile(lp, 384)
    tm_mm = _pick_tile(lp, 528)
    tm_proj = _pick_tile(lp, 1056)
    tn_big = 1024 if (5 * w_grp) % 1024 == 0 else 512
    tn_out = 1024 if d % 1024 == 0 else 512

    u = _rmsnorm(h, g_norm[0], _bf16, tm_norm)
    rs = jnp.ones((mp, LANE), _f32)
    for i in range(depth):
        small = _matmul_small(u, rs, w_rows, i * n_in, n_small_win, tm_mm)
        big = _matmul_colscale(u, rs, w_rows, i * n_in + n_small, 5 * w_grp, colscale, _bf16,
                               tm_proj, tn_big)
        q_cat = _q_proj(small, g_q[i], wq, i, cos, sin, lp, tm_proj, ql)
        k_nope, v_mla, k_rope = _kv_proj(small, g_kv[i], wkv, i, cos, sin, lp, tm_proj, ql, kvl)
        y_mla = _mla_attention(q_cat.reshape(b, lp, -1), k_nope.reshape(b, lp, -1),
                               k_rope.reshape(b, lp, -1), v_mla.reshape(b, lp, -1), heads)
        y_sb = _sb_attention(big.reshape(b, lp, -1), heads, qs_col, ks_col, vs_col)
        y = _gate(y_mla.reshape(mp, w_grp), y_sb.reshape(mp, w_grp), big,
                  g_out_mla[i], g_out_sb[i], zm_col, zs_col, tm_norm)
        h, u, ss = _matmul_residual(y, wo, i, h, g_norm[(i + 1) % depth], tm_mm, tn_out)
        rs = jnp.broadcast_to(lax.rsqrt(jnp.sum(ss[:, :, :1], axis=0) / d + EPS), (mp, LANE))

    return _final_norm(h.reshape(b, lp, d), g_final, seq, x.dtype, _pick_tile(seq, 512))


def kernel(x, meta_tokens, g_norm, w_in, g_q, g_kv, w_uq, w_ukv, g_out_mla, g_out_sb, w_o, g_final):
    return _forward(x, meta_tokens, g_norm, w_in, g_q, g_kv, w_uq, w_ukv, g_out_mla, g_out_sb,
                    w_o, g_final)
```

```python
import functools
import math

import jax
import jax.numpy as jnp
from jax import lax
from jax.experimental import pallas as pl
from jax.experimental.pallas import tpu as pltpu

N_META = 16
NOPE = 128
ROPE = 64
VDIM = 128
QK_CAT = 256
ROPE_THETA = 10000.0
EPS = 1e-6
LOG2E = math.log2(math.e)
MLA_SCALE2 = LOG2E / math.sqrt(NOPE + ROPE)
SB_SCALE2 = LOG2E / math.sqrt(VDIM)
LANE = 128
F32_SUBLANE = 8
BF16_SUBLANE = 16
ATT_BLK = 256
Q_BLK = 2 * ATT_BLK
MLA_HEADS_PER_STEP = 4
SB_HEADS_PER_STEP = 4
VMEM_LIMIT = 56 * 1024 * 1024
NEG_BIG = -1e30
SB_DEAD_LOG2 = -160.0

_f32 = jnp.float32
_bf16 = jnp.bfloat16


def _pick_tile(n, target):
    best = None
    for t in range(BF16_SUBLANE, min(n, target) + 1, BF16_SUBLANE):
        if n % t == 0:
            best = t
    assert best is not None, (n, target)
    return best


def _params(*sem):
    return pltpu.CompilerParams(dimension_semantics=sem, vmem_limit_bytes=VMEM_LIMIT)


def _rms(xf, g):
    return xf * lax.rsqrt(jnp.mean(xf * xf, axis=-1, keepdims=True) + EPS) * g


def _rmsnorm_kernel(x_ref, g_ref, o_ref):
    o_ref[...] = _rms(x_ref[...].astype(_f32), g_ref[...]).astype(o_ref.dtype)


def _rmsnorm(x, g, out_dtype, tm):
    m, d = x.shape
    return pl.pallas_call(
        _rmsnorm_kernel,
        grid=(m // tm,),
        in_specs=[pl.BlockSpec((tm, d), lambda i: (i, 0)),
                  pl.BlockSpec((1, d), lambda i: (0, 0))],
        out_specs=pl.BlockSpec((tm, d), lambda i: (i, 0)),
        out_shape=jax.ShapeDtypeStruct((m, d), out_dtype),
        compiler_params=_params("parallel"),
        name="rmsnorm",
    )(x, g.reshape(1, d))


def _final_norm(h, g, seq, out_dtype, tm):
    b, lp, d = h.shape
    sub = F32_SUBLANE
    assert N_META % sub == 0 and lp % sub == 0 and tm % sub == 0

    return pl.pallas_call(
        _rmsnorm_kernel,
        grid=(b, seq // tm),
        in_specs=[pl.BlockSpec((pl.Element(tm), pl.Element(d)),
                               lambda bi, i: (pl.multiple_of(bi * lp + N_META + i * tm, sub), 0)),
                  pl.BlockSpec((1, d), lambda bi, i: (0, 0))],
        out_specs=pl.BlockSpec((None, tm, d), lambda bi, i: (bi, i, 0)),
        out_shape=jax.ShapeDtypeStruct((b, seq, d), out_dtype),
        compiler_params=_params("parallel", "parallel"),
        name="final_norm",
    )(h.reshape(b * lp, d), g.reshape(1, d))


def _row_scale(rs_ref, n):
    return jnp.tile(rs_ref[...], (1, n // LANE))


def _mm_scale_kernel(x_ref, w_ref, s_ref, rs_ref, o_ref):
    acc = lax.dot_general(x_ref[...], w_ref[...], (((1,), (1,)), ((), ())),
                          preferred_element_type=_f32)
    o_ref[...] = (acc * _row_scale(rs_ref, acc.shape[1]) * s_ref[...]).astype(o_ref.dtype)


def _matmul_colscale(x, rs, w_rows, row0, n, colscale, out_dtype, tm, tn):
    m, k = x.shape
    assert row0 % BF16_SUBLANE == 0 and tn % BF16_SUBLANE == 0
    return pl.pallas_call(
        _mm_scale_kernel,
        grid=(n // tn, m // tm),
        in_specs=[pl.BlockSpec((tm, k), lambda j, i: (i, 0)),
                  pl.BlockSpec((pl.Element(tn), pl.Element(k)),
                               lambda j, i: (pl.multiple_of(row0 + j * tn, BF16_SUBLANE), 0)),
                  pl.BlockSpec((1, tn), lambda j, i: (0, j)),
                  pl.BlockSpec((tm, LANE), lambda j, i: (i, 0))],
        out_specs=pl.BlockSpec((tm, tn), lambda j, i: (i, j)),
        out_shape=jax.ShapeDtypeStruct((m, n), out_dtype),
        compiler_params=_params("parallel", "parallel"),
        name="matmul_colscale",
    )(x, w_rows, colscale, rs)


def _mm_kernel(x_ref, w_ref, rs_ref, o_ref):
    acc = lax.dot_general(x_ref[...], w_ref[...], (((1,), (1,)), ((), ())),
                          preferred_element_type=_f32)
    o_ref[...] = (acc * _row_scale(rs_ref, acc.shape[1])).astype(o_ref.dtype)


def _matmul_small(x, rs, w_rows, row0, n, tm):
    m, k = x.shape
    assert row0 % BF16_SUBLANE == 0
    return pl.pallas_call(
        _mm_kernel,
        grid=(m // tm,),
        in_specs=[pl.BlockSpec((tm, k), lambda i: (i, 0)),
                  pl.BlockSpec((pl.Element(n), pl.Element(k)), lambda i: (row0, 0)),
                  pl.BlockSpec((tm, LANE), lambda i: (i, 0))],
        out_specs=pl.BlockSpec((tm, n), lambda i: (i, 0)),
        out_shape=jax.ShapeDtypeStruct((m, n), _f32),
        compiler_params=_params("parallel"),
        name="matmul_small",
    )(x, w_rows, rs)


def _mm_res_kernel(x_ref, w_ref, r_ref, g_ref, o_ref, u_ref, ss_ref):
    h = r_ref[...] + jnp.dot(x_ref[...], w_ref[...], preferred_element_type=_f32)
    o_ref[...] = h
    u_ref[...] = (h * g_ref[...]).astype(u_ref.dtype)
    ss_ref[...] = jnp.broadcast_to(jnp.sum(h * h, axis=-1, keepdims=True), ss_ref.shape)


def _matmul_residual(x, w, layer, res, g_next, tm, tn):
    m, k = x.shape
    n = w.shape[2]
    tile = pl.BlockSpec((tm, tn), lambda j, i: (i, j))
    return pl.pallas_call(
        _mm_res_kernel,
        grid=(n // tn, m // tm),
        in_specs=[pl.BlockSpec((tm, k), lambda j, i: (i, 0)),
                  pl.BlockSpec((None, k, tn), lambda j, i: (layer, 0, j)),
                  tile,
                  pl.BlockSpec((1, tn), lambda j, i: (0, j))],
        out_specs=[tile, tile, pl.BlockSpec((None, tm, LANE), lambda j, i: (j, i, 0))],
        out_shape=[jax.ShapeDtypeStruct((m, n), _f32), jax.ShapeDtypeStruct((m, n), _bf16),
                   jax.ShapeDtypeStruct((n // tn, m, LANE), _f32)],
        compiler_params=_params("parallel", "parallel"),
        name="matmul_residual",
    )(x, w, res, g_next.reshape(1, n))


def _rope128(x, cos, sin_signed):
    lane = lax.broadcasted_iota(jnp.int32, x.shape, 1)
    partner = jnp.where(lane < ROPE // 2,
                        pltpu.roll(x, LANE - ROPE // 2, 1),
                        pltpu.roll(x, ROPE // 2, 1))
    return x * cos + partner * sin_signed


def _qproj_kernel(x_ref, g_ref, w_ref, cos_ref, sin_ref, o_ref, xn_ref, *, heads_per_step):
    @pl.when(pl.program_id(1) == 0)
    def _():
        xn_ref[...] = (_rms(x_ref[...], g_ref[...]) * MLA_SCALE2).astype(_bf16)

    acc = jnp.dot(xn_ref[...], w_ref[...], preferred_element_type=_f32)
    cos = cos_ref[...]
    sin = sin_ref[...]
    for hh in range(heads_per_step):
        c0 = hh * QK_CAT
        o_ref[:, c0:c0 + NOPE] = acc[:, c0:c0 + NOPE].astype(o_ref.dtype)
        roped = _rope128(acc[:, c0 + NOPE:c0 + QK_CAT], cos, sin)
        o_ref[:, c0 + NOPE:c0 + QK_CAT] = roped.astype(o_ref.dtype)


def _q_proj(small, g_q, w_q, layer, cos, sin, lp, tm, ql):
    m = small.shape[0]
    n = w_q.shape[2]
    hps = 4 if (n // QK_CAT) % 4 == 0 else 1
    tn = hps * QK_CAT
    nrow = lp // tm
    return pl.pallas_call(
        functools.partial(_qproj_kernel, heads_per_step=hps),
        grid=(m // tm, n // tn),
        in_specs=[pl.BlockSpec((tm, ql), lambda i, j: (i, 0)),
                  pl.BlockSpec((1, ql), lambda i, j: (0, 0)),
                  pl.BlockSpec((None, ql, tn), lambda i, j: (layer, 0, j)),
                  pl.BlockSpec((tm, LANE), lambda i, j: (i % nrow, 0)),
                  pl.BlockSpec((tm, LANE), lambda i, j: (i % nrow, 0))],
        out_specs=pl.BlockSpec((tm, tn), lambda i, j: (i, j)),
        out_shape=jax.ShapeDtypeStruct((m, n), _bf16),
        scratch_shapes=[pltpu.VMEM((tm, ql), _bf16)],
        compiler_params=_params("parallel", "arbitrary"),
        name="q_proj",
    )(small, g_q.reshape(1, ql), w_q, cos, sin)


def _kvproj_kernel(x_ref, kr_ref, g_ref, w_ref, cos_ref, sin_ref,
                   k_ref, v_ref, krr_ref, xn_ref, *, heads_per_step):
    @pl.when(pl.program_id(1) == 0)
    def _():
        xn_ref[...] = _rms(x_ref[...], g_ref[...]).astype(_bf16)
        krr_ref[...] = _rope128(kr_ref[...], cos_ref[...], sin_ref[...]).astype(krr_ref.dtype)

    kv = jnp.dot(xn_ref[...], w_ref[...], preferred_element_type=_f32)
    for hh in range(heads_per_step):
        c0 = hh * (NOPE + VDIM)
        k_ref[:, hh * NOPE:(hh + 1) * NOPE] = kv[:, c0:c0 + NOPE].astype(k_ref.dtype)
        v_ref[:, hh * VDIM:(hh + 1) * VDIM] = kv[:, c0 + NOPE:c0 + NOPE + VDIM].astype(v_ref.dtype)


def _kv_proj(small, g_kv, w_kv, layer, cos, sin, lp, tm, ql, kvl):
    m = small.shape[0]
    heads = w_kv.shape[2] // (NOPE + VDIM)
    hps = 4 if heads % 4 == 0 else 1
    nrow = lp // tm
    return pl.pallas_call(
        functools.partial(_kvproj_kernel, heads_per_step=hps),
        grid=(m // tm, heads // hps),
        in_specs=[pl.BlockSpec((tm, kvl), lambda i, j: (i, ql // kvl)),
                  pl.BlockSpec((tm, LANE), lambda i, j: (i, (ql + kvl) // LANE)),
                  pl.BlockSpec((1, kvl), lambda i, j: (0, 0)),
                  pl.BlockSpec((None, kvl, hps * (NOPE + VDIM)), lambda i, j: (layer, 0, j)),
                  pl.BlockSpec((tm, LANE), lambda i, j: (i % nrow, 0)),
                  pl.BlockSpec((tm, LANE), lambda i, j: (i % nrow, 0))],
        out_specs=[pl.BlockSpec((tm, hps * NOPE), lambda i, j: (i, j)),
                   pl.BlockSpec((tm, hps * VDIM), lambda i, j: (i, j)),
                   pl.BlockSpec((tm, LANE), lambda i, j: (i, 0))],
        out_shape=[jax.ShapeDtypeStruct((m, heads * NOPE), _bf16),
                   jax.ShapeDtypeStruct((m, heads * VDIM), _bf16),
                   jax.ShapeDtypeStruct((m, LANE), _bf16)],
        scratch_shapes=[pltpu.VMEM((tm, kvl), _bf16)],
        compiler_params=_params("parallel", "arbitrary"),
        name="kv_proj",
    )(small, small, g_kv.reshape(1, kvl), w_kv, cos, sin)


def _dot_nt(a, b):
    return lax.dot_general(a, b, (((1,), (1,)), ((), ())), preferred_element_type=_f32)


def _dot_tn(a, b):
    return lax.dot_general(a, b, (((0,), (0,)), ((), ())), preferred_element_type=_f32)


def _neg_abs(x):
    bits = lax.bitcast_convert_type(x, jnp.uint32) | jnp.uint32(0x80000000)
    return lax.bitcast_convert_type(bits, _f32)


def _row_blocks(lp):
    n_main = lp // Q_BLK
    tail = lp - n_main * Q_BLK
    assert tail <= ATT_BLK, (lp, tail)
    return n_main, tail


def _key_query_iota(bk, bq):
    key = lax.broadcasted_iota(jnp.int32, (bk, bq), 0)
    qry = lax.broadcasted_iota(jnp.int32, (bk, bq), 1)
    return key, qry


def _mla_kernel(q_ref, k_ref, kr_ref, v_ref, o_ref, s_ref, p_ref, al_ref, m_ref, l_ref, acc_ref,
                *, lp, hps):
    n_main, tail = _row_blocks(lp)

    def q_block(q0, bq, n_full, is_tail):
        qs = [q_ref[pl.ds(q0, bq), a * QK_CAT:(a + 1) * QK_CAT] for a in range(hps)]

        full = (0, bq)
        upper = (ATT_BLK, bq)

        def scores(a, k0, bk, lanes=full):
            k_cat = jnp.concatenate([k_ref[pl.ds(k0, bk), a * NOPE:(a + 1) * NOPE],
                                     kr_ref[pl.ds(k0, bk), :]], axis=1)
            return _dot_nt(k_cat, qs[a][lanes[0]:lanes[1]])

        def probs(a, s, slot, bk, mask_off, lanes=full):
            lo, hi = lanes
            if mask_off is not None:
                key, qry = _key_query_iota(bk, hi - lo)
                s = jnp.where(key + mask_off <= qry, s, NEG_BIG)
            m_prev = m_ref[a, :, lo:hi]
            m_new = jnp.maximum(m_prev, jnp.max(s, axis=0, keepdims=True))
            alpha = jnp.exp2(m_prev - m_new)
            p = jnp.exp2(s - m_new)
            l_ref[a, :, lo:hi] = alpha * l_ref[a, :, lo:hi] + jnp.sum(p, axis=0, keepdims=True)
            m_ref[a, :, lo:hi] = m_new
            p_ref[slot, a, 0:bk, lo:hi] = p.astype(_bf16)
            al_ref[slot, a, :, lo:hi] = alpha

        def values(a, slot, k0, bk, lanes=full):
            lo, hi = lanes
            pv = _dot_tn(v_ref[pl.ds(k0, bk), a * VDIM:(a + 1) * VDIM], p_ref[slot, a, 0:bk, lo:hi])
            acc_ref[a, :, lo:hi] = al_ref[slot, a, :, lo:hi] * acc_ref[a, :, lo:hi] + pv

        first = 0 if is_tail else 1
        last = n_full - 1 if is_tail else n_full + 1
        for a in range(hps):
            m_ref[a, :, 0:bq] = jnp.full((1, bq), NEG_BIG, _f32)
            l_ref[a, :, 0:bq] = jnp.zeros((1, bq), _f32)
            acc_ref[a, :, 0:bq] = jnp.zeros((VDIM, bq), _f32)
            p_ref[1 - first, a, :, 0:bq] = jnp.zeros((ATT_BLK, bq), _bf16)
            al_ref[1 - first, a, :, 0:bq] = jnp.ones((1, bq), _f32)
            s_ref[first, a, :, 0:bq] = scores(a, 0, ATT_BLK)

        def step(c, slot, mask_off=None, prefetch=full, lanes=full):
            nxt = pl.multiple_of(jnp.minimum(c + 1, last) * ATT_BLK, ATT_BLK)
            prv = pl.multiple_of(jnp.maximum(c - 1, 0) * ATT_BLK, ATT_BLK)
            for a in range(hps):
                if prefetch is not None:
                    s_ref[1 - slot, a, :, prefetch[0]:prefetch[1]] = scores(a, nxt, ATT_BLK, prefetch)
                for lo in range(lanes[0], lanes[1], ATT_BLK):
                    hi = min(lo + ATT_BLK, lanes[1])
                    off = None if mask_off is None else mask_off - (lo - lanes[0])
                    probs(a, s_ref[slot, a, :, lo:hi], slot, ATT_BLK, off, (lo, hi))
                values(a, 1 - slot, prv, ATT_BLK)

        def pair(t, carry):
            step(2 * t, first)
            step(2 * t + 1, 1 - first)
            return carry

        lax.fori_loop(0, n_full // 2, pair, 0)
        prv = pl.multiple_of(jnp.maximum(n_full - 1, 0) * ATT_BLK, ATT_BLK)
        if is_tail:
            for a in range(hps):
                probs(a, scores(a, q0, bq), 0, bq, 0)
                values(a, 1, prv, ATT_BLK)
                values(a, 0, q0, bq)
        else:
            step(n_full, 1, mask_off=0, prefetch=upper)
            step(n_full + 1, 0, mask_off=0, prefetch=None, lanes=upper)
            k_last = pl.multiple_of((n_full + 1) * ATT_BLK, ATT_BLK)
            for a in range(hps):
                values(a, 0, k_last, ATT_BLK, upper)
        for a in range(hps):
            out_t = acc_ref[a, :, 0:bq] * (1.0 / l_ref[a, :, 0:bq])
            o_ref[pl.ds(q0, bq), a * VDIM:(a + 1) * VDIM] = out_t.T.astype(o_ref.dtype)

    def main_block(i, carry):
        q_block(pl.multiple_of(i * Q_BLK, Q_BLK), Q_BLK, 2 * i, False)
        return carry

    lax.fori_loop(0, n_main, main_block, 0)
    if tail:
        q_block(n_main * Q_BLK, tail, n_main * (Q_BLK // ATT_BLK), True)


def _sb_kernel(q_ref, k_ref, v_ref, o_ref, s_ref, w_ref, r_ref, acc_ref, *, lp, hps):
    n_main, tail = _row_blocks(lp)
    key_i, key_j = _key_query_iota(ATT_BLK, ATT_BLK)
    later = (key_j > key_i).astype(_bf16)

    def q_block(q0, bq, n_full, is_tail):
        qs = [q_ref[pl.ds(q0, bq), a * VDIM:(a + 1) * VDIM] for a in range(hps)]

        def scores(a, k0, bk, lanes=None):
            q = qs[a] if lanes is None else qs[a][lanes[0]:lanes[1]]
            return _dot_nt(k_ref[pl.ds(k0, bk), a * VDIM:(a + 1) * VDIM], q)

        full = (0, bq)
        upper = (ATT_BLK, bq)

        def weights(a, z_of, slot, bk, mask_off, lanes=full):
            for lo in range(lanes[0], lanes[1], ATT_BLK):
                hi = min(lo + ATT_BLK, lanes[1])
                z = z_of(lo, hi)
                tail_sp = jnp.log(1.0 + jnp.exp2(_neg_abs(z))) * LOG2E
                log_beta = jnp.minimum(z, 0.0) - tail_sp
                log_rest = log_beta - z
                masked = mask_off is not None
                if masked:
                    key, qry = _key_query_iota(bk, hi - lo)
                    strict = key + (mask_off - lo) < qry
                    log_rest = jnp.where(strict, log_rest, 0.0)
                within = jnp.dot(later[0:bk, 0:bk], log_rest.astype(_bf16),
                                 preferred_element_type=_f32)
                r = r_ref[a, :, lo:hi]
                w = jnp.exp2(log_beta + within + r)
                if masked:
                    w = jnp.where(strict, w, 0.0)
                w_ref[slot, a, 0:bk, lo:hi] = w.astype(_bf16)
                r_ref[a, :, lo:hi] = r + within[0:1, :] + log_rest[0:1, :]

        def from_value(z, lane0=0):
            return lambda lo, hi: z[:, lo - lane0:hi - lane0]

        def values(a, slot, k0, bk, lanes=full):
            lo, hi = lanes
            acc_ref[a, :, lo:hi] = acc_ref[a, :, lo:hi] + _dot_tn(
                v_ref[pl.ds(k0, bk), a * VDIM:(a + 1) * VDIM], w_ref[slot, a, 0:bk, lo:hi])

        v_last = n_full - 1 if is_tail else n_full + 1

        def step(c, slot, mask_off=None, v_lanes=full):
            k_before = pl.multiple_of(jnp.maximum(c - 1, 0) * ATT_BLK, ATT_BLK)
            k_after = pl.multiple_of(jnp.minimum(c + 1, v_last) * ATT_BLK, ATT_BLK)
            for a in range(hps):
                s_ref[1 - slot, a, :, 0:bq] = scores(a, k_before, ATT_BLK)
                weights(a, lambda lo, hi: s_ref[slot, a, :, lo:hi], slot, ATT_BLK, mask_off)
                values(a, 1 - slot, k_after, ATT_BLK, v_lanes)

        k_top = pl.multiple_of(jnp.maximum(n_full - 1, 0) * ATT_BLK, ATT_BLK)
        for a in range(hps):
            r_ref[a, :, 0:bq] = jnp.zeros((1, bq), _f32)
            acc_ref[a, :, 0:bq] = jnp.zeros((VDIM, bq), _f32)
            if is_tail:
                s_ref[1, a, :, 0:bq] = scores(a, k_top, ATT_BLK)
                weights(a, from_value(scores(a, q0, bq)), 0, bq, 0)
                values(a, 0, q0, bq)
                w_ref[0, a, :, 0:bq] = jnp.zeros((ATT_BLK, bq), _bf16)
            else:
                k_hi = pl.multiple_of((n_full + 1) * ATT_BLK, ATT_BLK)
                s_ref[0, a, :, 0:bq] = scores(a, q0, ATT_BLK)
                weights(a, from_value(scores(a, k_hi, ATT_BLK, upper), ATT_BLK), 1, ATT_BLK,
                        ATT_BLK, upper)
        if not is_tail:
            step(n_full, 0, mask_off=0, v_lanes=upper)

        n_pairs = n_full // 2

        def alive():
            r_max = r_ref[0, :, 0:bq]
            for a in range(1, hps):
                r_max = jnp.maximum(r_max, r_ref[a, :, 0:bq])
            return (jnp.max(r_max) > SB_DEAD_LOG2).astype(jnp.int32)

        def pair(carry):
            tt, _, _ = carry
            c = 2 * (n_pairs - 1 - tt) + 1
            step(c, 1)
            go_on = alive()

            @pl.when(go_on == 1)
            def _():
                step(c - 1, 0)

            return tt + 1, c - go_on, go_on

        _, c_fin, _ = lax.while_loop(
            lambda carry: jnp.logical_and(carry[0] < n_pairs, carry[2] == 1), pair,
            (jnp.int32(0), jnp.int32(0) + n_full, jnp.int32(1)))
        k_fin = pl.multiple_of(jnp.minimum(c_fin, v_last) * ATT_BLK, ATT_BLK)
        for a in range(hps):
            values(a, c_fin % 2, k_fin, ATT_BLK)
            o_ref[pl.ds(q0, bq), a * VDIM:(a + 1) * VDIM] = acc_ref[a, :, 0:bq].T.astype(o_ref.dtype)

    def main_block(i, carry):
        q_block(pl.multiple_of(i * Q_BLK, Q_BLK), Q_BLK, 2 * i, False)
        return carry

    lax.fori_loop(0, n_main, main_block, 0)
    if tail:
        q_block(n_main * Q_BLK, tail, n_main * (Q_BLK // ATT_BLK), True)


def _att_scratch(hps, slot_stats, n_stat):
    return ([pltpu.VMEM((2, hps, ATT_BLK, Q_BLK), _f32),
             pltpu.VMEM((2, hps, ATT_BLK, Q_BLK), _bf16)]
            + [pltpu.VMEM((2, hps, 1, Q_BLK), _f32)] * slot_stats
            + [pltpu.VMEM((hps, 1, Q_BLK), _f32)] * n_stat
            + [pltpu.VMEM((hps, VDIM, Q_BLK), _f32)])


def _mla_attention(q_cat, k_nope, k_rope, v, heads):
    b, lp, _ = q_cat.shape
    hps = MLA_HEADS_PER_STEP if heads % MLA_HEADS_PER_STEP == 0 else 1
    q_spec = pl.BlockSpec((None, lp, hps * QK_CAT), lambda bi, h: (bi, 0, h))
    kr_spec = pl.BlockSpec((None, lp, LANE), lambda bi, h: (bi, 0, 0))
    v_spec = pl.BlockSpec((None, lp, hps * VDIM), lambda bi, h: (bi, 0, h))
    return pl.pallas_call(
        functools.partial(_mla_kernel, lp=lp, hps=hps),
        grid=(b, heads // hps),
        in_specs=[q_spec, v_spec, kr_spec, v_spec],
        out_specs=v_spec,
        out_shape=jax.ShapeDtypeStruct((b, lp, heads * VDIM), _bf16),
        scratch_shapes=_att_scratch(hps, 1, 2),
        compiler_params=_params("parallel", "parallel"),
        name="mla_attention",
    )(q_cat, k_nope, k_rope, v)


def _sb_attention(big, heads, q_col, k_col, v_col):
    b, lp, _ = big.shape
    hps = SB_HEADS_PER_STEP if heads % SB_HEADS_PER_STEP == 0 else 1
    w = hps * VDIM

    def spec(col0):
        return pl.BlockSpec((None, lp, w), lambda bi, h: (bi, 0, col0 // w + h))

    return pl.pallas_call(
        functools.partial(_sb_kernel, lp=lp, hps=hps),
        grid=(b, heads // hps),
        in_specs=[spec(q_col), spec(k_col), spec(v_col)],
        out_specs=pl.BlockSpec((None, lp, w), lambda bi, h: (bi, 0, h)),
        out_shape=jax.ShapeDtypeStruct((b, lp, heads * VDIM), _bf16),
        scratch_shapes=_att_scratch(hps, 0, 1),
        compiler_params=_params("parallel", "parallel"),
        name="sb_attention",
    )(big, big, big)


def _gate_kernel(ym_ref, ys_ref, zm_ref, zs_ref, gm_ref, gs_ref, o_ref, *, w):
    rows = BF16_SUBLANE

    def one(y_ref, z_ref, g_ref, r0):
        z = z_ref[pl.ds(r0, rows), :].astype(_f32)
        silu = z / (1.0 + jnp.exp(-z))
        return (_rms(y_ref[pl.ds(r0, rows), :].astype(_f32), g_ref[...]) * silu).astype(o_ref.dtype)

    def group(i, carry):
        r0 = pl.multiple_of(i * rows, rows)
        o_ref[pl.ds(r0, rows), 0:w] = one(ym_ref, zm_ref, gm_ref, r0)
        o_ref[pl.ds(r0, rows), w:2 * w] = one(ys_ref, zs_ref, gs_ref, r0)
        return carry

    lax.fori_loop(0, o_ref.shape[0] // rows, group, 0, unroll=2)


def _gate(y_mla, y_sb, big, g_mla, g_sb, zm_col, zs_col, tm):
    m, w = y_mla.shape
    yspec = pl.BlockSpec((tm, w), lambda i: (i, 0))
    gspec = pl.BlockSpec((1, w), lambda i: (0, 0))
    return pl.pallas_call(
        functools.partial(_gate_kernel, w=w),
        grid=(m // tm,),
        in_specs=[yspec, yspec,
                  pl.BlockSpec((tm, w), lambda i: (i, zm_col // w)),
                  pl.BlockSpec((tm, w), lambda i: (i, zs_col // w)),
                  gspec, gspec],
        out_specs=pl.BlockSpec((tm, 2 * w), lambda i: (i, 0)),
        out_shape=jax.ShapeDtypeStruct((m, 2 * w), _bf16),
        compiler_params=_params("parallel"),
        name="gate",
    )(y_mla, y_sb, big, big, g_mla.reshape(1, w), g_sb.reshape(1, w))


def _rope_tables(lp):
    inv_freq = ROPE_THETA ** (-jnp.arange(0, ROPE, 2, dtype=_f32) / ROPE)
    ang = jnp.arange(lp, dtype=jnp.int32).astype(_f32)[:, None] * inv_freq[None, :]
    cos, sin = jnp.cos(ang), jnp.sin(ang)
    zeros = jnp.zeros((lp, LANE - ROPE), _f32)
    return (jnp.concatenate([cos, cos, zeros], axis=1),
            jnp.concatenate([-sin, sin, zeros], axis=1))


@jax.jit
def _forward(x, meta_tokens, g_norm, w_in, g_q, g_kv, w_uq, w_ukv, g_out_mla, g_out_sb,
             w_o, g_final):
    b, seq, d = x.shape
    depth = w_in.shape[0]
    ql, kvl = g_q.shape[1], g_kv.shape[1]
    heads = w_uq.shape[2] // (NOPE + ROPE)
    w_grp = heads * VDIM
    assert w_ukv.shape[2] == heads * (NOPE + VDIM)
    assert w_in.shape[2] == ql + kvl + ROPE + 5 * w_grp
    assert ql % kvl == 0 and (ql + kvl) % LANE == 0 and kvl % LANE == 0

    l_real = N_META + seq
    lp = -(-l_real // LANE) * LANE
    mp = b * lp

    n_small = ql + kvl + ROPE
    n_in = w_in.shape[2]
    w_rows = jnp.swapaxes(w_in, 1, 2).astype(_bf16).reshape(depth * n_in, d)
    n_small_win = ql + kvl + LANE
    zm_col, qs_col, ks_col, vs_col, zs_col = (i * w_grp for i in range(5))
    colscale = jnp.ones((1, 5 * w_grp), _f32).at[:, qs_col:qs_col + w_grp].set(SB_SCALE2)
    wq = w_uq.reshape(depth, ql, heads, NOPE + ROPE)
    wq = jnp.pad(wq, ((0, 0), (0, 0), (0, 0), (0, QK_CAT - NOPE - ROPE)))
    wq = wq.reshape(depth, ql, heads * QK_CAT).astype(_bf16)
    wkv = w_ukv.astype(_bf16)
    wo = w_o.astype(_bf16)
    cos, sin = _rope_tables(lp)

    meta = jnp.broadcast_to(meta_tokens[None].astype(x.dtype), (b, N_META, d))
    h = jnp.concatenate([meta, x, jnp.zeros((b, lp - l_real, d), x.dtype)], axis=1)
    h = h.reshape(mp, d)

    tm_norm = _pick_tile(lp, 384)
    tm_mm = _pick_tile(lp, 528)
    tm_proj = _pick_tile(lp, 1056)
    tn_big = 1024 if (5 * w_grp) % 1024 == 0 else 512
    tn_out = 1024 if d % 1024 == 0 else 512

    u = _rmsnorm(h, g_norm[0], _bf16, tm_norm)
    rs = jnp.ones((mp, LANE), _f32)
    for i in range(depth):
        small = _matmul_small(u, rs, w_rows, i * n_in, n_small_win, tm_mm)
        big = _matmul_colscale(u, rs, w_rows, i * n_in + n_small, 5 * w_grp, colscale, _bf16,
                               tm_proj, tn_big)
        q_cat = _q_proj(small, g_q[i], wq, i, cos, sin, lp, tm_proj, ql)
        k_nope, v_mla, k_rope = _kv_proj(small, g_kv[i], wkv, i, cos, sin, lp, tm_proj, ql, kvl)
        y_mla = _mla_attention(q_cat.reshape(b, lp, -1), k_nope.reshape(b, lp, -1),
                               k_rope.reshape(b, lp, -1), v_mla.reshape(b, lp, -1), heads)
        y_sb = _sb_attention(big.reshape(b, lp, -1), heads, qs_col, ks_col, vs_col)
        y = _gate(y_mla.reshape(mp, w_grp), y_sb.reshape(mp, w_grp), big,
                  g_out_mla[i], g_out_sb[i], zm_col, zs_col, tm_norm)
        h, u, ss = _matmul_residual(y, wo, i, h, g_norm[(i + 1) % depth], tm_mm, tn_out)
        rs = jnp.broadcast_to(lax.rsqrt(jnp.sum(ss[:, :, :1], axis=0) / d + EPS), (mp, LANE))

    return _final_norm(h.reshape(b, lp, d), g_final, seq, x.dtype, _pick_tile(seq, 512))


def kernel(x, meta_tokens, g_norm, w_in, g_q, g_kv, w_uq, w_ukv, g_out_mla, g_out_sb, w_o, g_final):
    return _forward(x, meta_tokens, g_norm, w_in, g_q, g_kv, w_uq, w_ukv, g_out_mla, g_out_sb,
                    w_o, g_final)
```
